```python
import math
import jax, jax.numpy as jnp
from jax import lax
import numpy as np

D_MODEL = 1024
BATCH = 32
SEQ = 256
DEPTH = 4
DEC_BATCH = 8
DEC_SEQ = 1024
PAST_LEN = 512

GRID_W = 64
N_MIXERS = 3
LAYER_KINDS = ('A', 'B', 'C', 'A')
A_HEADS = 16
A_KV_HEADS = 4
A_HEAD_DIM = 64
A_GROUP = A_HEADS // A_KV_HEADS
WINDOW = 128
C_HEADS = 8
C_KV_HEADS = 4
C_HEAD_DIM = 128
C_GROUP = C_HEADS // C_KV_HEADS
Q_BLOCK = 128
S5_GROUP = 16
S5_GROUPS = D_MODEL // S5_GROUP
S5_STATE = 64
DT_MIN = 1e-3
DT_MAX = 1e-1
D_FF = 2816
CONV_W = 3
ROPE_THETA = 10000.0
EPS = 1e-6

kernel_name = 'hybrid_diffusion_step'


def rms_norm(x, gain):
    xf = x.astype(jnp.float32)
    y = xf * lax.rsqrt(jnp.mean(xf * xf, axis=-1, keepdims=True) + EPS)
    return (y * gain.astype(jnp.float32)).astype(x.dtype)


def ada_params(cond, w_mod, b_mod):
    m = jax.nn.silu(cond) @ w_mod + b_mod
    return jnp.split(m[..., None, :], 6, axis=-1)


def sublayer_in(x, gain, shift, scale):
    return rms_norm(x, gain) * (1 + scale) + shift


def grid_positions(n_tokens):
    rows = n_tokens // GRID_W
    row = jnp.repeat(jnp.arange(rows), GRID_W)
    col = jnp.tile(jnp.arange(GRID_W), rows)
    return row, col


def axial_rope(x, row, col):
    dh = x.shape[-1]
    half = dh // 2
    quarter = half // 2
    freqs = 1.0 / (ROPE_THETA ** (jnp.arange(quarter, dtype=jnp.float32) / quarter))

    def rot(xa, pos):
        ang = pos.astype(jnp.float32)[:, None] * freqs[None, :]
        cos = jnp.cos(ang)[None, :, None, :].astype(x.dtype)
        sin = jnp.sin(ang)[None, :, None, :].astype(x.dtype)
        x1, x2 = xa[..., :quarter], xa[..., quarter:]
        return jnp.concatenate([x1 * cos - x2 * sin, x1 * sin + x2 * cos], axis=-1)

    return jnp.concatenate([rot(x[..., :half], row), rot(x[..., half:], col)], axis=-1)


def qkv_heads(x, p, n_heads, n_kv, dh):
    b, s, _ = x.shape
    qkv = x @ p['w_qkv']
    q = qkv[..., :n_heads * dh].reshape(b, s, n_heads, dh)
    k = qkv[..., n_heads * dh:(n_heads + n_kv) * dh].reshape(b, s, n_kv, dh)
    v = qkv[..., (n_heads + n_kv) * dh:].reshape(b, s, n_kv, dh)
    return rms_norm(q, p['q_norm']), rms_norm(k, p['k_norm']), v


def attend_block(q, k, v, valid, sink):
    scale = q.shape[-1] ** -0.5
    s = jnp.einsum('bqkgd,bskd->bkgqs', q, k, preferred_element_type=jnp.float32) * scale
    if valid is not None:
        s = jnp.where(valid, s, -jnp.inf)
    if sink is not None:
        sk = jnp.broadcast_to(sink.astype(jnp.float32)[None, :, :, None, None], s.shape[:-1] + (1,))
        probs = jax.nn.softmax(jnp.concatenate([s, sk], axis=-1), axis=-1)[..., :-1]
    else:
        probs = jax.nn.softmax(s, axis=-1)
    return jnp.einsum('bkgqs,bskd->bqkgd', probs.astype(v.dtype), v)


def sweep_query_blocks(q, block_fn):
    b, s = q.shape[:2]
    nb = s // Q_BLOCK
    qb = q.reshape((b, nb, Q_BLOCK) + q.shape[2:]).swapaxes(0, 1)
    out = lax.map(lambda args: block_fn(args[0], args[1]), (qb, jnp.arange(nb)))
    return out.swapaxes(0, 1).reshape(q.shape)


def attn_context(x, p, n_heads, n_kv, dh, sink):
    b, s, _ = x.shape
    q, k, v = qkv_heads(x, p, n_heads, n_kv, dh)
    qg = q.reshape(b, s, n_kv, n_heads // n_kv, dh)
    o = sweep_query_blocks(qg, lambda qb, i: attend_block(qb, k, v, None, sink))
    return o.reshape(b, s, n_heads * dh) @ p['w_o'], (k, v)


def window_attn_latent(x, p, ck, cv, row, col):
    b, s, _ = x.shape
    n_ctx = ck.shape[1]
    q, k, v = qkv_heads(x, p, A_HEADS, A_KV_HEADS, A_HEAD_DIM)
    q = axial_rope(q, row, col)
    k = axial_rope(k, row, col)
    qg = q.reshape(b, s, A_KV_HEADS, A_GROUP, A_HEAD_DIM)
    pad = ((0, 0), (WINDOW, WINDOW), (0, 0), (0, 0))
    kp = jnp.pad(k, pad)
    vp = jnp.pad(v, pad)
    band = Q_BLOCK + 2 * WINDOW
    sink = p['sink'].reshape(A_KV_HEADS, A_GROUP)
    ctx_valid = jnp.ones((Q_BLOCK, n_ctx), dtype=bool)

    def block(qb, i):
        start = i * Q_BLOCK
        kb = lax.dynamic_slice_in_dim(kp, start, band, axis=1)
        vb = lax.dynamic_slice_in_dim(vp, start, band, axis=1)
        qpos = start + jnp.arange(Q_BLOCK)
        kpos = start - WINDOW + jnp.arange(band)
        band_valid = ((jnp.abs(qpos[:, None] - kpos[None, :]) <= WINDOW)
                      & (kpos[None, :] >= 0) & (kpos[None, :] < s))
        valid = jnp.concatenate([ctx_valid, band_valid], axis=1)
        return attend_block(qb, jnp.concatenate([ck, kb], axis=1),
                            jnp.concatenate([cv, vb], axis=1), valid, sink)

    o = sweep_query_blocks(qg, block)
    return o.reshape(b, s, A_HEADS * A_HEAD_DIM) @ p['w_o']


def full_attn_latent(x, p, ck, cv, row, col):
    b, s, _ = x.shape
    q, k, v = qkv_heads(x, p, C_HEADS, C_KV_HEADS, C_HEAD_DIM)
    q = axial_rope(q, row, col)
    k = axial_rope(k, row, col)
    qg = q.reshape(b, s, C_KV_HEADS, C_GROUP, C_HEAD_DIM)
    k_all = jnp.concatenate([ck, k], axis=1)
    v_all = jnp.concatenate([cv, v], axis=1)
    o = sweep_query_blocks(qg, lambda qb, i: attend_block(qb, k_all, v_all, None, None))
    return o.reshape(b, s, C_HEADS * C_HEAD_DIM) @ p['w_o']


def s5_discretize(lam_re, lam_im, log_dt, b_re, b_im):
    lam_re = lam_re.astype(jnp.float32)
    lam_im = lam_im.astype(jnp.float32)
    dt = jnp.exp(log_dt.astype(jnp.float32))[:, None]
    mag = jnp.exp(lam_re * dt)
    a_re = mag * jnp.cos(lam_im * dt)
    a_im = mag * jnp.sin(lam_im * dt)
    den = lam_re * lam_re + lam_im * lam_im
    n_re = a_re - 1.0
    f_re = (n_re * lam_re + a_im * lam_im) / den
    f_im = (a_im * lam_re - n_re * lam_im) / den
    b_re = b_re.astype(jnp.float32)
    b_im = b_im.astype(jnp.float32)
    bb_re = f_re[..., None] * b_re - f_im[..., None] * b_im
    bb_im = f_re[..., None] * b_im + f_im[..., None] * b_re
    return a_re, a_im, bb_re, bb_im


def complex_scan(a_re, a_im, b_re, b_im, reverse):
    def combine(e1, e2):
        a1r, a1i, b1r, b1i = e1
        a2r, a2i, b2r, b2i = e2
        return (a2r * a1r - a2i * a1i, a2r * a1i + a2i * a1r,
                a2r * b1r - a2i * b1i + b2r, a2r * b1i + a2i * b1r + b2i)
    ar = jnp.broadcast_to(a_re, b_re.shape)
    ai = jnp.broadcast_to(a_im, b_re.shape)
    _, _, hr, hi = lax.associative_scan(combine, (ar, ai, b_re, b_im), reverse=reverse, axis=1)
    return hr, hi


def s5_mixer(u, p, s0):
    b, s, d = u.shape
    uf = u.astype(jnp.float32)
    ug = uf.reshape(b, s, S5_GROUPS, S5_GROUP)
    y = uf * p['d_skip'].astype(jnp.float32)
    finals = []
    for direction, reverse in ((0, False), (1, True)):
        a_re, a_im, bb_re, bb_im = s5_discretize(p['lam_re'][direction], p['lam_im'][direction],
                                                 p['log_dt'][direction], p['b_re'][direction],
                                                 p['b_im'][direction])
        br = jnp.einsum('bsgc,gpc->bsgp', ug, bb_re)
        bi = jnp.einsum('bsgc,gpc->bsgp', ug, bb_im)
        if s0 is not None:
            first = -1 if reverse else 0
            s_re = s0[:, direction, 0].astype(jnp.float32)
            s_im = s0[:, direction, 1].astype(jnp.float32)
            br = br.at[:, first].add(a_re * s_re - a_im * s_im)
            bi = bi.at[:, first].add(a_re * s_im + a_im * s_re)
        hr, hi = complex_scan(a_re, a_im, br, bi, reverse)
        c_re = p['c_re'][direction].astype(jnp.float32)
        c_im = p['c_im'][direction].astype(jnp.float32)
        y_dir = jnp.einsum('bsgp,gcp->bsgc', hr, c_re) - jnp.einsum('bsgp,gcp->bsgc', hi, c_im)
        y = y + y_dir.reshape(b, s, d)
        if s0 is None:
            last = 0 if reverse else -1
            finals.append(jnp.stack([hr[:, last], hi[:, last]], axis=1))
    g = jax.nn.gelu(y)
    h = g @ p['w_glu'].astype(jnp.float32)
    out = (h[..., :d] * jax.nn.sigmoid(h[..., d:])).astype(u.dtype)
    state = jnp.stack(finals, axis=1).astype(u.dtype) if s0 is None else None
    return out, state


def conv_ffn(x, w_up, b_up, conv_k, conv_b, w_down):
    s = x.shape[1]
    h = x @ w_up + b_up
    r = CONV_W // 2
    hp = jnp.pad(h, ((0, 0), (r, r), (0, 0)))
    h = conv_b + sum(hp[:, j:j + s] * conv_k[j] for j in range(CONV_W))
    gate, val = jnp.split(h, 2, axis=-1)
    return (jax.nn.silu(gate) * val) @ w_down


def context_mixer(kind, p, h):
    if kind == 'A':
        return attn_context(h, p, A_HEADS, A_KV_HEADS, A_HEAD_DIM,
                            p['sink'].reshape(A_KV_HEADS, A_GROUP))
    if kind == 'B':
        return s5_mixer(h, p, None)
    return attn_context(h, p, C_HEADS, C_KV_HEADS, C_HEAD_DIM, None)


def latent_mixer(kind, p, h, cache, row, col):
    if kind == 'A':
        return window_attn_latent(h, p, cache[0], cache[1], row, col)
    if kind == 'B':
        return s5_mixer(h, p, cache)[0]
    return full_attn_latent(h, p, cache[0], cache[1], row, col)


def setup_inputs(seed: int = 0) -> dict:
    key = jax.random.key(seed)
    keys = iter(jax.random.split(key, 64))

    def nrm(shape, scale):
        return scale * jax.random.normal(next(keys), shape, jnp.float32)

    def gain(shape):
        return 1.0 + nrm(shape, 0.02)

    D = D_MODEL
    a_qkv = (A_HEADS + 2 * A_KV_HEADS) * A_HEAD_DIM
    c_qkv = (C_HEADS + 2 * C_KV_HEADS) * C_HEAD_DIM

    def s5_lam_im():
        base = jnp.broadcast_to(jnp.pi * jnp.arange(S5_STATE, dtype=jnp.float32), (2, S5_GROUPS, S5_STATE))
        return base + nrm((2, S5_GROUPS, S5_STATE), 0.01)

    inputs = {
        'x_prompt': nrm((BATCH, SEQ, D), 1.0),
        'x_sample': nrm((DEC_BATCH, DEC_SEQ, D), 1.0),
        'cache_l0_k': nrm((DEC_BATCH, PAST_LEN, A_KV_HEADS, A_HEAD_DIM), 1.0),
        'cache_l0_v': nrm((DEC_BATCH, PAST_LEN, A_KV_HEADS, A_HEAD_DIM), 1.0),
        'state_l1': nrm((DEC_BATCH, 2, 2, S5_GROUPS, S5_STATE), 0.3),
        'cache_l2_k': nrm((DEC_BATCH, PAST_LEN, C_KV_HEADS, C_HEAD_DIM), 1.0),
        'cache_l2_v': nrm((DEC_BATCH, PAST_LEN, C_KV_HEADS, C_HEAD_DIM), 1.0),
        'cache_l3_k': nrm((DEC_BATCH, PAST_LEN, A_KV_HEADS, A_HEAD_DIM), 1.0),
        'cache_l3_v': nrm((DEC_BATCH, PAST_LEN, A_KV_HEADS, A_HEAD_DIM), 1.0),
        'c': nrm((DEC_BATCH, D), 1.0),
        'c_ctx': nrm((D,), 1.0),
        'norm1': gain((DEPTH, D)),
        'norm2': gain((DEPTH, D)),
        'w_mod': nrm((DEPTH, D, 6 * D), 0.5 * D ** -0.5),
        'b_mod': nrm((DEPTH, 6 * D), 0.02),
        'w_up': nrm((DEPTH, D, 2 * D_FF), D ** -0.5),
        'b_up': nrm((DEPTH, 2 * D_FF), 0.02),
        'conv_k': nrm((DEPTH, CONV_W, 2 * D_FF), CONV_W ** -0.5),
        'conv_b': nrm((DEPTH, 2 * D_FF), 0.02),
        'w_down': nrm((DEPTH, D_FF, D), D_FF ** -0.5),
        'l0_w_qkv': nrm((D, a_qkv), D ** -0.5),
        'l0_q_norm': gain((A_HEAD_DIM,)),
        'l0_k_norm': gain((A_HEAD_DIM,)),
        'l0_sink': nrm((A_HEADS,), 1.0),
        'l0_w_o': nrm((A_HEADS * A_HEAD_DIM, D), (A_HEADS * A_HEAD_DIM) ** -0.5),
        'l1_lam_re': -0.5 + nrm((2, S5_GROUPS, S5_STATE), 0.01),
        'l1_lam_im': s5_lam_im(),
        'l1_log_dt': jax.random.uniform(next(keys), (2, S5_GROUPS), jnp.float32,
                                        math.log(DT_MIN), math.log(DT_MAX)),
        'l1_b_re': nrm((2, S5_GROUPS, S5_STATE, S5_GROUP), (2 * S5_GROUP) ** -0.5),
        'l1_b_im': nrm((2, S5_GROUPS, S5_STATE, S5_GROUP), (2 * S5_GROUP) ** -0.5),
        'l1_c_re': nrm((2, S5_GROUPS, S5_GROUP, S5_STATE), (2 * S5_STATE) ** -0.5),
        'l1_c_im': nrm((2, S5_GROUPS, S5_GROUP, S5_STATE), (2 * S5_STATE) ** -0.5),
        'l1_d_skip': nrm((D,), 1.0),
        'l1_w_glu': nrm((D, 2 * D), D ** -0.5),
        'l2_w_qkv': nrm((D, c_qkv), D ** -0.5),
        'l2_q_norm': gain((C_HEAD_DIM,)),
        'l2_k_norm': gain((C_HEAD_DIM,)),
        'l2_w_o': nrm((C_HEADS * C_HEAD_DIM, D), (C_HEADS * C_HEAD_DIM) ** -0.5),
        'l3_w_qkv': nrm((D, a_qkv), D ** -0.5),
        'l3_q_norm': gain((A_HEAD_DIM,)),
        'l3_k_norm': gain((A_HEAD_DIM,)),
        'l3_sink': nrm((A_HEADS,), 1.0),
        'l3_w_o': nrm((A_HEADS * A_HEAD_DIM, D), (A_HEADS * A_HEAD_DIM) ** -0.5),
    }
    return inputs


def reference(x_prompt, x_sample, cache_l0_k, cache_l0_v, state_l1, cache_l2_k, cache_l2_v,
              cache_l3_k, cache_l3_v, c, c_ctx, norm1, norm2, w_mod, b_mod, w_up, b_up, conv_k,
              conv_b, w_down, l0_w_qkv, l0_q_norm, l0_k_norm, l0_sink, l0_w_o, l1_lam_re, l1_lam_im,
              l1_log_dt, l1_b_re, l1_b_im, l1_c_re, l1_c_im, l1_d_skip, l1_w_glu, l2_w_qkv,
              l2_q_norm, l2_k_norm, l2_w_o, l3_w_qkv, l3_q_norm, l3_k_norm, l3_sink, l3_w_o):
    mixer_params = (
        {'w_qkv': l0_w_qkv, 'q_norm': l0_q_norm, 'k_norm': l0_k_norm, 'sink': l0_sink, 'w_o': l0_w_o},
        {'lam_re': l1_lam_re, 'lam_im': l1_lam_im, 'log_dt': l1_log_dt, 'b_re': l1_b_re,
         'b_im': l1_b_im, 'c_re': l1_c_re, 'c_im': l1_c_im, 'd_skip': l1_d_skip, 'w_glu': l1_w_glu},
        {'w_qkv': l2_w_qkv, 'q_norm': l2_q_norm, 'k_norm': l2_k_norm, 'w_o': l2_w_o},
        {'w_qkv': l3_w_qkv, 'q_norm': l3_q_norm, 'k_norm': l3_k_norm, 'sink': l3_sink, 'w_o': l3_w_o},
    )
    caches = ((cache_l0_k, cache_l0_v), state_l1, (cache_l2_k, cache_l2_v), (cache_l3_k, cache_l3_v))

    row, col = grid_positions(x_sample.shape[1])

    h_ctx = x_prompt
    h_lat = x_sample
    new_state = []
    for l in range(DEPTH):
        kind = LAYER_KINDS[l % N_MIXERS] if l < len(LAYER_KINDS) else 'ABC'[l % N_MIXERS]
        p = mixer_params[l]
        ffn_p = (w_up[l], b_up[l], conv_k[l], conv_b[l], w_down[l])

        sh1, sc1, g1, sh2, sc2, g2 = ada_params(c_ctx, w_mod[l], b_mod[l])
        mix, st = context_mixer(kind, p, sublayer_in(h_ctx, norm1[l], sh1, sc1))
        h_ctx = h_ctx + g1 * mix
        h_ctx = h_ctx + g2 * conv_ffn(sublayer_in(h_ctx, norm2[l], sh2, sc2), *ffn_p)
        new_state.append(st)

        sh1, sc1, g1, sh2, sc2, g2 = ada_params(c, w_mod[l], b_mod[l])
        mix = latent_mixer(kind, p, sublayer_in(h_lat, norm1[l], sh1, sc1), caches[l], row, col)
        h_lat = h_lat + g1 * mix
        h_lat = h_lat + g2 * conv_ffn(sublayer_in(h_lat, norm2[l], sh2, sc2), *ffn_p)

    new_l0_k, new_l0_v = new_state[0]
    new_l1_state = new_state[1]
    new_l2_k, new_l2_v = new_state[2]
    new_l3_k, new_l3_v = new_state[3]
    return (h_ctx, h_lat, new_l0_k, new_l0_v, new_l1_state, new_l2_k, new_l2_v, new_l3_k, new_l3_v)
```

```python
import functools
import math

import jax
import jax.numpy as jnp
import numpy as np
from jax import lax
from jax.experimental import pallas as pl
from jax.experimental.pallas import tpu as pltpu

F32 = jnp.float32
BF16 = jnp.bfloat16

D = 1024
N_LAYERS = 4
CTX_B, CTX_S = 32, 256
LAT_B, LAT_S = 8, 1024
PAST = 512
N_CTX = CTX_B * CTX_S
N_LAT = LAT_B * LAT_S
N_TOK = N_CTX + N_LAT
GRID_W = 64
WINDOW = 128
ROPE_THETA = 10000.0
EPS = 1e-6
D_FF = 2816
N_COND = 16

S5_G = 64
S5_C = 16
S5_P = 64
S5_L = 16

VMEM_LIMIT = 56 * 1024 * 1024
LANES = 128
NEG_BIG = -1e30

TM = 512
TM_FFN = 1024
TF = 256
TQ = 256


def _params(*sem):
    return pltpu.CompilerParams(dimension_semantics=sem, vmem_limit_bytes=VMEM_LIMIT)


def _cond_row(i, tm):
    n_ctx_tiles = N_CTX // tm
    return jnp.where(i < n_ctx_tiles, 0, 1 + (i - n_ctx_tiles) // (LAT_S // tm))


def _norm_mod(x, gain, shift, scale):
    ms = jnp.mean(x * x, axis=-1, keepdims=True)
    return (x * lax.rsqrt(ms + EPS) * gain) * (1.0 + scale) + shift


def _silu(x):
    return x * jax.nn.sigmoid(x)


def _mod_body(cond_ref, w_ref, b_ref, o_ref):
    s = _silu(cond_ref[...]).astype(BF16)
    o_ref[...] = jnp.dot(s, w_ref[...].astype(BF16), preferred_element_type=F32) + b_ref[...]


def _modulation(cond, w_mod, b_mod):
    tn = 1536
    return pl.pallas_call(
        _mod_body,
        grid=(N_LAYERS, 6 * D // tn),
        in_specs=[
            pl.BlockSpec((N_COND, D), lambda l, n: (0, 0)),
            pl.BlockSpec((None, D, tn), lambda l, n: (l, 0, n)),
            pl.BlockSpec((None, 1, tn), lambda l, n: (l, 0, n)),
        ],
        out_specs=pl.BlockSpec((None, N_COND, tn), lambda l, n: (l, 0, n)),
        out_shape=jax.ShapeDtypeStruct((N_LAYERS, N_COND, 6 * D), F32),
        compiler_params=_params("arbitrary", "arbitrary"),
        name="modulation",
    )(cond, w_mod, b_mod.reshape(N_LAYERS, 1, 6 * D))


def _mod_spec(layer, tm):
    return pl.BlockSpec((None, None, 1, 6 * D), lambda i, *_: (layer, _cond_row(i, tm), 0, 0))


def _qkv_body(h_ref, mod_ref, g_ref, w_ref, qn_ref, kn_ref, bd_ref, cos_ref, sin_ref,
              q_ref, k_ref, v_ref, *, nq, nk, dh):
    mod = mod_ref[...]
    xn = _norm_mod(h_ref[...], g_ref[...], mod[:, 0:D], mod[:, D:2 * D]).astype(BF16)
    qkv = jnp.dot(xn, w_ref[...].astype(BF16), preferred_element_type=F32)
    rows = qkv.shape[0]
    cos = cos_ref[...]
    sin = sin_ref[...]
    bd = bd_ref[...]
    quarter = dh // 4
    lane = lax.broadcasted_iota(jnp.int32, (rows, LANES), 1)
    first = (lane % (2 * quarter)) < quarter
    inv_dh = 1.0 / dh

    def head_norm_rope(z, gain):
        ss = jnp.dot((z * z).astype(BF16), bd, preferred_element_type=F32)
        zn = z * lax.rsqrt(ss * inv_dh + EPS)
        out = []
        for j in range(2):
            zz = zn[:, LANES * j:LANES * (j + 1)] * gain
            partner = jnp.where(first, pltpu.roll(zz, LANES - quarter, 1), pltpu.roll(zz, quarter, 1))
            out.append(zz * cos + partner * sin)
        return out

    qgain = qn_ref[...] * (dh ** -0.5)
    kgain = kn_ref[...]
    for c in range(nq // 256):
        lo, hi = head_norm_rope(qkv[:, 256 * c:256 * (c + 1)], qgain)
        q_ref[:, 256 * c:256 * c + LANES] = lo.astype(BF16)
        q_ref[:, 256 * c + LANES:256 * (c + 1)] = hi.astype(BF16)
    for c in range(nk // 256):
        lo, hi = head_norm_rope(qkv[:, nq + 256 * c:nq + 256 * (c + 1)], kgain)
        k_ref[:, 256 * c:256 * c + LANES] = lo
        k_ref[:, 256 * c + LANES:256 * (c + 1)] = hi
    v_ref[...] = qkv[:, nq + nk:]


def _rope_tables(dh):
    half, quarter = dh // 2, dh // 4
    freqs = 1.0 / (ROPE_THETA ** (np.arange(quarter, dtype=np.float32) / quarter))
    pos = np.arange(LAT_S)
    row = (pos // GRID_W).astype(np.float32)
    col = (pos % GRID_W).astype(np.float32)
    ang_r = (row[:, None] * freqs[None, :]).astype(np.float32)
    ang_c = (col[:, None] * freqs[None, :]).astype(np.float32)
    cos = np.concatenate([np.cos(ang_r), np.cos(ang_r), np.cos(ang_c), np.cos(ang_c)], axis=1)
    sin = np.concatenate([-np.sin(ang_r), np.sin(ang_r), -np.sin(ang_c), np.sin(ang_c)], axis=1)
    reps = LANES // dh
    cos = np.tile(cos.astype(np.float32), (1, reps))
    sin = np.tile(sin.astype(np.float32), (1, reps))
    cos = np.concatenate([np.ones_like(cos), cos], axis=0)
    sin = np.concatenate([np.zeros_like(sin), sin], axis=0)
    return jnp.asarray(cos), jnp.asarray(sin)


def _block_diag_ones(dh):
    idx = np.arange(256) // dh
    return jnp.asarray((idx[:, None] == idx[None, :]).astype(np.float32), dtype=BF16)


def _qkv_proj(h, mods, layer, gain, w_qkv, q_norm, k_norm, n_heads, n_kv, dh):
    nq, nk = n_heads * dh, n_kv * dh
    nqkv = nq + 2 * nk
    cos, sin = _rope_tables(dh)
    reps = LANES // dh
    n_ctx_tiles = N_CTX // TM
    lat_tiles = LAT_S // TM

    def rope_idx(i):
        return (jnp.where(i < n_ctx_tiles, 0, lat_tiles + (i - n_ctx_tiles) % lat_tiles), 0)

    return pl.pallas_call(
        functools.partial(_qkv_body, nq=nq, nk=nk, dh=dh),
        grid=(N_TOK // TM,),
        in_specs=[
            pl.BlockSpec((TM, D), lambda i: (i, 0)),
            _mod_spec(layer, TM),
            pl.BlockSpec((1, D), lambda i: (0, 0)),
            pl.BlockSpec((D, nqkv), lambda i: (0, 0)),
            pl.BlockSpec((1, LANES), lambda i: (0, 0)),
            pl.BlockSpec((1, LANES), lambda i: (0, 0)),
            pl.BlockSpec((256, 256), lambda i: (0, 0)),
            pl.BlockSpec((TM, LANES), rope_idx),
            pl.BlockSpec((TM, LANES), rope_idx),
        ],
        out_specs=[
            pl.BlockSpec((TM, nq), lambda i: (i, 0)),
            pl.BlockSpec((TM, nk), lambda i: (i, 0)),
            pl.BlockSpec((TM, nk), lambda i: (i, 0)),
        ],
        out_shape=[
            jax.ShapeDtypeStruct((N_TOK, nq), BF16),
            jax.ShapeDtypeStruct((N_TOK, nk), F32),
            jax.ShapeDtypeStruct((N_TOK, nk), F32),
        ],
        compiler_params=_params("arbitrary"),
        name=f"qkv_l{layer}",
    )(h, mods.reshape(N_LAYERS, N_COND, 1, 6 * D), gain.reshape(1, D), w_qkv,
      jnp.tile(q_norm, reps).reshape(1, LANES), jnp.tile(k_norm, reps).reshape(1, LANES),
      _block_diag_ones(dh), cos, sin)


def _attend(q, segs, sink):
    scores = []
    m = None
    for k, _, mask in segs:
        s = lax.dot_general(q, k, (((1,), (1,)), ((), ())), preferred_element_type=F32)
        if mask is not None:
            s = jnp.where(mask, s, NEG_BIG)
        scores.append(s)
        sm = jnp.max(s, axis=-1, keepdims=True)
        m = sm if m is None else jnp.maximum(m, sm)
    if sink is not None:
        m = jnp.maximum(m, sink)
    den = None
    acc = None
    for s, (_, v, _) in zip(scores, segs):
        p = jnp.exp(s - m)
        ps = jnp.sum(p, axis=-1, keepdims=True)
        pv = jnp.dot(p.astype(BF16), v, preferred_element_type=F32)
        den = ps if den is None else den + ps
        acc = pv if acc is None else acc + pv
    if sink is not None:
        den = den + jnp.exp(sink - m)
    return acc / den


def _dup_halves(x, kv):
    lane = lax.broadcasted_iota(jnp.int32, x.shape, 1)
    r = pltpu.roll(x, 64, 1)
    lo = lane < 64
    return (jnp.where(lo, x, r) if kv % 2 == 0 else jnp.where(lo, r, x)).astype(BF16)


def _attend_heads64(q_ref, o_ref, sink_ref, kv_segs, masks, tq):
    lane = lax.broadcasted_iota(jnp.int32, (tq, LANES), 1)
    lo = lane < 64
    rowsel = lax.broadcasted_iota(jnp.int32, (4 * tq, 1), 0)
    for kv in range(4):
        segs = []
        for (k, v), mask in zip(kv_segs, masks):
            c = LANES * (kv // 2)
            segs.append((_dup_halves(k[:, c:c + LANES], kv), _dup_halves(v[:, c:c + LANES], kv),
                         None if mask is None else jnp.concatenate([mask] * 4, axis=0)))
        parts = []
        sinks = None
        for pair in range(2):
            j = 2 * kv + pair
            qp = q_ref[:, LANES * j:LANES * (j + 1)]
            zero = jnp.zeros_like(qp)
            parts += [jnp.where(lo, qp, zero), jnp.where(lo, zero, qp)]
        for r in range(4):
            sv = sink_ref[4 * kv + r]
            sinks = jnp.full((4 * tq, 1), sv, F32) if sinks is None else jnp.where(rowsel >= r * tq, sv, sinks)
        out = _attend(jnp.concatenate(parts, axis=0), segs, sinks)
        for pair in range(2):
            j = 2 * kv + pair
            a = out[(2 * pair) * tq:(2 * pair + 1) * tq]
            b = out[(2 * pair + 1) * tq:(2 * pair + 2) * tq]
            o_ref[:, LANES * j:LANES * (j + 1)] = jnp.where(lo, a, b).astype(BF16)


def _attend_heads128(q_ref, o_ref, kv_segs, tq):
    for kv in range(4):
        c = LANES * kv
        segs = [(k[:, c:c + LANES].astype(BF16), v[:, c:c + LANES].astype(BF16), None) for k, v in kv_segs]
        q = jnp.concatenate([q_ref[:, LANES * (2 * kv):LANES * (2 * kv + 1)],
                             q_ref[:, LANES * (2 * kv + 1):LANES * (2 * kv + 2)]], axis=0)
        out = _attend(q, segs, None)
        o_ref[:, LANES * (2 * kv):LANES * (2 * kv + 1)] = out[:tq].astype(BF16)
        o_ref[:, LANES * (2 * kv + 1):LANES * (2 * kv + 2)] = out[tq:].astype(BF16)


def _ctx_attn_a_body(sink_ref, q_ref, k_ref, v_ref, o_ref):
    _attend_heads64(q_ref, o_ref, sink_ref, [(k_ref[...], v_ref[...])], [None], CTX_S)


def _ctx_attn_c_body(q_ref, k_ref, v_ref, o_ref):
    _attend_heads128(q_ref, o_ref, [(k_ref[...], v_ref[...])], CTX_S)


def _lat_attn_a_body(sink_ref, q_ref, k_ref, v_ref, ck_ref, cv_ref, o_ref):
    qi = pl.program_id(1)
    band = TQ + 2 * WINDOW
    ws = pl.multiple_of(jnp.clip(qi * TQ - WINDOW, 0, LAT_S - band), WINDOW)
    qpos = qi * TQ + lax.broadcasted_iota(jnp.int32, (TQ, band), 0)
    kpos = ws + lax.broadcasted_iota(jnp.int32, (TQ, band), 1)
    mask = jnp.abs(qpos - kpos) <= WINDOW
    segs = [(ck_ref[...], cv_ref[...]), (k_ref[pl.ds(ws, band), :], v_ref[pl.ds(ws, band), :])]
    _attend_heads64(q_ref, o_ref, sink_ref, segs, [None, mask], TQ)


def _lat_attn_c_body(q_ref, k_ref, v_ref, ck_ref, cv_ref, o_ref):
    _attend_heads128(q_ref, o_ref, [(ck_ref[...], cv_ref[...]), (k_ref[...], v_ref[...])], TQ)


_SMEM_SPEC = pl.BlockSpec(memory_space=pltpu.SMEM)


def _attention(q, k, v, cache_k, cache_v, sink, layer):
    nk = k.shape[1]
    ck = cache_k.reshape(LAT_B, PAST, nk)
    cv = cache_v.reshape(LAT_B, PAST, nk)
    ctx_specs = [
        pl.BlockSpec((CTX_S, D), lambda b: (b, 0)),
        pl.BlockSpec((CTX_S, nk), lambda b: (b, 0)),
        pl.BlockSpec((CTX_S, nk), lambda b: (b, 0)),
    ]
    qb = LAT_S // TQ
    lat_specs = [
        pl.BlockSpec((TQ, D), lambda b, i: (N_CTX // TQ + b * qb + i, 0)),
        pl.BlockSpec((LAT_S, nk), lambda b, i: (N_CTX // LAT_S + b, 0)),
        pl.BlockSpec((LAT_S, nk), lambda b, i: (N_CTX // LAT_S + b, 0)),
        pl.BlockSpec((None, PAST, nk), lambda b, i: (b, 0, 0)),
        pl.BlockSpec((None, PAST, nk), lambda b, i: (b, 0, 0)),
    ]
    ctx_out = dict(out_specs=pl.BlockSpec((CTX_S, D), lambda b: (b, 0)),
                   out_shape=jax.ShapeDtypeStruct((N_CTX, D), BF16),
                   grid=(CTX_B,), compiler_params=_params("arbitrary"))
    lat_out = dict(out_specs=pl.BlockSpec((TQ, D), lambda b, i: (b * qb + i, 0)),
                   out_shape=jax.ShapeDtypeStruct((N_LAT, D), BF16),
                   grid=(LAT_B, qb), compiler_params=_params("arbitrary", "arbitrary"))
    if sink is not None:
        o_ctx = pl.pallas_call(_ctx_attn_a_body, in_specs=[_SMEM_SPEC] + ctx_specs,
                               name=f"attn_ctx_l{layer}", **ctx_out)(sink, q, k, v)
        o_lat = pl.pallas_call(_lat_attn_a_body, in_specs=[_SMEM_SPEC] + lat_specs,
                               name=f"attn_lat_l{layer}", **lat_out)(sink, q, k, v, ck, cv)
    else:
        o_ctx = pl.pallas_call(_ctx_attn_c_body, in_specs=ctx_specs,
                               name=f"attn_ctx_l{layer}", **ctx_out)(q, k, v)
        o_lat = pl.pallas_call(_lat_attn_c_body, in_specs=lat_specs,
                               name=f"attn_lat_l{layer}", **lat_out)(q, k, v, ck, cv)
    return o_ctx, o_lat


def _oproj_body(oc_ref, ol_ref, h_ref, mod_ref, w_ref, out_ref):
    is_ctx = pl.program_id(0) < N_CTX // TM
    o = jnp.where(is_ctx, oc_ref[...], ol_ref[...])
    mix = jnp.dot(o, w_ref[...].astype(BF16), preferred_element_type=F32)
    out_ref[...] = h_ref[...] + mod_ref[:, 2 * D:3 * D] * mix


def _out_proj(o_ctx, o_lat, h, mods, layer, w_o):
    n_ctx_tiles = N_CTX // TM
    return pl.pallas_call(
        _oproj_body,
        grid=(N_TOK // TM,),
        in_specs=[
            pl.BlockSpec((TM, D), lambda i: (jnp.minimum(i, n_ctx_tiles - 1), 0)),
            pl.BlockSpec((TM, D), lambda i: (jnp.maximum(i - n_ctx_tiles, 0), 0)),
            pl.BlockSpec((TM, D), lambda i: (i, 0)),
            _mod_spec(layer, TM),
            pl.BlockSpec((D, D), lambda i: (0, 0)),
        ],
        out_specs=pl.BlockSpec((TM, D), lambda i: (i, 0)),
        out_shape=jax.ShapeDtypeStruct((N_TOK, D), F32),
        compiler_params=_params("arbitrary"),
        name=f"oproj_l{layer}",
    )(o_ctx, o_lat, h, mods.reshape(N_LAYERS, N_COND, 1, 6 * D), w_o)


def _ffn_body(h_ref, mod_ref, g_ref, wg_ref, wv_ref, bg_ref, bv_ref, kg_ref, kv_ref, cg_ref, cv_ref,
              wd_ref, out_ref, xn_scr, acc_scr):
    i = pl.program_id(0)
    f = pl.program_id(1)

    @pl.when(f == 0)
    def _():
        mod = mod_ref[...]
        xn_scr[...] = _norm_mod(h_ref[...], g_ref[...], mod[:, 3 * D:4 * D], mod[:, 4 * D:5 * D]).astype(BF16)
        acc_scr[...] = jnp.zeros_like(acc_scr)

    xn = xn_scr[...]
    seq = jnp.where(i < N_CTX // TM_FFN, CTX_S, LAT_S)
    pos = lax.broadcasted_iota(jnp.int32, (TM_FFN, 1), 0) & (seq - 1)
    is_first = pos == 0
    is_last = pos == seq - 1

    def branch(w_ref, b_ref, k_ref, c_ref):
        u = jnp.dot(xn, w_ref[...].astype(BF16), preferred_element_type=F32) + b_ref[...]
        prev = jnp.where(is_first, 0.0, pltpu.roll(u, 1, 0))
        nxt = jnp.where(is_last, 0.0, pltpu.roll(u, TM_FFN - 1, 0))
        kk = k_ref[...]
        return c_ref[...] + prev * kk[0:1] + u * kk[1:2] + nxt * kk[2:3]

    gate = branch(wg_ref, bg_ref, kg_ref, cg_ref)
    val = branch(wv_ref, bv_ref, kv_ref, cv_ref)
    a = (_silu(gate) * val).astype(BF16)
    acc_scr[...] += jnp.dot(a, wd_ref[...].astype(BF16), preferred_element_type=F32)

    @pl.when(f == pl.num_programs(1) - 1)
    def _():
        out_ref[...] = h_ref[...] + mod_ref[:, 5 * D:6 * D] * acc_scr[...]


def _ffn(h, mods, layer, gain, w_up, b_up, conv_k, conv_b, w_down):
    nf = D_FF // TF
    b3 = b_up.reshape(N_LAYERS, 1, 2 * D_FF)
    c3 = conv_b.reshape(N_LAYERS, 1, 2 * D_FF)
    col = lambda off: (lambda i, f: (layer, 0, off + f))
    return pl.pallas_call(
        _ffn_body,
        grid=(N_TOK // TM_FFN, nf),
        in_specs=[
            pl.BlockSpec((TM_FFN, D), lambda i, f: (i, 0)),
            _mod_spec(layer, TM_FFN),
            pl.BlockSpec((1, D), lambda i, f: (0, 0)),
            pl.BlockSpec((None, D, TF), col(0)),
            pl.BlockSpec((None, D, TF), col(nf)),
            pl.BlockSpec((None, 1, TF), col(0)),
            pl.BlockSpec((None, 1, TF), col(nf)),
            pl.BlockSpec((None, 3, TF), col(0)),
            pl.BlockSpec((None, 3, TF), col(nf)),
            pl.BlockSpec((None, 1, TF), col(0)),
            pl.BlockSpec((None, 1, TF), col(nf)),
            pl.BlockSpec((None, TF, D), lambda i, f: (layer, f, 0)),
        ],
        out_specs=pl.BlockSpec((TM_FFN, D), lambda i, f: (i, 0)),
        out_shape=jax.ShapeDtypeStruct((N_TOK, D), F32),
        scratch_shapes=[pltpu.VMEM((TM_FFN, D), BF16), pltpu.VMEM((TM_FFN, D), F32)],
        compiler_params=_params("arbitrary", "arbitrary"),
        name=f"ffn_l{layer}",
    )(h, mods.reshape(N_LAYERS, N_COND, 1, 6 * D), gain.reshape(1, D), w_up, w_up, b3, b3,
      conv_k, conv_k, c3, c3, w_down)


def _s5_in_body(h_ref, mod_ref, g_ref, u_ref):
    mod = mod_ref[...]
    u_ref[...] = _norm_mod(h_ref[...], g_ref[...], mod[:, 0:D], mod[:, D:2 * D]).astype(BF16)


def _s5_in(h, mods, layer, gain):
    return pl.pallas_call(
        _s5_in_body,
        grid=(N_TOK // TM,),
        in_specs=[pl.BlockSpec((TM, D), lambda i: (i, 0)), _mod_spec(layer, TM),
                  pl.BlockSpec((1, D), lambda i: (0, 0))],
        out_specs=pl.BlockSpec((TM, D), lambda i: (i, 0)),
        out_shape=jax.ShapeDtypeStruct((N_TOK, D), BF16),
        compiler_params=_params("arbitrary"),
        name="s5_in",
    )(h, mods.reshape(N_LAYERS, N_COND, 1, 6 * D), gain.reshape(1, D))


def _s5_weights_body(lamc_re_ref, lamc_im_ref, lamr_re_ref, lamr_im_ref, ldt_ref,
                     bt_re_ref, bt_im_ref, btr_re_ref, btr_im_ref, ct_re_ref, ct_im_ref,
                     w_ref, a_ref):
    blk = (lax.broadcasted_iota(jnp.int32, (1, 256), 1) // S5_L).astype(F32)
    lane256 = lax.broadcasted_iota(jnp.int32, (S5_C, 256), 1)
    hi = lax.Precision.HIGHEST
    krow = []
    st_rows = {}
    w_rows = {}
    for d in range(2):
        dt = jnp.exp(ldt_ref[d])
        lr = lamc_re_ref[d]
        li = lamc_im_ref[d]

        def powers(expo):
            mag = jnp.exp((lr * dt) * expo)
            ang = (li * dt) * expo
            return mag * jnp.cos(ang), mag * jnp.sin(ang)

        asc_r, asc_i = powers(blk)
        dsc_r, dsc_i = powers(15.0 - blk)
        ar = asc_r[:, S5_L:S5_L + 1]
        ai = asc_i[:, S5_L:S5_L + 1]
        den = lr * lr + li * li
        n_re = ar - 1.0
        f_re = (n_re * lr + ai * li) / den
        f_im = (ai * lr - n_re * li) / den
        bbr = f_re * bt_re_ref[d] - f_im * bt_im_ref[d]
        bbi = f_re * bt_im_ref[d] + f_im * bt_re_ref[d]
        e0r, e0i = (asc_r, asc_i) if d == 0 else (dsc_r, dsc_i)
        e1r = e0r * ar - e0i * ai
        e1i = e0r * ai + e0i * ar
        pr, pi = (dsc_r, dsc_i) if d == 0 else (asc_r, asc_i)
        st_rows[("re", d)] = pr * bbr - pi * bbi
        st_rows[("im", d)] = pr * bbi + pi * bbr
        ctr = ct_re_ref[d]
        cti = ct_im_ref[d]
        k_re = ctr * e0r - cti * e0i
        k_imneg = -(ctr * e0i + cti * e0r)
        w_rows[("re", d)] = ctr * e1r - cti * e1i
        w_rows[("im", d)] = -(ctr * e1i + cti * e1r)
        lrr = lamr_re_ref[d:d + 1, :]
        lir = lamr_im_ref[d:d + 1, :]
        magr = jnp.exp(lrr * dt)
        arr = magr * jnp.cos(lir * dt)
        air = magr * jnp.sin(lir * dt)
        denr = lrr * lrr + lir * lir
        nr = arr - 1.0
        fr = (nr * lrr + air * lir) / denr
        fi = (air * lrr - nr * lir) / denr
        bbr_row = fr * btr_re_ref[d] - fi * btr_im_ref[d]
        bbi_row = fr * btr_im_ref[d] + fi * btr_re_ref[d]
        krow.append(jnp.dot(bbr_row, k_re, precision=hi, preferred_element_type=F32)
                    + jnp.dot(bbi_row, k_imneg, precision=hi, preferred_element_type=F32))
        mag16 = jnp.exp(lrr * dt * 16.0)
        a_ref[2 * d:2 * d + 1, :] = mag16 * jnp.cos(lir * dt * 16.0)
        a_ref[2 * d + 1:2 * d + 2, :] = mag16 * jnp.sin(lir * dt * 16.0)
    a_ref[4:8, :] = jnp.zeros((4, S5_P), F32)

    for s in range(S5_L):
        fwd = krow[0] if s == 0 else jnp.where(lane256 >= S5_C * s, pltpu.roll(krow[0], S5_C * s, 1), 0.0)
        bwd = krow[1] if s == S5_L - 1 else jnp.where(lane256 < S5_C * (s + 1),
                                                      pltpu.roll(krow[1], S5_C * (s + 1), 1), 0.0)
        w_ref[S5_C * s:S5_C * (s + 1), 0:256] = (fwd + bwd).astype(BF16)
    order = [("re", 0), ("re", 1), ("im", 0), ("im", 1)]
    st = jnp.concatenate([st_rows[o] for o in order], axis=0)
    w_ref[:, 256:512] = st.T.astype(BF16)
    w_ref[:, 512:768] = jnp.concatenate([w_rows[o] for o in order], axis=0).astype(BF16)


def _s5_weights(lam_re, lam_im, log_dt, b_re, b_im, c_re, c_im):
    col = lambda x: x.transpose(1, 0, 2).reshape(S5_G, 2, S5_P, 1)
    row = lambda x: x.transpose(1, 0, 2)
    ldt = log_dt.transpose(1, 0).reshape(S5_G, 2, 1, 1)
    b_tiled = lambda x: jnp.tile(x.transpose(1, 0, 2, 3), (1, 1, 1, S5_L))
    b_rowf = lambda x: x.transpose(1, 0, 3, 2)
    c_tiled = lambda x: jnp.tile(x.transpose(1, 0, 3, 2), (1, 1, 1, S5_L))
    g4 = lambda *tail: pl.BlockSpec((None,) + tail, lambda g: (g,) + (0,) * len(tail))
    return pl.pallas_call(
        _s5_weights_body,
        grid=(S5_G,),
        in_specs=[g4(2, S5_P, 1), g4(2, S5_P, 1), g4(2, S5_P), g4(2, S5_P), g4(2, 1, 1),
                  g4(2, S5_P, 256), g4(2, S5_P, 256), g4(2, S5_C, S5_P), g4(2, S5_C, S5_P),
                  g4(2, S5_P, 256), g4(2, S5_P, 256)],
        out_specs=[g4(256, 768), g4(8, S5_P)],
        out_shape=[jax.ShapeDtypeStruct((S5_G, 256, 768), BF16),
                   jax.ShapeDtypeStruct((S5_G, 8, S5_P), F32)],
        compiler_params=_params("arbitrary"),
        name="s5_weights",
    )(col(lam_re), col(lam_im), row(lam_re), row(lam_im), ldt,
      b_tiled(b_re), b_tiled(b_im), b_rowf(b_re), b_rowf(b_im), c_tiled(c_re), c_tiled(c_im))


def _s5_scan(d, a_re, a_im, init_re, init_im, hin_scr, n_chunks, rows):
    lane = lax.broadcasted_iota(jnp.int32, (rows, LANES), 1)
    fwd = lane < S5_P
    hr, hi = init_re, init_im
    for c in range(n_chunks):
        cf = slice(c * rows, (c + 1) * rows)
        cb = slice((n_chunks - 1 - c) * rows, (n_chunks - c) * rows)
        hin_scr[cf, 0:S5_P] = hr[:, 0:S5_P]
        hin_scr[cb, S5_P:LANES] = hr[:, S5_P:LANES]
        hin_scr[cf, LANES:LANES + S5_P] = hi[:, 0:S5_P]
        hin_scr[cb, LANES + S5_P:2 * LANES] = hi[:, S5_P:LANES]
        dr = jnp.where(fwd, d[cf, 0:LANES], d[cb, 0:LANES])
        di = jnp.where(fwd, d[cf, LANES:2 * LANES], d[cb, LANES:2 * LANES])
        hr, hi = hr * a_re - hi * a_im + dr, hr * a_im + hi * a_re + di
    return hr, hi


def _s5_core_body(xc_ref, xl_ref, w_ref, a_ref, s0_ref, yc_ref, yl_ref, fin_ref, hc_scr, hl_scr):
    a_re = a_ref[0:1, :]
    a_im = a_ref[1:2, :]
    w_ts = w_ref[:, 0:512]
    w_w = w_ref[:, 512:768]

    ts = jnp.dot(xc_ref[...], w_ts, preferred_element_type=F32)
    zero = jnp.zeros((CTX_B, LANES), F32)
    fr, fi = _s5_scan(ts[:, 256:512], a_re, a_im, zero, zero, hc_scr, CTX_S // S5_L, CTX_B)
    fin_ref[:, 0:LANES] = fr
    fin_ref[:, LANES:2 * LANES] = fi
    yc_ref[...] = ts[:, 0:256] + jnp.dot(hc_scr[...].astype(BF16), w_w, preferred_element_type=F32)

    ts = jnp.dot(xl_ref[...], w_ts, preferred_element_type=F32)
    _s5_scan(ts[:, 256:512], a_re, a_im, s0_ref[:, 0:LANES], s0_ref[:, LANES:2 * LANES], hl_scr,
             LAT_S // S5_L, LAT_B)
    yl_ref[...] = ts[:, 0:256] + jnp.dot(hl_scr[...].astype(BF16), w_w, preferred_element_type=F32)


def _s5_core(x_ctx, x_lat, wall, avec, s0):
    rc = N_CTX // S5_L
    rl = N_LAT // S5_L
    g3 = lambda a, b: pl.BlockSpec((None, a, b), lambda g: (g, 0, 0))
    return pl.pallas_call(
        _s5_core_body,
        grid=(S5_G,),
        in_specs=[g3(rc, 256), g3(rl, 256), g3(256, 768), g3(8, LANES), g3(LAT_B, 256)],
        out_specs=[g3(rc, 256), g3(rl, 256), g3(CTX_B, 256)],
        out_shape=[jax.ShapeDtypeStruct((S5_G, rc, 256), F32),
                   jax.ShapeDtypeStruct((S5_G, rl, 256), F32),
                   jax.ShapeDtypeStruct((S5_G, CTX_B, 256), F32)],
        scratch_shapes=[pltpu.VMEM((rc, 256), F32), pltpu.VMEM((rl, 256), F32)],
        compiler_params=_params("arbitrary"),
        name="s5_core",
    )(x_ctx, x_lat, wall, avec, s0)


def _gelu_tanh(x):
    return 0.5 * x * (1.0 + jnp.tanh(math.sqrt(2.0 / math.pi) * (x + 0.044715 * (x * x * x))))


def _s5_out_body(h_ref, y_ref, mod_ref, g_ref, dskip_ref, w_ref, out_ref):
    mod = mod_ref[...]
    h = h_ref[...]
    u = _norm_mod(h, g_ref[...], mod[:, 0:D], mod[:, D:2 * D])
    y = u * dskip_ref[...] + y_ref[...]
    g = _gelu_tanh(y).astype(BF16)
    hh = jnp.dot(g, w_ref[...].astype(BF16), preferred_element_type=F32)
    out_ref[...] = h + mod[:, 2 * D:3 * D] * (hh[:, 0:D] * jax.nn.sigmoid(hh[:, D:2 * D]))


def _s5_out(h, y, mods, layer, gain, d_skip, w_glu):
    return pl.pallas_call(
        _s5_out_body,
        grid=(N_TOK // TM,),
        in_specs=[pl.BlockSpec((TM, D), lambda i: (i, 0)), pl.BlockSpec((TM, D), lambda i: (i, 0)),
                  _mod_spec(layer, TM), pl.BlockSpec((1, D), lambda i: (0, 0)),
                  pl.BlockSpec((1, D), lambda i: (0, 0)), pl.BlockSpec((D, 2 * D), lambda i: (0, 0))],
        out_specs=pl.BlockSpec((TM, D), lambda i: (i, 0)),
        out_shape=jax.ShapeDtypeStruct((N_TOK, D), F32),
        compiler_params=_params("arbitrary"),
        name="s5_out",
    )(h, y, mods.reshape(N_LAYERS, N_COND, 1, 6 * D), gain.reshape(1, D), d_skip.reshape(1, D), w_glu)


def _to_chunks(u, batch, seq):
    x = u.reshape(batch, seq // S5_L, S5_L, S5_G, S5_C).transpose(3, 1, 0, 2, 4)
    return x.reshape(S5_G, (seq // S5_L) * batch, S5_L * S5_C)


def _from_chunks(y, batch, seq):
    x = y.reshape(S5_G, seq // S5_L, batch, S5_L, S5_C).transpose(2, 1, 3, 0, 4)
    return x.reshape(batch * seq, D)


def _s5_mixer(h, mods, layer, gain, state, lam_re, lam_im, log_dt, b_re, b_im, c_re, c_im, d_skip, w_glu):
    u = _s5_in(h, mods, layer, gain)
    wall, avec = _s5_weights(lam_re, lam_im, log_dt, b_re, b_im, c_re, c_im)
    a2 = jnp.stack([jnp.concatenate([avec[:, 0], avec[:, 2]], axis=-1),
                    jnp.concatenate([avec[:, 1], avec[:, 3]], axis=-1)], axis=1)
    a2 = jnp.pad(a2, ((0, 0), (0, 6), (0, 0)))
    s0 = state.transpose(3, 0, 2, 1, 4).reshape(S5_G, LAT_B, 4 * S5_P)
    y_ctx, y_lat, fin = _s5_core(_to_chunks(u[:N_CTX], CTX_B, CTX_S), _to_chunks(u[N_CTX:], LAT_B, LAT_S),
                                 wall, a2, s0)
    y = jnp.concatenate([_from_chunks(y_ctx, CTX_B, CTX_S), _from_chunks(y_lat, LAT_B, LAT_S)], axis=0)
    new_state = fin.reshape(S5_G, CTX_B, 2, 2, S5_P).transpose(1, 3, 2, 0, 4)
    return _s5_out(h, y, mods, layer, gain, d_skip, w_glu), new_state


def kernel(x_prompt, x_sample, cache_l0_k, cache_l0_v, state_l1, cache_l2_k, cache_l2_v, cache_l3_k, cache_l3_v, c, c_ctx, norm1, norm2, w_mod, b_mod, w_up, b_up, conv_k, conv_b, w_down, l0_w_qkv, l0_q_norm, l0_k_norm, l0_sink, l0_w_o, l1_lam_re, l1_lam_im, l1_log_dt, l1_b_re, l1_b_im, l1_c_re, l1_c_im, l1_d_skip, l1_w_glu, l2_w_qkv, l2_q_norm, l2_k_norm, l2_w_o, l3_w_qkv, l3_q_norm, l3_k_norm, l3_sink, l3_w_o):
    h = jnp.concatenate([x_prompt.reshape(N_CTX, D), x_sample.reshape(N_LAT, D)], axis=0)
    cond = jnp.concatenate([c_ctx[None, :], c, jnp.zeros((N_COND - 1 - LAT_B, D), F32)], axis=0)
    mods = _modulation(cond, w_mod, b_mod)

    attn_layers = {
        0: (l0_w_qkv, l0_q_norm, l0_k_norm, l0_sink, l0_w_o, cache_l0_k, cache_l0_v, 16, 4, 64),
        2: (l2_w_qkv, l2_q_norm, l2_k_norm, None, l2_w_o, cache_l2_k, cache_l2_v, 8, 4, 128),
        3: (l3_w_qkv, l3_q_norm, l3_k_norm, l3_sink, l3_w_o, cache_l3_k, cache_l3_v, 16, 4, 64),
    }
    new_kv = {}
    new_state = None
    for layer in range(N_LAYERS):
        if layer in attn_layers:
            w_qkv, q_norm, k_norm, sink, w_o, ck, cv, n_heads, n_kv, dh = attn_layers[layer]
            q, k, v = _qkv_proj(h, mods, layer, norm1[layer], w_qkv, q_norm, k_norm, n_heads, n_kv, dh)
            new_kv[layer] = (k[:N_CTX].reshape(CTX_B, CTX_S, n_kv, dh), v[:N_CTX].reshape(CTX_B, CTX_S, n_kv, dh))
            o_ctx, o_lat = _attention(q, k, v, ck, cv, sink, layer)
            h = _out_proj(o_ctx, o_lat, h, mods, layer, w_o)
        else:
            h, new_state = _s5_mixer(h, mods, layer, norm1[layer], state_l1, l1_lam_re, l1_lam_im, l1_log_dt,
                                     l1_b_re, l1_b_im, l1_c_re, l1_c_im, l1_d_skip, l1_w_glu)
        h = _ffn(h, mods, layer, norm2[layer], w_up, b_up, conv_k, conv_b, w_down)

    y_prompt = h[:N_CTX].reshape(CTX_B, CTX_S, D)
    y_sample = h[N_CTX:].reshape(LAT_B, LAT_S, D)
    return (y_prompt, y_sample, new_kv[0][0], new_kv[0][1], new_state,
            new_kv[2][0], new_kv[2][1], new_kv[3][0], new_kv[3][1])
```

```python
import functools
import math

import jax
import jax.numpy as jnp
import numpy as np
from jax import lax
from jax.experimental import pallas as pl
from jax.experimental.pallas import tpu as pltpu

F32 = jnp.float32
BF16 = jnp.bfloat16

D = 1024
N_LAYERS = 4
CTX_B, CTX_S = 32, 256
LAT_B, LAT_S = 8, 1024
PAST = 512
N_CTX = CTX_B * CTX_S
N_LAT = LAT_B * LAT_S
N_TOK = N_CTX + N_LAT
GRID_W = 64
WINDOW = 128
ROPE_THETA = 10000.0
EPS = 1e-6
D_FF = 2816
N_COND = 16

S5_G = 64
S5_C = 16
S5_P = 64
S5_L = 16

VMEM_LIMIT = 56 * 1024 * 1024
LANES = 128
NEG_BIG = -1e30

TM = 512
TM_FFN = 1024
TF = 256
TQ = 256


def _params(*sem):
    return pltpu.CompilerParams(dimension_semantics=sem, vmem_limit_bytes=VMEM_LIMIT)


def _cond_row(i, tm):
    n_ctx_tiles = N_CTX // tm
    return jnp.where(i < n_ctx_tiles, 0, 1 + (i - n_ctx_tiles) // (LAT_S // tm))


def _norm_mod(x, gain, shift, scale):
    ms = jnp.mean(x * x, axis=-1, keepdims=True)
    return (x * lax.rsqrt(ms + EPS) * gain) * (1.0 + scale) + shift


def _silu(x):
    return x * jax.nn.sigmoid(x)


def _mod_body(cond_ref, w_ref, b_ref, o_ref):
    s = _silu(cond_ref[...]).astype(BF16)
    o_ref[...] = jnp.dot(s, w_ref[...].astype(BF16), preferred_element_type=F32) + b_ref[...]


def _modulation(cond, w_mod, b_mod):
    tn = 1536
    return pl.pallas_call(
        _mod_body,
        grid=(N_LAYERS, 6 * D // tn),
        in_specs=[
            pl.BlockSpec((N_COND, D), lambda l, n: (0, 0)),
            pl.BlockSpec((None, D, tn), lambda l, n: (l, 0, n)),
            pl.BlockSpec((None, 1, tn), lambda l, n: (l, 0, n)),
        ],
        out_specs=pl.BlockSpec((None, N_COND, tn), lambda l, n: (l, 0, n)),
        out_shape=jax.ShapeDtypeStruct((N_LAYERS, N_COND, 6 * D), F32),
        compiler_params=_params("arbitrary", "arbitrary"),
        name="modulation",
    )(cond, w_mod, b_mod.reshape(N_LAYERS, 1, 6 * D))


def _mod_spec(layer, tm):
    return pl.BlockSpec((None, None, 1, 6 * D), lambda i, *_: (layer, _cond_row(i, tm), 0, 0))


def _qkv_body(h_ref, mod_ref, g_ref, w_ref, qn_ref, kn_ref, bd_ref, cos_ref, sin_ref,
              q_ref, k_ref, v_ref, *, nq, nk, dh):
    mod = mod_ref[...]
    xn = _norm_mod(h_ref[...], g_ref[...], mod[:, 0:D], mod[:, D:2 * D]).astype(BF16)
    qkv = jnp.dot(xn, w_ref[...].astype(BF16), preferred_element_type=F32)
    rows = qkv.shape[0]
    cos = cos_ref[...]
    sin = sin_ref[...]
    bd = bd_ref[...]
    quarter = dh // 4
    lane = lax.broadcasted_iota(jnp.int32, (rows, LANES), 1)
    first = (lane % (2 * quarter)) < quarter
    inv_dh = 1.0 / dh

    def head_norm_rope(z, gain):
        ss = jnp.dot((z * z).astype(BF16), bd, preferred_element_type=F32)
        zn = z * lax.rsqrt(ss * inv_dh + EPS)
        out = []
        for j in range(2):
            zz = zn[:, LANES * j:LANES * (j + 1)] * gain
            partner = jnp.where(first, pltpu.roll(zz, LANES - quarter, 1), pltpu.roll(zz, quarter, 1))
            out.append(zz * cos + partner * sin)
        return out

    qgain = qn_ref[...] * (dh ** -0.5)
    kgain = kn_ref[...]
    for c in range(nq // 256):
        lo, hi = head_norm_rope(qkv[:, 256 * c:256 * (c + 1)], qgain)
        q_ref[:, 256 * c:256 * c + LANES] = lo.astype(BF16)
        q_ref[:, 256 * c + LANES:256 * (c + 1)] = hi.astype(BF16)
    for c in range(nk // 256):
        lo, hi = head_norm_rope(qkv[:, nq + 256 * c:nq + 256 * (c + 1)], kgain)
        k_ref[:, 256 * c:256 * c + LANES] = lo
        k_ref[:, 256 * c + LANES:256 * (c + 1)] = hi
    v_ref[...] = qkv[:, nq + nk:]


def _rope_tables(dh):
    half, quarter = dh // 2, dh // 4
    freqs = 1.0 / (ROPE_THETA ** (np.arange(quarter, dtype=np.float32) / quarter))
    pos = np.arange(LAT_S)
    row = (pos // GRID_W).astype(np.float32)
    col = (pos % GRID_W).astype(np.float32)
    ang_r = (row[:, None] * freqs[None, :]).astype(np.float32)
    ang_c = (col[:, None] * freqs[None, :]).astype(np.float32)
    cos = np.concatenate([np.cos(ang_r), np.cos(ang_r), np.cos(ang_c), np.cos(ang_c)], axis=1)
    sin = np.concatenate([-np.sin(ang_r), np.sin(ang_r), -np.sin(ang_c), np.sin(ang_c)], axis=1)
    reps = LANES // dh
    cos = np.tile(cos.astype(np.float32), (1, reps))
    sin = np.tile(sin.astype(np.float32), (1, reps))
    cos = np.concatenate([np.ones_like(cos), cos], axis=0)
    sin = np.concatenate([np.zeros_like(sin), sin], axis=0)
    return jnp.asarray(cos), jnp.asarray(sin)


def _block_diag_ones(dh):
    idx = np.arange(256) // dh
    return jnp.asarray((idx[:, None] == idx[None, :]).astype(np.float32), dtype=BF16)


def _qkv_proj(h, mods, layer, gain, w_qkv, q_norm, k_norm, n_heads, n_kv, dh):
    nq, nk = n_heads * dh, n_kv * dh
    nqkv = nq + 2 * nk
    cos, sin = _rope_tables(dh)
    reps = LANES // dh
    n_ctx_tiles = N_CTX // TM
    lat_tiles = LAT_S // TM

    def rope_idx(i):
        return (jnp.where(i < n_ctx_tiles, 0, lat_tiles + (i - n_ctx_tiles) % lat_tiles), 0)

    return pl.pallas_call(
        functools.partial(_qkv_body, nq=nq, nk=nk, dh=dh),
        grid=(N_TOK // TM,),
        in_specs=[
            pl.BlockSpec((TM, D), lambda i: (i, 0)),
            _mod_spec(layer, TM),
            pl.BlockSpec((1, D), lambda i: (0, 0)),
            pl.BlockSpec((D, nqkv), lambda i: (0, 0)),
            pl.BlockSpec((1, LANES), lambda i: (0, 0)),
            pl.BlockSpec((1, LANES), lambda i: (0, 0)),
            pl.BlockSpec((256, 256), lambda i: (0, 0)),
            pl.BlockSpec((TM, LANES), rope_idx),
            pl.BlockSpec((TM, LANES), rope_idx),
        ],
        out_specs=[
            pl.BlockSpec((TM, nq), lambda i: (i, 0)),
            pl.BlockSpec((TM, nk), lambda i: (i, 0)),
            pl.BlockSpec((TM, nk), lambda i: (i, 0)),
        ],
        out_shape=[
            jax.ShapeDtypeStruct((N_TOK, nq), BF16),
            jax.ShapeDtypeStruct((N_TOK, nk), F32),
            jax.ShapeDtypeStruct((N_TOK, nk), F32),
        ],
        compiler_params=_params("arbitrary"),
        name=f"qkv_l{layer}",
    )(h, mods.reshape(N_LAYERS, N_COND, 1, 6 * D), gain.reshape(1, D), w_qkv,
      jnp.tile(q_norm, reps).reshape(1, LANES), jnp.tile(k_norm, reps).reshape(1, LANES),
      _block_diag_ones(dh), cos, sin)


def _attend(q, segs, sink):
    scores = []
    m = None
    for k, _, mask in segs:
        s = lax.dot_general(q, k, (((1,), (1,)), ((), ())), preferred_element_type=F32)
        if mask is not None:
            s = jnp.where(mask, s, NEG_BIG)
        scores.append(s)
        sm = jnp.max(s, axis=-1, keepdims=True)
        m = sm if m is None else jnp.maximum(m, sm)
    if sink is not None:
        m = jnp.maximum(m, sink)
    den = None
    acc = None
    for s, (_, v, _) in zip(scores, segs):
        p = jnp.exp(s - m)
        ps = jnp.sum(p, axis=-1, keepdims=True)
        pv = jnp.dot(p.astype(BF16), v, preferred_element_type=F32)
        den = ps if den is None else den + ps
        acc = pv if acc is None else acc + pv
    if sink is not None:
        den = den + jnp.exp(sink - m)
    return acc / den


def _dup_halves(x, kv):
    lane = lax.broadcasted_iota(jnp.int32, x.shape, 1)
    r = pltpu.roll(x, 64, 1)
    lo = lane < 64
    return (jnp.where(lo, x, r) if kv % 2 == 0 else jnp.where(lo, r, x)).astype(BF16)


def _attend_heads64(q_ref, o_ref, sink_ref, kv_segs, masks, tq):
    lane = lax.broadcasted_iota(jnp.int32, (tq, LANES), 1)
    lo = lane < 64
    rowsel = lax.broadcasted_iota(jnp.int32, (4 * tq, 1), 0)
    for kv in range(4):
        segs = []
        for (k, v), mask in zip(kv_segs, masks):
            c = LANES * (kv // 2)
            segs.append((_dup_halves(k[:, c:c + LANES], kv), _dup_halves(v[:, c:c + LANES], kv),
                         None if mask is None else jnp.concatenate([mask] * 4, axis=0)))
        parts = []
        sinks = None
        for pair in range(2):
            j = 2 * kv + pair
            qp = q_ref[:, LANES * j:LANES * (j + 1)]
            zero = jnp.zeros_like(qp)
            parts += [jnp.where(lo, qp, zero), jnp.where(lo, zero, qp)]
        for r in range(4):
            sv = sink_ref[4 * kv + r]
            sinks = jnp.full((4 * tq, 1), sv, F32) if sinks is None else jnp.where(rowsel >= r * tq, sv, sinks)
        out = _attend(jnp.concatenate(parts, axis=0), segs, sinks)
        for pair in range(2):
            j = 2 * kv + pair
            a = out[(2 * pair) * tq:(2 * pair + 1) * tq]
            b = out[(2 * pair + 1) * tq:(2 * pair + 2) * tq]
            o_ref[:, LANES * j:LANES * (j + 1)] = jnp.where(lo, a, b).astype(BF16)


def _attend_heads128(q_ref, o_ref, kv_segs, tq):
    for kv in range(4):
        c = LANES * kv
        segs = [(k[:, c:c + LANES].astype(BF16), v[:, c:c + LANES].astype(BF16), None) for k, v in kv_segs]
        q = jnp.concatenate([q_ref[:, LANES * (2 * kv):LANES * (2 * kv + 1)],
                             q_ref[:, LANES * (2 * kv + 1):LANES * (2 * kv + 2)]], axis=0)
        out = _attend(q, segs, None)
        o_ref[:, LANES * (2 * kv):LANES * (2 * kv + 1)] = out[:tq].astype(BF16)
        o_ref[:, LANES * (2 * kv + 1):LANES * (2 * kv + 2)] = out[tq:].astype(BF16)


def _ctx_attn_a_body(sink_ref, q_ref, k_ref, v_ref, o_ref):
    _attend_heads64(q_ref, o_ref, sink_ref, [(k_ref[...], v_ref[...])], [None], CTX_S)


def _ctx_attn_c_body(q_ref, k_ref, v_ref, o_ref):
    _attend_heads128(q_ref, o_ref, [(k_ref[...], v_ref[...])], CTX_S)


def _lat_attn_a_body(sink_ref, q_ref, k_ref, v_ref, ck_ref, cv_ref, o_ref):
    qi = pl.program_id(1)
    band = TQ + 2 * WINDOW
    ws = pl.multiple_of(jnp.clip(qi * TQ - WINDOW, 0, LAT_S - band), WINDOW)
    qpos = qi * TQ + lax.broadcasted_iota(jnp.int32, (TQ, band), 0)
    kpos = ws + lax.broadcasted_iota(jnp.int32, (TQ, band), 1)
    mask = jnp.abs(qpos - kpos) <= WINDOW
    segs = [(ck_ref[...], cv_ref[...]), (k_ref[pl.ds(ws, band), :], v_ref[pl.ds(ws, band), :])]
    _attend_heads64(q_ref, o_ref, sink_ref, segs, [None, mask], TQ)


def _lat_attn_c_body(q_ref, k_ref, v_ref, ck_ref, cv_ref, o_ref):
    _attend_heads128(q_ref, o_ref, [(ck_ref[...], cv_ref[...]), (k_ref[...], v_ref[...])], TQ)


_SMEM_SPEC = pl.BlockSpec(memory_space=pltpu.SMEM)


def _attention(q, k, v, cache_k, cache_v, sink, layer):
    nk = k.shape[1]
    ck = cache_k.reshape(LAT_B, PAST, nk)
    cv = cache_v.reshape(LAT_B, PAST, nk)
    ctx_specs = [
        pl.BlockSpec((CTX_S, D), lambda b: (b, 0)),
        pl.BlockSpec((CTX_S, nk), lambda b: (b, 0)),
        pl.BlockSpec((CTX_S, nk), lambda b: (b, 0)),
    ]
    qb = LAT_S // TQ
    lat_specs = [
        pl.BlockSpec((TQ, D), lambda b, i: (N_CTX // TQ + b * qb + i, 0)),
        pl.BlockSpec((LAT_S, nk), lambda b, i: (N_CTX // LAT_S + b, 0)),
        pl.BlockSpec((LAT_S, nk), lambda b, i: (N_CTX // LAT_S + b, 0)),
        pl.BlockSpec((None, PAST, nk), lambda b, i: (b, 0, 0)),
        pl.BlockSpec((None, PAST, nk), lambda b, i: (b, 0, 0)),
    ]
    ctx_out = dict(out_specs=pl.BlockSpec((CTX_S, D), lambda b: (b, 0)),
                   out_shape=jax.ShapeDtypeStruct((N_CTX, D), BF16),
                   grid=(CTX_B,), compiler_params=_params("arbitrary"))
    lat_out = dict(out_specs=pl.BlockSpec((TQ, D), lambda b, i: (b * qb + i, 0)),
                   out_shape=jax.ShapeDtypeStruct((N_LAT, D), BF16),
                   grid=(LAT_B, qb), compiler_params=_params("arbitrary", "arbitrary"))
    if sink is not None:
        o_ctx = pl.pallas_call(_ctx_attn_a_body, in_specs=[_SMEM_SPEC] + ctx_specs,
                               name=f"attn_ctx_l{layer}", **ctx_out)(sink, q, k, v)
        o_lat = pl.pallas_call(_lat_attn_a_body, in_specs=[_SMEM_SPEC] + lat_specs,
                               name=f"attn_lat_l{layer}", **lat_out)(sink, q, k, v, ck, cv)
    else:
        o_ctx = pl.pallas_call(_ctx_attn_c_body, in_specs=ctx_specs,
                               name=f"attn_ctx_l{layer}", **ctx_out)(q, k, v)
        o_lat = pl.pallas_call(_lat_attn_c_body, in_specs=lat_specs,
                               name=f"attn_lat_l{layer}", **lat_out)(q, k, v, ck, cv)
    return o_ctx, o_lat


def _oproj_body(oc_ref, ol_ref, h_ref, mod_ref, w_ref, out_ref):
    is_ctx = pl.program_id(0) < N_CTX // TM
    o = jnp.where(is_ctx, oc_ref[...], ol_ref[...])
    mix = jnp.dot(o, w_ref[...].astype(BF16), preferred_element_type=F32)
    out_ref[...] = h_ref[...] + mod_ref[:, 2 * D:3 * D] * mix


def _out_proj(o_ctx, o_lat, h, mods, layer, w_o):
    n_ctx_tiles = N_CTX // TM
    return pl.pallas_call(
        _oproj_body,
        grid=(N_TOK // TM,),
        in_specs=[
            pl.BlockSpec((TM, D), lambda i: (jnp.minimum(i, n_ctx_tiles - 1), 0)),
            pl.BlockSpec((TM, D), lambda i: (jnp.maximum(i - n_ctx_tiles, 0), 0)),
            pl.BlockSpec((TM, D), lambda i: (i, 0)),
            _mod_spec(layer, TM),
            pl.BlockSpec((D, D), lambda i: (0, 0)),
        ],
        out_specs=pl.BlockSpec((TM, D), lambda i: (i, 0)),
        out_shape=jax.ShapeDtypeStruct((N_TOK, D), F32),
        compiler_params=_params("arbitrary"),
        name=f"oproj_l{layer}",
    )(o_ctx, o_lat, h, mods.reshape(N_LAYERS, N_COND, 1, 6 * D), w_o)


FFN_CHUNK = CTX_S
FFN_SLOT = FFN_CHUNK + 16
FFN_NCHUNK = TM_FFN // FFN_CHUNK


def _ffn_body(h_ref, mod_ref, g_ref, wg_ref, wv_ref, bg_ref, bv_ref, kg_ref, kv_ref, cg_ref, cv_ref,
              wd_ref, out_ref, xn_scr, acc_scr, ug_scr, uv_scr, wg_scr, wv_scr, wd_scr):
    i = pl.program_id(0)
    f = pl.program_id(1)

    @pl.when(f == 0)
    def _():
        mod = mod_ref[...]
        xn_scr[...] = _norm_mod(h_ref[...], g_ref[...], mod[:, 3 * D:4 * D], mod[:, 4 * D:5 * D]).astype(BF16)
        acc_scr[...] = jnp.zeros_like(acc_scr)

    wg_scr[...] = wg_ref[...].astype(BF16)
    wv_scr[...] = wv_ref[...].astype(BF16)
    wd_scr[...] = wd_ref[...].astype(BF16)
    is_lat = i >= N_CTX // TM_FFN
    base = [8 + FFN_SLOT * k for k in range(FFN_NCHUNK)]

    for k in range(FFN_NCHUNK):
        xk = xn_scr[FFN_CHUNK * k:FFN_CHUNK * (k + 1), :]
        ug_scr[base[k]:base[k] + FFN_CHUNK, :] = jnp.dot(xk, wg_scr[...], preferred_element_type=F32)
        uv_scr[base[k]:base[k] + FFN_CHUNK, :] = jnp.dot(xk, wv_scr[...], preferred_element_type=F32)

    for scr, b_ref in ((ug_scr, bg_ref), (uv_scr, bv_ref)):
        pad = -b_ref[...]
        tops = [pad] + [jnp.where(is_lat, scr[base[k - 1] + FFN_CHUNK - 1:base[k - 1] + FFN_CHUNK, :], pad)
                        for k in range(1, FFN_NCHUNK)]
        bots = [jnp.where(is_lat, scr[base[k + 1]:base[k + 1] + 1, :], pad)
                for k in range(FFN_NCHUNK - 1)] + [pad]
        for k in range(FFN_NCHUNK):
            scr[base[k] - 1:base[k], :] = tops[k]
            scr[base[k] + FFN_CHUNK:base[k] + FFN_CHUNK + 1, :] = bots[k]

    def conv(scr, b_ref, k_ref, c_ref, k):
        kk = k_ref[...]
        const = c_ref[...] + b_ref[...] * (kk[0:1] + kk[1:2] + kk[2:3])
        lo = base[k]
        return (const + kk[0:1] * scr[lo - 1:lo - 1 + FFN_CHUNK, :] + kk[1:2] * scr[lo:lo + FFN_CHUNK, :]
                + kk[2:3] * scr[lo + 1:lo + 1 + FFN_CHUNK, :])

    for k in range(FFN_NCHUNK):
        gate = conv(ug_scr, bg_ref, kg_ref, cg_ref, k)
        val = conv(uv_scr, bv_ref, kv_ref, cv_ref, k)
        a = (_silu(gate) * val).astype(BF16)
        rows = slice(FFN_CHUNK * k, FFN_CHUNK * (k + 1))
        acc_scr[rows, :] += jnp.dot(a, wd_scr[...], preferred_element_type=F32)

    @pl.when(f == pl.num_programs(1) - 1)
    def _():
        out_ref[...] = h_ref[...] + mod_ref[:, 5 * D:6 * D] * acc_scr[...]


def _ffn(h, mods, layer, gain, w_up, b_up, conv_k, conv_b, w_down):
    nf = D_FF // TF
    b3 = b_up.reshape(N_LAYERS, 1, 2 * D_FF)
    c3 = conv_b.reshape(N_LAYERS, 1, 2 * D_FF)
    col = lambda off: (lambda i, f: (layer, 0, off + f))
    return pl.pallas_call(
        _ffn_body,
        grid=(N_TOK // TM_FFN, nf),
        in_specs=[
            pl.BlockSpec((TM_FFN, D), lambda i, f: (i, 0)),
            _mod_spec(layer, TM_FFN),
            pl.BlockSpec((1, D), lambda i, f: (0, 0)),
            pl.BlockSpec((None, D, TF), col(0)),
            pl.BlockSpec((None, D, TF), col(nf)),
            pl.BlockSpec((None, 1, TF), col(0)),
            pl.BlockSpec((None, 1, TF), col(nf)),
            pl.BlockSpec((None, 3, TF), col(0)),
            pl.BlockSpec((None, 3, TF), col(nf)),
            pl.BlockSpec((None, 1, TF), col(0)),
            pl.BlockSpec((None, 1, TF), col(nf)),
            pl.BlockSpec((None, TF, D), lambda i, f: (layer, f, 0)),
        ],
        out_specs=pl.BlockSpec((TM_FFN, D), lambda i, f: (i, 0)),
        out_shape=jax.ShapeDtypeStruct((N_TOK, D), F32),
        scratch_shapes=[pltpu.VMEM((TM_FFN, D), BF16), pltpu.VMEM((TM_FFN, D), F32),
                        pltpu.VMEM((8 + FFN_SLOT * FFN_NCHUNK, TF), F32),
                        pltpu.VMEM((8 + FFN_SLOT * FFN_NCHUNK, TF), F32),
                        pltpu.VMEM((D, TF), BF16), pltpu.VMEM((D, TF), BF16), pltpu.VMEM((TF, D), BF16)],
        compiler_params=_params("arbitrary", "arbitrary"),
        name=f"ffn_l{layer}",
    )(h, mods.reshape(N_LAYERS, N_COND, 1, 6 * D), gain.reshape(1, D), w_up, w_up, b3, b3,
      conv_k, conv_k, c3, c3, w_down)


def _s5_in_body(h_ref, mod_ref, g_ref, u_ref):
    mod = mod_ref[...]
    u_ref[...] = _norm_mod(h_ref[...], g_ref[...], mod[:, 0:D], mod[:, D:2 * D]).astype(BF16)


def _s5_in(h, mods, layer, gain):
    return pl.pallas_call(
        _s5_in_body,
        grid=(N_TOK // TM,),
        in_specs=[pl.BlockSpec((TM, D), lambda i: (i, 0)), _mod_spec(layer, TM),
                  pl.BlockSpec((1, D), lambda i: (0, 0))],
        out_specs=pl.BlockSpec((TM, D), lambda i: (i, 0)),
        out_shape=jax.ShapeDtypeStruct((N_TOK, D), BF16),
        compiler_params=_params("arbitrary"),
        name="s5_in",
    )(h, mods.reshape(N_LAYERS, N_COND, 1, 6 * D), gain.reshape(1, D))


def _s5_weights_body(lamc_re_ref, lamc_im_ref, lamr_re_ref, lamr_im_ref, ldt_ref,
                     bt_re_ref, bt_im_ref, btr_re_ref, btr_im_ref, ct_re_ref, ct_im_ref,
                     w_ref, a_ref):
    blk = (lax.broadcasted_iota(jnp.int32, (1, 256), 1) // S5_L).astype(F32)
    lane256 = lax.broadcasted_iota(jnp.int32, (S5_C, 256), 1)
    hi = lax.Precision.HIGHEST
    krow = []
    st_rows = {}
    w_rows = {}
    for d in range(2):
        dt = jnp.exp(ldt_ref[d])
        lr = lamc_re_ref[d]
        li = lamc_im_ref[d]

        def powers(expo):
            mag = jnp.exp((lr * dt) * expo)
            ang = (li * dt) * expo
            return mag * jnp.cos(ang), mag * jnp.sin(ang)

        asc_r, asc_i = powers(blk)
        dsc_r, dsc_i = powers(15.0 - blk)
        ar = asc_r[:, S5_L:S5_L + 1]
        ai = asc_i[:, S5_L:S5_L + 1]
        den = lr * lr + li * li
        n_re = ar - 1.0
        f_re = (n_re * lr + ai * li) / den
        f_im = (ai * lr - n_re * li) / den
        bbr = f_re * bt_re_ref[d] - f_im * bt_im_ref[d]
        bbi = f_re * bt_im_ref[d] + f_im * bt_re_ref[d]
        e0r, e0i = (asc_r, asc_i) if d == 0 else (dsc_r, dsc_i)
        e1r = e0r * ar - e0i * ai
        e1i = e0r * ai + e0i * ar
        pr, pi = (dsc_r, dsc_i) if d == 0 else (asc_r, asc_i)
        st_rows[("re", d)] = pr * bbr - pi * bbi
        st_rows[("im", d)] = pr * bbi + pi * bbr
        ctr = ct_re_ref[d]
        cti = ct_im_ref[d]
        k_re = ctr * e0r - cti * e0i
        k_imneg = -(ctr * e0i + cti * e0r)
        w_rows[("re", d)] = ctr * e1r - cti * e1i
        w_rows[("im", d)] = -(ctr * e1i + cti * e1r)
        lrr = lamr_re_ref[d:d + 1, :]
        lir = lamr_im_ref[d:d + 1, :]
        magr = jnp.exp(lrr * dt)
        arr = magr * jnp.cos(lir * dt)
        air = magr * jnp.sin(lir * dt)
        denr = lrr * lrr + lir * lir
        nr = arr - 1.0
        fr = (nr * lrr + air * lir) / denr
        fi = (air * lrr - nr * lir) / denr
        bbr_row = fr * btr_re_ref[d] - fi * btr_im_ref[d]
        bbi_row = fr * btr_im_ref[d] + fi * btr_re_ref[d]
        krow.append(jnp.dot(bbr_row, k_re, precision=hi, preferred_element_type=F32)
                    + jnp.dot(bbi_row, k_imneg, precision=hi, preferred_element_type=F32))
        mag16 = jnp.exp(lrr * dt * 16.0)
        a_ref[2 * d:2 * d + 1, :] = mag16 * jnp.cos(lir * dt * 16.0)
        a_ref[2 * d + 1:2 * d + 2, :] = mag16 * jnp.sin(lir * dt * 16.0)
    a_ref[4:8, :] = jnp.zeros((4, S5_P), F32)

    for s in range(S5_L):
        fwd = krow[0] if s == 0 else jnp.where(lane256 >= S5_C * s, pltpu.roll(krow[0], S5_C * s, 1), 0.0)
        bwd = krow[1] if s == S5_L - 1 else jnp.where(lane256 < S5_C * (s + 1),
                                                      pltpu.roll(krow[1], S5_C * (s + 1), 1), 0.0)
        w_ref[S5_C * s:S5_C * (s + 1), 0:256] = (fwd + bwd).astype(BF16)
    order = [("re", 0), ("re", 1), ("im", 0), ("im", 1)]
    st = jnp.concatenate([st_rows[o] for o in order], axis=0)
    w_ref[:, 256:512] = st.T.astype(BF16)
    w_ref[:, 512:768] = jnp.concatenate([w_rows[o] for o in order], axis=0).astype(BF16)


def _s5_weights(lam_re, lam_im, log_dt, b_re, b_im, c_re, c_im):
    col = lambda x: x.transpose(1, 0, 2).reshape(S5_G, 2, S5_P, 1)
    row = lambda x: x.transpose(1, 0, 2)
    ldt = log_dt.transpose(1, 0).reshape(S5_G, 2, 1, 1)
    b_tiled = lambda x: jnp.tile(x.transpose(1, 0, 2, 3), (1, 1, 1, S5_L))
    b_rowf = lambda x: x.transpose(1, 0, 3, 2)
    c_tiled = lambda x: jnp.tile(x.transpose(1, 0, 3, 2), (1, 1, 1, S5_L))
    g4 = lambda *tail: pl.BlockSpec((None,) + tail, lambda g: (g,) + (0,) * len(tail))
    return pl.pallas_call(
        _s5_weights_body,
        grid=(S5_G,),
        in_specs=[g4(2, S5_P, 1), g4(2, S5_P, 1), g4(2, S5_P), g4(2, S5_P), g4(2, 1, 1),
                  g4(2, S5_P, 256), g4(2, S5_P, 256), g4(2, S5_C, S5_P), g4(2, S5_C, S5_P),
                  g4(2, S5_P, 256), g4(2, S5_P, 256)],
        out_specs=[g4(256, 768), g4(8, S5_P)],
        out_shape=[jax.ShapeDtypeStruct((S5_G, 256, 768), BF16),
                   jax.ShapeDtypeStruct((S5_G, 8, S5_P), F32)],
        compiler_params=_params("arbitrary"),
        name="s5_weights",
    )(col(lam_re), col(lam_im), row(lam_re), row(lam_im), ldt,
      b_tiled(b_re), b_tiled(b_im), b_rowf(b_re), b_rowf(b_im), c_tiled(c_re), c_tiled(c_im))


def _s5_scan(d, a_re, a_im, init_re, init_im, hin_scr, n_chunks, rows):
    lane = lax.broadcasted_iota(jnp.int32, (rows, LANES), 1)
    fwd = lane < S5_P
    hr, hi = init_re, init_im
    for c in range(n_chunks):
        cf = slice(c * rows, (c + 1) * rows)
        cb = slice((n_chunks - 1 - c) * rows, (n_chunks - c) * rows)
        hin_scr[cf, 0:S5_P] = hr[:, 0:S5_P]
        hin_scr[cb, S5_P:LANES] = hr[:, S5_P:LANES]
        hin_scr[cf, LANES:LANES + S5_P] = hi[:, 0:S5_P]
        hin_scr[cb, LANES + S5_P:2 * LANES] = hi[:, S5_P:LANES]
        dr = jnp.where(fwd, d[cf, 0:LANES], d[cb, 0:LANES])
        di = jnp.where(fwd, d[cf, LANES:2 * LANES], d[cb, LANES:2 * LANES])
        hr, hi = hr * a_re - hi * a_im + dr, hr * a_im + hi * a_re + di
    return hr, hi


def _s5_core_body(xc_ref, xl_ref, w_ref, a_ref, s0_ref, yc_ref, yl_ref, fin_ref, hc_scr, hl_scr):
    a_re = a_ref[0:1, :]
    a_im = a_ref[1:2, :]
    w_ts = w_ref[:, 0:512]
    w_w = w_ref[:, 512:768]

    ts = jnp.dot(xc_ref[...], w_ts, preferred_element_type=F32)
    zero = jnp.zeros((CTX_B, LANES), F32)
    fr, fi = _s5_scan(ts[:, 256:512], a_re, a_im, zero, zero, hc_scr, CTX_S // S5_L, CTX_B)
    fin_ref[:, 0:LANES] = fr
    fin_ref[:, LANES:2 * LANES] = fi
    yc_ref[...] = ts[:, 0:256] + jnp.dot(hc_scr[...].astype(BF16), w_w, preferred_element_type=F32)

    ts = jnp.dot(xl_ref[...], w_ts, preferred_element_type=F32)
    _s5_scan(ts[:, 256:512], a_re, a_im, s0_ref[:, 0:LANES], s0_ref[:, LANES:2 * LANES], hl_scr,
             LAT_S // S5_L, LAT_B)
    yl_ref[...] = ts[:, 0:256] + jnp.dot(hl_scr[...].astype(BF16), w_w, preferred_element_type=F32)


def _s5_core(x_ctx, x_lat, wall, avec, s0):
    rc = N_CTX // S5_L
    rl = N_LAT // S5_L
    g3 = lambda a, b: pl.BlockSpec((None, a, b), lambda g: (g, 0, 0))
    return pl.pallas_call(
        _s5_core_body,
        grid=(S5_G,),
        in_specs=[g3(rc, 256), g3(rl, 256), g3(256, 768), g3(8, LANES), g3(LAT_B, 256)],
        out_specs=[g3(rc, 256), g3(rl, 256), g3(CTX_B, 256)],
        out_shape=[jax.ShapeDtypeStruct((S5_G, rc, 256), F32),
                   jax.ShapeDtypeStruct((S5_G, rl, 256), F32),
                   jax.ShapeDtypeStruct((S5_G, CTX_B, 256), F32)],
        scratch_shapes=[pltpu.VMEM((rc, 256), F32), pltpu.VMEM((rl, 256), F32)],
        compiler_params=_params("arbitrary"),
        name="s5_core",
    )(x_ctx, x_lat, wall, avec, s0)


def _gelu_tanh(x):
    return 0.5 * x * (1.0 + jnp.tanh(math.sqrt(2.0 / math.pi) * (x + 0.044715 * (x * x * x))))


def _s5_out_body(h_ref, y_ref, mod_ref, g_ref, dskip_ref, w_ref, out_ref):
    mod = mod_ref[...]
    h = h_ref[...]
    u = _norm_mod(h, g_ref[...], mod[:, 0:D], mod[:, D:2 * D])
    y = u * dskip_ref[...] + y_ref[...]
    g = _gelu_tanh(y).astype(BF16)
    hh = jnp.dot(g, w_ref[...].astype(BF16), preferred_element_type=F32)
    out_ref[...] = h + mod[:, 2 * D:3 * D] * (hh[:, 0:D] * jax.nn.sigmoid(hh[:, D:2 * D]))


def _s5_out(h, y, mods, layer, gain, d_skip, w_glu):
    return pl.pallas_call(
        _s5_out_body,
        grid=(N_TOK // TM,),
        in_specs=[pl.BlockSpec((TM, D), lambda i: (i, 0)), pl.BlockSpec((TM, D), lambda i: (i, 0)),
                  _mod_spec(layer, TM), pl.BlockSpec((1, D), lambda i: (0, 0)),
                  pl.BlockSpec((1, D), lambda i: (0, 0)), pl.BlockSpec((D, 2 * D), lambda i: (0, 0))],
        out_specs=pl.BlockSpec((TM, D), lambda i: (i, 0)),
        out_shape=jax.ShapeDtypeStruct((N_TOK, D), F32),
        compiler_params=_params("arbitrary"),
        name="s5_out",
    )(h, y, mods.reshape(N_LAYERS, N_COND, 1, 6 * D), gain.reshape(1, D), d_skip.reshape(1, D), w_glu)


def _to_chunks(u, batch, seq):
    x = u.reshape(batch, seq // S5_L, S5_L, S5_G, S5_C).transpose(3, 1, 0, 2, 4)
    return x.reshape(S5_G, (seq // S5_L) * batch, S5_L * S5_C)


def _from_chunks(y, batch, seq):
    x = y.reshape(S5_G, seq // S5_L, batch, S5_L, S5_C).transpose(2, 1, 3, 0, 4)
    return x.reshape(batch * seq, D)


def _s5_mixer(h, mods, layer, gain, state, lam_re, lam_im, log_dt, b_re, b_im, c_re, c_im, d_skip, w_glu):
    u = _s5_in(h, mods, layer, gain)
    wall, avec = _s5_weights(lam_re, lam_im, log_dt, b_re, b_im, c_re, c_im)
    a2 = jnp.stack([jnp.concatenate([avec[:, 0], avec[:, 2]], axis=-1),
                    jnp.concatenate([avec[:, 1], avec[:, 3]], axis=-1)], axis=1)
    a2 = jnp.pad(a2, ((0, 0), (0, 6), (0, 0)))
    s0 = state.transpose(3, 0, 2, 1, 4).reshape(S5_G, LAT_B, 4 * S5_P)
    y_ctx, y_lat, fin = _s5_core(_to_chunks(u[:N_CTX], CTX_B, CTX_S), _to_chunks(u[N_CTX:], LAT_B, LAT_S),
                                 wall, a2, s0)
    y = jnp.concatenate([_from_chunks(y_ctx, CTX_B, CTX_S), _from_chunks(y_lat, LAT_B, LAT_S)], axis=0)
    new_state = fin.reshape(S5_G, CTX_B, 2, 2, S5_P).transpose(1, 3, 2, 0, 4)
    return _s5_out(h, y, mods, layer, gain, d_skip, w_glu), new_state


def kernel(x_prompt, x_sample, cache_l0_k, cache_l0_v, state_l1, cache_l2_k, cache_l2_v, cache_l3_k, cache_l3_v, c, c_ctx, norm1, norm2, w_mod, b_mod, w_up, b_up, conv_k, conv_b, w_down, l0_w_qkv, l0_q_norm, l0_k_norm, l0_sink, l0_w_o, l1_lam_re, l1_lam_im, l1_log_dt, l1_b_re, l1_b_im, l1_c_re, l1_c_im, l1_d_skip, l1_w_glu, l2_w_qkv, l2_q_norm, l2_k_norm, l2_w_o, l3_w_qkv, l3_q_norm, l3_k_norm, l3_sink, l3_w_o):
    h = jnp.concatenate([x_prompt.reshape(N_CTX, D), x_sample.reshape(N_LAT, D)], axis=0)
    cond = jnp.concatenate([c_ctx[None, :], c, jnp.zeros((N_COND - 1 - LAT_B, D), F32)], axis=0)
    mods = _modulation(cond, w_mod, b_mod)

    attn_layers = {
        0: (l0_w_qkv, l0_q_norm, l0_k_norm, l0_sink, l0_w_o, cache_l0_k, cache_l0_v, 16, 4, 64),
        2: (l2_w_qkv, l2_q_norm, l2_k_norm, None, l2_w_o, cache_l2_k, cache_l2_v, 8, 4, 128),
        3: (l3_w_qkv, l3_q_norm, l3_k_norm, l3_sink, l3_w_o, cache_l3_k, cache_l3_v, 16, 4, 64),
    }
    new_kv = {}
    new_state = None
    for layer in range(N_LAYERS):
        if layer in attn_layers:
            w_qkv, q_norm, k_norm, sink, w_o, ck, cv, n_heads, n_kv, dh = attn_layers[layer]
            q, k, v = _qkv_proj(h, mods, layer, norm1[layer], w_qkv, q_norm, k_norm, n_heads, n_kv, dh)
            new_kv[layer] = (k[:N_CTX].reshape(CTX_B, CTX_S, n_kv, dh), v[:N_CTX].reshape(CTX_B, CTX_S, n_kv, dh))
            o_ctx, o_lat = _attention(q, k, v, ck, cv, sink, layer)
            h = _out_proj(o_ctx, o_lat, h, mods, layer, w_o)
        else:
            h, new_state = _s5_mixer(h, mods, layer, norm1[layer], state_l1, l1_lam_re, l1_lam_im, l1_log_dt,
                                     l1_b_re, l1_b_im, l1_c_re, l1_c_im, l1_d_skip, l1_w_glu)
        h = _ffn(h, mods, layer, norm2[layer], w_up, b_up, conv_k, conv_b, w_down)

    y_prompt = h[:N_CTX].reshape(CTX_B, CTX_S, D)
    y_sample = h[N_CTX:].reshape(LAT_B, LAT_S, D)
    return (y_prompt, y_sample, new_kv[0][0], new_kv[0][1], new_state,
            new_kv[2][0], new_kv[2][1], new_kv[3][0], new_kv[3][1])
```

```python
import functools
import math

import jax
import jax.numpy as jnp
import numpy as np
from jax import lax
from jax.experimental import pallas as pl
from jax.experimental.pallas import tpu as pltpu

F32 = jnp.float32
BF16 = jnp.bfloat16

D = 1024
N_LAYERS = 4
CTX_B, CTX_S = 32, 256
LAT_B, LAT_S = 8, 1024
PAST = 512
N_CTX = CTX_B * CTX_S
N_LAT = LAT_B * LAT_S
N_TOK = N_CTX + N_LAT
GRID_W = 64
WINDOW = 128
ROPE_THETA = 10000.0
EPS = 1e-6
D_FF = 2816
N_COND = 16
LAT_ROW0 = 8

S5_G = 64
S5_C = 16
S5_P = 64
S5_L = 16

VMEM_LIMIT = 56 * 1024 * 1024
LANES = 128
NEG_BIG = -1e30

TM = 512
TM_FFN = 1024
TF = 256
TQ = 256


def _params(*sem):
    return pltpu.CompilerParams(dimension_semantics=sem, vmem_limit_bytes=VMEM_LIMIT)


def _cond_row(i, tm):
    n_ctx_tiles = N_CTX // tm
    return jnp.where(i < n_ctx_tiles, 0, LAT_ROW0 + (i - n_ctx_tiles) // (LAT_S // tm))


def _norm_mod(x, gain, shift, scale):
    ms = jnp.mean(x * x, axis=-1, keepdims=True)
    return (x * lax.rsqrt(ms + EPS) * gain) * (1.0 + scale) + shift


def _silu(x):
    return x * jax.nn.sigmoid(x)


def _mod_body(cond_ref, w_ref, b_ref, o_ref):
    s = _silu(cond_ref[...]).astype(BF16)
    o_ref[...] = jnp.dot(s, w_ref[...].astype(BF16), preferred_element_type=F32) + b_ref[...]


def _modulation(cond, w_mod, b_mod):
    tn = 1536
    return pl.pallas_call(
        _mod_body,
        grid=(N_LAYERS, 6 * D // tn),
        in_specs=[
            pl.BlockSpec((N_COND, D), lambda l, n: (0, 0)),
            pl.BlockSpec((None, D, tn), lambda l, n: (l, 0, n)),
            pl.BlockSpec((None, 1, tn), lambda l, n: (l, 0, n)),
        ],
        out_specs=pl.BlockSpec((None, N_COND, tn), lambda l, n: (l, 0, n)),
        out_shape=jax.ShapeDtypeStruct((N_LAYERS, N_COND, 6 * D), F32),
        compiler_params=_params("arbitrary", "arbitrary"),
        name="modulation",
    )(cond, w_mod, b_mod.reshape(N_LAYERS, 1, 6 * D))


def _mod_spec(layer, tm):
    return pl.BlockSpec((None, None, 1, 6 * D), lambda i, *_: (layer, _cond_row(i, tm), 0, 0))


def _qkv_body(h_ref, mod_ref, g_ref, w_ref, qn_ref, kn_ref, bd_ref, cos_ref, sin_ref,
              q_ref, k_ref, v_ref, *, nq, nk, dh):
    mod = mod_ref[...]
    xn = _norm_mod(h_ref[...], g_ref[...], mod[:, 0:D], mod[:, D:2 * D]).astype(BF16)
    qkv = jnp.dot(xn, w_ref[...].astype(BF16), preferred_element_type=F32)
    rows = qkv.shape[0]
    cos = cos_ref[...]
    sin = sin_ref[...]
    bd = bd_ref[...]
    quarter = dh // 4
    lane = lax.broadcasted_iota(jnp.int32, (rows, LANES), 1)
    first = (lane % (2 * quarter)) < quarter
    inv_dh = 1.0 / dh

    def head_norm_rope(z, gain):
        ss = jnp.dot((z * z).astype(BF16), bd, preferred_element_type=F32)
        zn = z * lax.rsqrt(ss * inv_dh + EPS)
        out = []
        for j in range(2):
            zz = zn[:, LANES * j:LANES * (j + 1)] * gain
            partner = jnp.where(first, pltpu.roll(zz, LANES - quarter, 1), pltpu.roll(zz, quarter, 1))
            out.append(zz * cos + partner * sin)
        return out

    qgain = qn_ref[...] * (dh ** -0.5)
    kgain = kn_ref[...]
    for c in range(nq // 256):
        lo, hi = head_norm_rope(qkv[:, 256 * c:256 * (c + 1)], qgain)
        q_ref[:, 256 * c:256 * c + LANES] = lo.astype(BF16)
        q_ref[:, 256 * c + LANES:256 * (c + 1)] = hi.astype(BF16)
    for c in range(nk // 256):
        lo, hi = head_norm_rope(qkv[:, nq + 256 * c:nq + 256 * (c + 1)], kgain)
        k_ref[:, 256 * c:256 * c + LANES] = lo
        k_ref[:, 256 * c + LANES:256 * (c + 1)] = hi
    v_ref[...] = qkv[:, nq + nk:]


def _rope_tables(dh):
    half, quarter = dh // 2, dh // 4
    freqs = 1.0 / (ROPE_THETA ** (np.arange(quarter, dtype=np.float32) / quarter))
    pos = np.arange(LAT_S)
    row = (pos // GRID_W).astype(np.float32)
    col = (pos % GRID_W).astype(np.float32)
    ang_r = (row[:, None] * freqs[None, :]).astype(np.float32)
    ang_c = (col[:, None] * freqs[None, :]).astype(np.float32)
    cos = np.concatenate([np.cos(ang_r), np.cos(ang_r), np.cos(ang_c), np.cos(ang_c)], axis=1)
    sin = np.concatenate([-np.sin(ang_r), np.sin(ang_r), -np.sin(ang_c), np.sin(ang_c)], axis=1)
    reps = LANES // dh
    cos = np.tile(cos.astype(np.float32), (1, reps))
    sin = np.tile(sin.astype(np.float32), (1, reps))
    cos = np.concatenate([np.ones_like(cos), cos], axis=0)
    sin = np.concatenate([np.zeros_like(sin), sin], axis=0)
    return jnp.asarray(cos), jnp.asarray(sin)


def _block_diag_ones(dh):
    idx = np.arange(256) // dh
    return jnp.asarray((idx[:, None] == idx[None, :]).astype(np.float32), dtype=BF16)


def _qkv_proj(h, mods, layer, gain, w_qkv, q_norm, k_norm, n_heads, n_kv, dh):
    nq, nk = n_heads * dh, n_kv * dh
    nqkv = nq + 2 * nk
    cos, sin = _rope_tables(dh)
    reps = LANES // dh
    n_ctx_tiles = N_CTX // TM
    lat_tiles = LAT_S // TM

    def rope_idx(i):
        return (jnp.where(i < n_ctx_tiles, 0, lat_tiles + (i - n_ctx_tiles) % lat_tiles), 0)

    return pl.pallas_call(
        functools.partial(_qkv_body, nq=nq, nk=nk, dh=dh),
        grid=(N_TOK // TM,),
        in_specs=[
            pl.BlockSpec((TM, D), lambda i: (i, 0)),
            _mod_spec(layer, TM),
            pl.BlockSpec((1, D), lambda i: (0, 0)),
            pl.BlockSpec((D, nqkv), lambda i: (0, 0)),
            pl.BlockSpec((1, LANES), lambda i: (0, 0)),
            pl.BlockSpec((1, LANES), lambda i: (0, 0)),
            pl.BlockSpec((256, 256), lambda i: (0, 0)),
            pl.BlockSpec((TM, LANES), rope_idx),
            pl.BlockSpec((TM, LANES), rope_idx),
        ],
        out_specs=[
            pl.BlockSpec((TM, nq), lambda i: (i, 0)),
            pl.BlockSpec((TM, nk), lambda i: (i, 0)),
            pl.BlockSpec((TM, nk), lambda i: (i, 0)),
        ],
        out_shape=[
            jax.ShapeDtypeStruct((N_TOK, nq), BF16),
            jax.ShapeDtypeStruct((N_TOK, nk), F32),
            jax.ShapeDtypeStruct((N_TOK, nk), F32),
        ],
        compiler_params=_params("arbitrary"),
        name=f"qkv_l{layer}",
    )(h, mods.reshape(N_LAYERS, N_COND, 1, 6 * D), gain.reshape(1, D), w_qkv,
      jnp.tile(q_norm, reps).reshape(1, LANES), jnp.tile(k_norm, reps).reshape(1, LANES),
      _block_diag_ones(dh), cos, sin)


def _attend(q, segs, sink):
    scores = []
    m = None
    for k, _, mask in segs:
        s = lax.dot_general(q, k, (((1,), (1,)), ((), ())), preferred_element_type=F32)
        if mask is not None:
            s = jnp.where(mask, s, NEG_BIG)
        scores.append(s)
        sm = jnp.max(s, axis=-1, keepdims=True)
        m = sm if m is None else jnp.maximum(m, sm)
    if sink is not None:
        m = jnp.maximum(m, sink)
    den = None
    acc = None
    for s, (_, v, _) in zip(scores, segs):
        p = jnp.exp(s - m)
        ps = jnp.sum(p, axis=-1, keepdims=True)
        pv = jnp.dot(p.astype(BF16), v, preferred_element_type=F32)
        den = ps if den is None else den + ps
        acc = pv if acc is None else acc + pv
    if sink is not None:
        den = den + jnp.exp(sink - m)
    return acc / den


def _dup_halves(x, kv):
    lane = lax.broadcasted_iota(jnp.int32, x.shape, 1)
    r = pltpu.roll(x, 64, 1)
    lo = lane < 64
    return (jnp.where(lo, x, r) if kv % 2 == 0 else jnp.where(lo, r, x)).astype(BF16)


def _attend_heads64(q_ref, o_ref, sink_ref, kv_segs, masks, tq):
    lane = lax.broadcasted_iota(jnp.int32, (tq, LANES), 1)
    lo = lane < 64
    rowsel = lax.broadcasted_iota(jnp.int32, (4 * tq, 1), 0)
    for kv in range(4):
        segs = []
        for (k, v), mask in zip(kv_segs, masks):
            c = LANES * (kv // 2)
            segs.append((_dup_halves(k[:, c:c + LANES], kv), _dup_halves(v[:, c:c + LANES], kv),
                         None if mask is None else jnp.concatenate([mask] * 4, axis=0)))
        parts = []
        sinks = None
        for pair in range(2):
            j = 2 * kv + pair
            qp = q_ref[:, LANES * j:LANES * (j + 1)]
            zero = jnp.zeros_like(qp)
            parts += [jnp.where(lo, qp, zero), jnp.where(lo, zero, qp)]
        for r in range(4):
            sv = sink_ref[4 * kv + r]
            sinks = jnp.full((4 * tq, 1), sv, F32) if sinks is None else jnp.where(rowsel >= r * tq, sv, sinks)
        out = _attend(jnp.concatenate(parts, axis=0), segs, sinks)
        for pair in range(2):
            j = 2 * kv + pair
            a = out[(2 * pair) * tq:(2 * pair + 1) * tq]
            b = out[(2 * pair + 1) * tq:(2 * pair + 2) * tq]
            o_ref[:, LANES * j:LANES * (j + 1)] = jnp.where(lo, a, b).astype(BF16)


def _attend_heads128(q_ref, o_ref, kv_segs, tq):
    for kv in range(4):
        c = LANES * kv
        segs = [(k[:, c:c + LANES].astype(BF16), v[:, c:c + LANES].astype(BF16), None) for k, v in kv_segs]
        q = jnp.concatenate([q_ref[:, LANES * (2 * kv):LANES * (2 * kv + 1)],
                             q_ref[:, LANES * (2 * kv + 1):LANES * (2 * kv + 2)]], axis=0)
        out = _attend(q, segs, None)
        o_ref[:, LANES * (2 * kv):LANES * (2 * kv + 1)] = out[:tq].astype(BF16)
        o_ref[:, LANES * (2 * kv + 1):LANES * (2 * kv + 2)] = out[tq:].astype(BF16)


def _ctx_attn_a_body(sink_ref, q_ref, k_ref, v_ref, o_ref):
    _attend_heads64(q_ref, o_ref, sink_ref, [(k_ref[...], v_ref[...])], [None], CTX_S)


def _ctx_attn_c_body(q_ref, k_ref, v_ref, o_ref):
    _attend_heads128(q_ref, o_ref, [(k_ref[...], v_ref[...])], CTX_S)


def _lat_attn_a_body(sink_ref, q_ref, k_ref, v_ref, ck_ref, cv_ref, o_ref):
    qi = pl.program_id(1)
    band = TQ + 2 * WINDOW
    ws = pl.multiple_of(jnp.clip(qi * TQ - WINDOW, 0, LAT_S - band), WINDOW)
    qpos = qi * TQ + lax.broadcasted_iota(jnp.int32, (TQ, band), 0)
    kpos = ws + lax.broadcasted_iota(jnp.int32, (TQ, band), 1)
    mask = jnp.abs(qpos - kpos) <= WINDOW
    segs = [(ck_ref[...], cv_ref[...]), (k_ref[pl.ds(ws, band), :], v_ref[pl.ds(ws, band), :])]
    _attend_heads64(q_ref, o_ref, sink_ref, segs, [None, mask], TQ)


def _lat_attn_c_body(q_ref, k_ref, v_ref, ck_ref, cv_ref, o_ref):
    _attend_heads128(q_ref, o_ref, [(ck_ref[...], cv_ref[...]), (k_ref[...], v_ref[...])], TQ)


_SMEM_SPEC = pl.BlockSpec(memory_space=pltpu.SMEM)


def _attention(q, k, v, cache_k, cache_v, sink, layer):
    nk = k.shape[1]
    ck = cache_k.reshape(LAT_B, PAST, nk)
    cv = cache_v.reshape(LAT_B, PAST, nk)
    ctx_specs = [
        pl.BlockSpec((CTX_S, D), lambda b: (b, 0)),
        pl.BlockSpec((CTX_S, nk), lambda b: (b, 0)),
        pl.BlockSpec((CTX_S, nk), lambda b: (b, 0)),
    ]
    qb = LAT_S // TQ
    lat_specs = [
        pl.BlockSpec((TQ, D), lambda b, i: (N_CTX // TQ + b * qb + i, 0)),
        pl.BlockSpec((LAT_S, nk), lambda b, i: (N_CTX // LAT_S + b, 0)),
        pl.BlockSpec((LAT_S, nk), lambda b, i: (N_CTX // LAT_S + b, 0)),
        pl.BlockSpec((None, PAST, nk), lambda b, i: (b, 0, 0)),
        pl.BlockSpec((None, PAST, nk), lambda b, i: (b, 0, 0)),
    ]
    ctx_out = dict(out_specs=pl.BlockSpec((CTX_S, D), lambda b: (b, 0)),
                   out_shape=jax.ShapeDtypeStruct((N_CTX, D), BF16),
                   grid=(CTX_B,), compiler_params=_params("arbitrary"))
    lat_out = dict(out_specs=pl.BlockSpec((TQ, D), lambda b, i: (b * qb + i, 0)),
                   out_shape=jax.ShapeDtypeStruct((N_LAT, D), BF16),
                   grid=(LAT_B, qb), compiler_params=_params("arbitrary", "arbitrary"))
    if sink is not None:
        o_ctx = pl.pallas_call(_ctx_attn_a_body, in_specs=[_SMEM_SPEC] + ctx_specs,
                               name=f"attn_ctx_l{layer}", **ctx_out)(sink, q, k, v)
        o_lat = pl.pallas_call(_lat_attn_a_body, in_specs=[_SMEM_SPEC] + lat_specs,
                               name=f"attn_lat_l{layer}", **lat_out)(sink, q, k, v, ck, cv)
    else:
        o_ctx = pl.pallas_call(_ctx_attn_c_body, in_specs=ctx_specs,
                               name=f"attn_ctx_l{layer}", **ctx_out)(q, k, v)
        o_lat = pl.pallas_call(_lat_attn_c_body, in_specs=lat_specs,
                               name=f"attn_lat_l{layer}", **lat_out)(q, k, v, ck, cv)
    return o_ctx, o_lat


def _oproj_body(oc_ref, ol_ref, h_ref, mod_ref, w_ref, out_ref):
    is_ctx = pl.program_id(0) < N_CTX // TM
    o = jnp.where(is_ctx, oc_ref[...], ol_ref[...])
    mix = jnp.dot(o, w_ref[...].astype(BF16), preferred_element_type=F32)
    out_ref[...] = h_ref[...] + mod_ref[:, 2 * D:3 * D] * mix


def _out_proj(o_ctx, o_lat, h, mods, layer, w_o):
    n_ctx_tiles = N_CTX // TM
    return pl.pallas_call(
        _oproj_body,
        grid=(N_TOK // TM,),
        in_specs=[
            pl.BlockSpec((TM, D), lambda i: (jnp.minimum(i, n_ctx_tiles - 1), 0)),
            pl.BlockSpec((TM, D), lambda i: (jnp.maximum(i - n_ctx_tiles, 0), 0)),
            pl.BlockSpec((TM, D), lambda i: (i, 0)),
            _mod_spec(layer, TM),
            pl.BlockSpec((D, D), lambda i: (0, 0)),
        ],
        out_specs=pl.BlockSpec((TM, D), lambda i: (i, 0)),
        out_shape=jax.ShapeDtypeStruct((N_TOK, D), F32),
        compiler_params=_params("arbitrary"),
        name=f"oproj_l{layer}",
    )(o_ctx, o_lat, h, mods.reshape(N_LAYERS, N_COND, 1, 6 * D), w_o)


FFN_CHUNK = CTX_S
FFN_SLOT = FFN_CHUNK + 16
FFN_NCHUNK = TM_FFN // FFN_CHUNK


def _ffn_body(hc_ref, hl_ref, mod_ref, g_ref, wg_ref, wv_ref, bg_ref, bv_ref, kg_ref, kv_ref, cg_ref, cv_ref,
              wd_ref, out_ref, xn_scr, acc_scr, ug_scr, uv_scr, wg_scr, wv_scr, wd_scr):
    i = pl.program_id(0)
    f = pl.program_id(1)
    is_lat = i >= N_CTX // TM_FFN

    @pl.when(f == 0)
    def _():
        mod = mod_ref[...]
        h = jnp.where(is_lat, hl_ref[...], hc_ref[...])
        xn_scr[...] = _norm_mod(h, g_ref[...], mod[:, 3 * D:4 * D], mod[:, 4 * D:5 * D]).astype(BF16)
        acc_scr[...] = jnp.zeros_like(acc_scr)

    wg_scr[...] = wg_ref[...].astype(BF16)
    wv_scr[...] = wv_ref[...].astype(BF16)
    wd_scr[...] = wd_ref[...].astype(BF16)
    base = [8 + FFN_SLOT * k for k in range(FFN_NCHUNK)]

    for k in range(FFN_NCHUNK):
        xk = xn_scr[FFN_CHUNK * k:FFN_CHUNK * (k + 1), :]
        ug_scr[base[k]:base[k] + FFN_CHUNK, :] = jnp.dot(xk, wg_scr[...], preferred_element_type=F32)
        uv_scr[base[k]:base[k] + FFN_CHUNK, :] = jnp.dot(xk, wv_scr[...], preferred_element_type=F32)

    for scr, b_ref in ((ug_scr, bg_ref), (uv_scr, bv_ref)):
        pad = -b_ref[...]
        tops = [pad] + [jnp.where(is_lat, scr[base[k - 1] + FFN_CHUNK - 1:base[k - 1] + FFN_CHUNK, :], pad)
                        for k in range(1, FFN_NCHUNK)]
        bots = [jnp.where(is_lat, scr[base[k + 1]:base[k + 1] + 1, :], pad)
                for k in range(FFN_NCHUNK - 1)] + [pad]
        for k in range(FFN_NCHUNK):
            scr[base[k] - 1:base[k], :] = tops[k]
            scr[base[k] + FFN_CHUNK:base[k] + FFN_CHUNK + 1, :] = bots[k]

    def conv(scr, b_ref, k_ref, c_ref, k):
        kk = k_ref[...]
        const = c_ref[...] + b_ref[...] * (kk[0:1] + kk[1:2] + kk[2:3])
        lo = base[k]
        return (const + kk[0:1] * scr[lo - 1:lo - 1 + FFN_CHUNK, :] + kk[1:2] * scr[lo:lo + FFN_CHUNK, :]
                + kk[2:3] * scr[lo + 1:lo + 1 + FFN_CHUNK, :])

    for k in range(FFN_NCHUNK):
        gate = conv(ug_scr, bg_ref, kg_ref, cg_ref, k)
        val = conv(uv_scr, bv_ref, kv_ref, cv_ref, k)
        a = (_silu(gate) * val).astype(BF16)
        rows = slice(FFN_CHUNK * k, FFN_CHUNK * (k + 1))
        acc_scr[rows, :] += jnp.dot(a, wd_scr[...], preferred_element_type=F32)

    @pl.when(f == pl.num_programs(1) - 1)
    def _():
        h = jnp.where(is_lat, hl_ref[...], hc_ref[...])
        out_ref[...] = h + mod_ref[:, 5 * D:6 * D] * acc_scr[...]


def _ffn(h, mods, layer, gain, w_up, b_up, conv_k, conv_b, w_down):
    nf = D_FF // TF
    b3 = b_up.reshape(N_LAYERS, 1, 2 * D_FF)
    c3 = conv_b.reshape(N_LAYERS, 1, 2 * D_FF)
    col = lambda off: (lambda i, f: (layer, 0, off + f))
    n_ctx_tiles = N_CTX // TM_FFN
    h_ctx, h_lat, lat_off = (h[0], h[1], 0) if isinstance(h, tuple) else (h, h, n_ctx_tiles)
    return pl.pallas_call(
        _ffn_body,
        grid=(N_TOK // TM_FFN, nf),
        in_specs=[
            pl.BlockSpec((TM_FFN, D), lambda i, f: (jnp.minimum(i, n_ctx_tiles - 1), 0)),
            pl.BlockSpec((TM_FFN, D), lambda i, f: (jnp.maximum(i - n_ctx_tiles, 0) + lat_off, 0)),
            _mod_spec(layer, TM_FFN),
            pl.BlockSpec((1, D), lambda i, f: (0, 0)),
            pl.BlockSpec((None, D, TF), col(0)),
            pl.BlockSpec((None, D, TF), col(nf)),
            pl.BlockSpec((None, 1, TF), col(0)),
            pl.BlockSpec((None, 1, TF), col(nf)),
            pl.BlockSpec((None, 3, TF), col(0)),
            pl.BlockSpec((None, 3, TF), col(nf)),
            pl.BlockSpec((None, 1, TF), col(0)),
            pl.BlockSpec((None, 1, TF), col(nf)),
            pl.BlockSpec((None, TF, D), lambda i, f: (layer, f, 0)),
        ],
        out_specs=pl.BlockSpec((TM_FFN, D), lambda i, f: (i, 0)),
        out_shape=jax.ShapeDtypeStruct((N_TOK, D), F32),
        scratch_shapes=[pltpu.VMEM((TM_FFN, D), BF16), pltpu.VMEM((TM_FFN, D), F32),
                        pltpu.VMEM((8 + FFN_SLOT * FFN_NCHUNK, TF), F32),
                        pltpu.VMEM((8 + FFN_SLOT * FFN_NCHUNK, TF), F32),
                        pltpu.VMEM((D, TF), BF16), pltpu.VMEM((D, TF), BF16), pltpu.VMEM((TF, D), BF16)],
        compiler_params=_params("arbitrary", "arbitrary"),
        name=f"ffn_l{layer}",
    )(h_ctx, h_lat, mods.reshape(N_LAYERS, N_COND, 1, 6 * D), gain.reshape(1, D), w_up, w_up, b3, b3,
      conv_k, conv_k, c3, c3, w_down)


S5_SEQ = 8
S5_CB = 4
CTX_SETS = CTX_B // S5_SEQ
CTX_CHUNKS = CTX_S // S5_L
LAT_CHUNKS = LAT_S // S5_L
S5_CTX_STEPS = CTX_SETS * CTX_CHUNKS // S5_CB
S5_LAT_STEPS = LAT_CHUNKS // S5_CB
S5_STEPS = S5_CTX_STEPS + S5_LAT_STEPS
S5_ROWS = (N_CTX + N_LAT) // S5_L
S5_CTX_ROWS = N_CTX // S5_L


def _s5_tile_specs():
    per = CTX_CHUNKS // S5_CB
    blk = (S5_SEQ, S5_CB, S5_L, D)

    def ctx_idx(n):
        m = jnp.minimum(n, S5_CTX_STEPS - 1)
        return (m // per, m % per, 0, 0)

    return blk, ctx_idx, (lambda n: (jnp.maximum(n - S5_CTX_STEPS, 0)))


def _s5_mod8(mods, layer):
    m = mods[layer]
    return jnp.stack([jnp.broadcast_to(m[0:1], (S5_SEQ, 6 * D)), m[LAT_ROW0:LAT_ROW0 + LAT_B]]).reshape(
        2, S5_SEQ, 1, 6 * D)


def _s5_in_body(hc_ref, hl_ref, mod_ref, g_ref, u_ref):
    is_lat = pl.program_id(0) >= S5_CTX_STEPS
    mod = mod_ref[...]
    gain = g_ref[...]
    for c in range(S5_CB):
        x = jnp.where(is_lat, hl_ref[:, c], hc_ref[:, c])
        u_ref[c] = _norm_mod(x, gain, mod[:, :, 0:D], mod[:, :, D:2 * D]).astype(BF16)


def _s5_in(h, mod8, gain):
    blk, ctx_idx, lat_idx = _s5_tile_specs()
    return pl.pallas_call(
        _s5_in_body,
        grid=(S5_STEPS,),
        in_specs=[pl.BlockSpec(blk, ctx_idx),
                  pl.BlockSpec(blk, lambda n: (N_CTX // N_LAT, lat_idx(n), 0, 0)),
                  pl.BlockSpec((None, S5_SEQ, 1, 6 * D), lambda n: (jnp.where(n >= S5_CTX_STEPS, 1, 0), 0, 0, 0)),
                  pl.BlockSpec((1, D), lambda n: (0, 0))],
        out_specs=pl.BlockSpec((None, S5_CB, S5_SEQ, S5_L, D), lambda n: (n, 0, 0, 0, 0)),
        out_shape=jax.ShapeDtypeStruct((S5_STEPS, S5_CB, S5_SEQ, S5_L, D), BF16),
        compiler_params=_params("arbitrary"),
        name="s5_in",
    )(h.reshape(N_TOK // CTX_S, CTX_CHUNKS, S5_L, D), h.reshape(N_TOK // LAT_S, LAT_CHUNKS, S5_L, D),
      mod8, gain.reshape(1, D))


def _s5_weights_body(lamc_re_ref, lamc_im_ref, lamr_re_ref, lamr_im_ref, ldt_ref,
                     bt_re_ref, bt_im_ref, btr_re_ref, btr_im_ref, ct_re_ref, ct_im_ref,
                     w_ref, a_ref):
    blk = (lax.broadcasted_iota(jnp.int32, (1, 256), 1) // S5_L).astype(F32)
    lane256 = lax.broadcasted_iota(jnp.int32, (S5_C, 256), 1)
    hi = lax.Precision.HIGHEST
    krow = []
    st_rows = {}
    w_rows = {}
    for d in range(2):
        dt = jnp.exp(ldt_ref[d])
        lr = lamc_re_ref[d]
        li = lamc_im_ref[d]

        def powers(expo):
            mag = jnp.exp((lr * dt) * expo)
            ang = (li * dt) * expo
            return mag * jnp.cos(ang), mag * jnp.sin(ang)

        asc_r, asc_i = powers(blk)
        dsc_r, dsc_i = powers(15.0 - blk)
        ar = asc_r[:, S5_L:S5_L + 1]
        ai = asc_i[:, S5_L:S5_L + 1]
        den = lr * lr + li * li
        n_re = ar - 1.0
        f_re = (n_re * lr + ai * li) / den
        f_im = (ai * lr - n_re * li) / den
        bbr = f_re * bt_re_ref[d] - f_im * bt_im_ref[d]
        bbi = f_re * bt_im_ref[d] + f_im * bt_re_ref[d]
        e0r, e0i = (asc_r, asc_i) if d == 0 else (dsc_r, dsc_i)
        e1r = e0r * ar - e0i * ai
        e1i = e0r * ai + e0i * ar
        pr, pi = (dsc_r, dsc_i) if d == 0 else (asc_r, asc_i)
        st_rows[("re", d)] = pr * bbr - pi * bbi
        st_rows[("im", d)] = pr * bbi + pi * bbr
        ctr = ct_re_ref[d]
        cti = ct_im_ref[d]
        k_re = ctr * e0r - cti * e0i
        k_imneg = -(ctr * e0i + cti * e0r)
        w_rows[("re", d)] = ctr * e1r - cti * e1i
        w_rows[("im", d)] = -(ctr * e1i + cti * e1r)
        lrr = lamr_re_ref[d:d + 1, :]
        lir = lamr_im_ref[d:d + 1, :]
        magr = jnp.exp(lrr * dt)
        arr = magr * jnp.cos(lir * dt)
        air = magr * jnp.sin(lir * dt)
        denr = lrr * lrr + lir * lir
        nr = arr - 1.0
        fr = (nr * lrr + air * lir) / denr
        fi = (air * lrr - nr * lir) / denr
        bbr_row = fr * btr_re_ref[d] - fi * btr_im_ref[d]
        bbi_row = fr * btr_im_ref[d] + fi * btr_re_ref[d]
        krow.append(jnp.dot(bbr_row, k_re, precision=hi, preferred_element_type=F32)
                    + jnp.dot(bbi_row, k_imneg, precision=hi, preferred_element_type=F32))
        mag16 = jnp.exp(lrr * dt * 16.0)
        a_ref[2 * d:2 * d + 1, :] = mag16 * jnp.cos(lir * dt * 16.0)
        a_ref[2 * d + 1:2 * d + 2, :] = mag16 * jnp.sin(lir * dt * 16.0)
    a_ref[4:8, :] = jnp.zeros((4, S5_P), F32)

    t_rows = []
    for s in range(S5_L):
        fwd = krow[0] if s == 0 else jnp.where(lane256 >= S5_C * s, pltpu.roll(krow[0], S5_C * s, 1), 0.0)
        bwd = krow[1] if s == S5_L - 1 else jnp.where(lane256 < S5_C * (s + 1),
                                                      pltpu.roll(krow[1], S5_C * (s + 1), 1), 0.0)
        t_rows.append(fwd + bwd)
    order = [("re", 0), ("re", 1), ("im", 0), ("im", 1)]
    w_ref[0:256, :] = jnp.concatenate(t_rows, axis=0).T.astype(BF16)
    w_ref[256:512, :] = jnp.concatenate([st_rows[o] for o in order], axis=0).astype(BF16)
    w_ref[512:768, :] = jnp.concatenate([w_rows[o] for o in order], axis=0).T.astype(BF16)


def _s5_weights(lam_re, lam_im, log_dt, b_re, b_im, c_re, c_im):
    col = lambda x: x.transpose(1, 0, 2).reshape(S5_G, 2, S5_P, 1)
    row = lambda x: x.transpose(1, 0, 2)
    ldt = log_dt.transpose(1, 0).reshape(S5_G, 2, 1, 1)
    b_tiled = lambda x: jnp.tile(x.transpose(1, 0, 2, 3), (1, 1, 1, S5_L))
    b_rowf = lambda x: x.transpose(1, 0, 3, 2)
    c_tiled = lambda x: jnp.tile(x.transpose(1, 0, 3, 2), (1, 1, 1, S5_L))
    g4 = lambda *tail: pl.BlockSpec((None,) + tail, lambda g: (g,) + (0,) * len(tail))
    return pl.pallas_call(
        _s5_weights_body,
        grid=(S5_G,),
        in_specs=[g4(2, S5_P, 1), g4(2, S5_P, 1), g4(2, S5_P), g4(2, S5_P), g4(2, 1, 1),
                  g4(2, S5_P, 256), g4(2, S5_P, 256), g4(2, S5_C, S5_P), g4(2, S5_C, S5_P),
                  g4(2, S5_P, 256), g4(2, S5_P, 256)],
        out_specs=[g4(768, 256), g4(8, S5_P)],
        out_shape=[jax.ShapeDtypeStruct((S5_G, 768, 256), BF16),
                   jax.ShapeDtypeStruct((S5_G, 8, S5_P), F32)],
        compiler_params=_params("arbitrary"),
        name="s5_weights",
    )(col(lam_re), col(lam_im), row(lam_re), row(lam_im), ldt,
      b_tiled(b_re), b_tiled(b_im), b_rowf(b_re), b_rowf(b_im), c_tiled(c_re), c_tiled(c_im))


def _s5_scan(d_scr, r0, a_re, a_im, init_re, init_im, hin_scr, n_chunks):
    lane = lax.broadcasted_iota(jnp.int32, (S5_SEQ, LANES), 1)
    fwd = lane < S5_P
    hr, hi = init_re, init_im
    for c in range(n_chunks):
        cf = slice(r0 + c * S5_SEQ, r0 + (c + 1) * S5_SEQ)
        cb = slice(r0 + (n_chunks - 1 - c) * S5_SEQ, r0 + (n_chunks - c) * S5_SEQ)
        hin_scr[cf, 0:S5_P] = hr[:, 0:S5_P]
        hin_scr[cb, S5_P:LANES] = hr[:, S5_P:LANES]
        hin_scr[cf, LANES:LANES + S5_P] = hi[:, 0:S5_P]
        hin_scr[cb, LANES + S5_P:2 * LANES] = hi[:, S5_P:LANES]
        dr = jnp.where(fwd, d_scr[cf, 0:LANES], d_scr[cb, 0:LANES])
        di = jnp.where(fwd, d_scr[cf, LANES:2 * LANES], d_scr[cb, LANES:2 * LANES])
        hr, hi = hr * a_re - hi * a_im + dr, hr * a_im + hi * a_re + di
    return hr, hi


def _s5_core_body(ut_ref, w_ref, a_ref, s0_ref, yt_ref, fin_ref, ts_scr, d_scr, hin_scr):
    a_re = a_ref[0:1, :]
    a_im = a_ref[1:2, :]
    rhs = ut_ref[...].reshape(S5_L * S5_C, S5_ROWS)
    ts_scr[...] = jnp.dot(w_ref[0:512, :], rhs, preferred_element_type=F32)
    d_scr[...] = ts_scr[256:512, :].T
    zero = jnp.zeros((S5_SEQ, LANES), F32)
    for hb in range(CTX_SETS):
        fr, fi = _s5_scan(d_scr, hb * CTX_CHUNKS * S5_SEQ, a_re, a_im, zero, zero, hin_scr, CTX_CHUNKS)
        fin_ref[S5_SEQ * hb:S5_SEQ * (hb + 1), 0:LANES] = fr
        fin_ref[S5_SEQ * hb:S5_SEQ * (hb + 1), LANES:2 * LANES] = fi
    _s5_scan(d_scr, S5_CTX_ROWS, a_re, a_im, s0_ref[:, 0:LANES], s0_ref[:, LANES:2 * LANES], hin_scr, LAT_CHUNKS)
    y = ts_scr[0:256, :] + lax.dot_general(w_ref[512:768, :], hin_scr[...].astype(BF16),
                                           (((1,), (1,)), ((), ())), preferred_element_type=F32)
    yt_ref[...] = y.reshape(S5_L, S5_C, S5_ROWS)


def _s5_core(ut, wall, avec, s0):
    g3 = lambda a, b: pl.BlockSpec((None, a, b), lambda g: (g, 0, 0))
    tok = pl.BlockSpec((S5_L, S5_C, S5_ROWS), lambda g: (0, g, 0))
    return pl.pallas_call(
        _s5_core_body,
        grid=(S5_G,),
        in_specs=[tok, g3(768, 256), g3(8, LANES), g3(LAT_B, 256)],
        out_specs=[tok, g3(CTX_B, 256)],
        out_shape=[jax.ShapeDtypeStruct((S5_L, D, S5_ROWS), F32),
                   jax.ShapeDtypeStruct((S5_G, CTX_B, 256), F32)],
        scratch_shapes=[pltpu.VMEM((512, S5_ROWS), F32), pltpu.VMEM((S5_ROWS, 256), F32),
                        pltpu.VMEM((S5_ROWS, 256), F32)],
        compiler_params=_params("arbitrary"),
        name="s5_core",
    )(ut, wall, avec, s0)


def _gelu_tanh(x):
    return 0.5 * x * (1.0 + jnp.tanh(math.sqrt(2.0 / math.pi) * (x + 0.044715 * (x * x * x))))


def _s5_out_body(hc_ref, hl_ref, y_ref, mod_ref, g_ref, dskip_ref, w_ref, oc_ref, ol_ref, w_scr):
    n = pl.program_id(0)
    is_lat = n >= S5_CTX_STEPS

    @pl.when(n == 0)
    def _():
        w_scr[...] = w_ref[...].astype(BF16)

    mod = mod_ref[...]
    gain = g_ref[...]
    dskip = dskip_ref[...]
    hs, gs = [], []
    for c in range(S5_CB):
        h = jnp.where(is_lat, hl_ref[:, c], hc_ref[:, c])
        u = _norm_mod(h, gain, mod[:, :, 0:D], mod[:, :, D:2 * D])
        y = u * dskip + y_ref[c]
        hs.append(h)
        gs.append(_gelu_tanh(y).astype(BF16).reshape(S5_SEQ * S5_L, D))
    hh = jnp.dot(jnp.concatenate(gs, axis=0), w_scr[...], preferred_element_type=F32)
    rows = S5_SEQ * S5_L
    for c in range(S5_CB):
        blk = hh[rows * c:rows * (c + 1)]
        mix = (blk[:, 0:D] * jax.nn.sigmoid(blk[:, D:2 * D])).reshape(S5_SEQ, S5_L, D)
        out = hs[c] + mod[:, :, 2 * D:3 * D] * mix

        @pl.when(is_lat)
        def _():
            ol_ref[:, c] = out

        @pl.when(jnp.logical_not(is_lat))
        def _():
            oc_ref[:, c] = out


def _s5_out(h, y, mod8, gain, d_skip, w_glu):
    blk, ctx_idx, lat_idx = _s5_tile_specs()
    h_ctx, h_lat = pl.pallas_call(
        _s5_out_body,
        grid=(S5_STEPS,),
        in_specs=[pl.BlockSpec(blk, ctx_idx),
                  pl.BlockSpec(blk, lambda n: (N_CTX // N_LAT, lat_idx(n), 0, 0)),
                  pl.BlockSpec((None, S5_CB, S5_SEQ, S5_L, D), lambda n: (n, 0, 0, 0, 0)),
                  pl.BlockSpec((None, S5_SEQ, 1, 6 * D), lambda n: (jnp.where(n >= S5_CTX_STEPS, 1, 0), 0, 0, 0)),
                  pl.BlockSpec((1, D), lambda n: (0, 0)),
                  pl.BlockSpec((1, D), lambda n: (0, 0)),
                  pl.BlockSpec((D, 2 * D), lambda n: (0, 0))],
        out_specs=[pl.BlockSpec(blk, ctx_idx),
                   pl.BlockSpec(blk, lambda n: (0, lat_idx(n), 0, 0))],
        out_shape=[jax.ShapeDtypeStruct((CTX_B, CTX_CHUNKS, S5_L, D), F32),
                   jax.ShapeDtypeStruct((LAT_B, LAT_CHUNKS, S5_L, D), F32)],
        scratch_shapes=[pltpu.VMEM((D, 2 * D), BF16)],
        compiler_params=_params("arbitrary"),
        name="s5_out",
    )(h.reshape(N_TOK // CTX_S, CTX_CHUNKS, S5_L, D), h.reshape(N_TOK // LAT_S, LAT_CHUNKS, S5_L, D),
      y, mod8, gain.reshape(1, D), d_skip.reshape(1, D), w_glu)
    return h_ctx.reshape(N_CTX, D), h_lat.reshape(N_LAT, D)


def _s5_mixer(h, mods, layer, gain, state, lam_re, lam_im, log_dt, b_re, b_im, c_re, c_im, d_skip, w_glu):
    mod8 = _s5_mod8(mods, layer)
    u = _s5_in(h, mod8, gain)
    ut = u.reshape(S5_ROWS, S5_L, D).transpose(1, 2, 0)
    wall, avec = _s5_weights(lam_re, lam_im, log_dt, b_re, b_im, c_re, c_im)
    a2 = jnp.stack([jnp.concatenate([avec[:, 0], avec[:, 2]], axis=-1),
                    jnp.concatenate([avec[:, 1], avec[:, 3]], axis=-1)], axis=1)
    a2 = jnp.pad(a2, ((0, 0), (0, 6), (0, 0)))
    s0 = state.transpose(3, 0, 2, 1, 4).reshape(S5_G, LAT_B, 4 * S5_P)
    yt, fin = _s5_core(ut, wall, a2, s0)
    y = yt.transpose(2, 0, 1).reshape(S5_STEPS, S5_CB, S5_SEQ, S5_L, D)
    new_state = fin.reshape(S5_G, CTX_B, 2, 2, S5_P).transpose(1, 3, 2, 0, 4)
    return _s5_out(h, y, mod8, gain, d_skip, w_glu), new_state


def kernel(x_prompt, x_sample, cache_l0_k, cache_l0_v, state_l1, cache_l2_k, cache_l2_v, cache_l3_k, cache_l3_v, c, c_ctx, norm1, norm2, w_mod, b_mod, w_up, b_up, conv_k, conv_b, w_down, l0_w_qkv, l0_q_norm, l0_k_norm, l0_sink, l0_w_o, l1_lam_re, l1_lam_im, l1_log_dt, l1_b_re, l1_b_im, l1_c_re, l1_c_im, l1_d_skip, l1_w_glu, l2_w_qkv, l2_q_norm, l2_k_norm, l2_w_o, l3_w_qkv, l3_q_norm, l3_k_norm, l3_sink, l3_w_o):
    h = jnp.concatenate([x_prompt.reshape(N_CTX, D), x_sample.reshape(N_LAT, D)], axis=0)
    cond = jnp.concatenate([c_ctx[None, :], jnp.zeros((LAT_ROW0 - 1, D), F32), c], axis=0)
    mods = _modulation(cond, w_mod, b_mod)

    attn_layers = {
        0: (l0_w_qkv, l0_q_norm, l0_k_norm, l0_sink, l0_w_o, cache_l0_k, cache_l0_v, 16, 4, 64),
        2: (l2_w_qkv, l2_q_norm, l2_k_norm, None, l2_w_o, cache_l2_k, cache_l2_v, 8, 4, 128),
        3: (l3_w_qkv, l3_q_norm, l3_k_norm, l3_sink, l3_w_o, cache_l3_k, cache_l3_v, 16, 4, 64),
    }
    new_kv = {}
    new_state = None
    for layer in range(N_LAYERS):
        if layer in attn_layers:
            w_qkv, q_norm, k_norm, sink, w_o, ck, cv, n_heads, n_kv, dh = attn_layers[layer]
            q, k, v = _qkv_proj(h, mods, layer, norm1[layer], w_qkv, q_norm, k_norm, n_heads, n_kv, dh)
            new_kv[layer] = (k[:N_CTX].reshape(CTX_B, CTX_S, n_kv, dh), v[:N_CTX].reshape(CTX_B, CTX_S, n_kv, dh))
            o_ctx, o_lat = _attention(q, k, v, ck, cv, sink, layer)
            h = _out_proj(o_ctx, o_lat, h, mods, layer, w_o)
        else:
            h, new_state = _s5_mixer(h, mods, layer, norm1[layer], state_l1, l1_lam_re, l1_lam_im, l1_log_dt,
                                     l1_b_re, l1_b_im, l1_c_re, l1_c_im, l1_d_skip, l1_w_glu)
        h = _ffn(h, mods, layer, norm2[layer], w_up, b_up, conv_k, conv_b, w_down)

    y_prompt = h[:N_CTX].reshape(CTX_B, CTX_S, D)
    y_sample = h[N_CTX:].reshape(LAT_B, LAT_S, D)
    return (y_prompt, y_sample, new_kv[0][0], new_kv[0][1], new_state,
            new_kv[2][0], new_kv[2][1], new_kv[3][0], new_kv[3][1])
```

```python
import functools
import math

import jax
import jax.numpy as jnp
import numpy as np
from jax import lax
from jax.experimental import pallas as pl
from jax.experimental.pallas import tpu as pltpu

F32 = jnp.float32
BF16 = jnp.bfloat16

D = 1024
N_LAYERS = 4
CTX_B, CTX_S = 32, 256
LAT_B, LAT_S = 8, 1024
PAST = 512
N_CTX = CTX_B * CTX_S
N_LAT = LAT_B * LAT_S
N_TOK = N_CTX + N_LAT
GRID_W = 64
WINDOW = 128
ROPE_THETA = 10000.0
EPS = 1e-6
D_FF = 2816
N_COND = 16
LAT_ROW0 = 8

S5_G = 64
S5_C = 16
S5_P = 64
S5_L = 16

VMEM_LIMIT = 56 * 1024 * 1024
LANES = 128
NEG_BIG = -1e30

TM = 512
TM_FFN = 1024
TF = 256
TQ = 256


def _params(*sem):
    return pltpu.CompilerParams(dimension_semantics=sem, vmem_limit_bytes=VMEM_LIMIT)


def _cond_row(i, tm):
    n_ctx_tiles = N_CTX // tm
    return jnp.where(i < n_ctx_tiles, 0, LAT_ROW0 + (i - n_ctx_tiles) // (LAT_S // tm))


def _norm_mod(x, gain, shift, scale):
    ms = jnp.mean(x * x, axis=-1, keepdims=True)
    return (x * lax.rsqrt(ms + EPS) * gain) * (1.0 + scale) + shift


def _silu(x):
    return x * jax.nn.sigmoid(x)


def _mod_body(cond_ref, w_ref, b_ref, o_ref):
    s = _silu(cond_ref[...]).astype(BF16)
    o_ref[...] = jnp.dot(s, w_ref[...].astype(BF16), preferred_element_type=F32) + b_ref[...]


def _modulation(cond, w_mod, b_mod):
    tn = 1536
    return pl.pallas_call(
        _mod_body,
        grid=(N_LAYERS, 6 * D // tn),
        in_specs=[
            pl.BlockSpec((N_COND, D), lambda l, n: (0, 0)),
            pl.BlockSpec((None, D, tn), lambda l, n: (l, 0, n)),
            pl.BlockSpec((None, 1, tn), lambda l, n: (l, 0, n)),
        ],
        out_specs=pl.BlockSpec((None, N_COND, tn), lambda l, n: (l, 0, n)),
        out_shape=jax.ShapeDtypeStruct((N_LAYERS, N_COND, 6 * D), F32),
        compiler_params=_params("arbitrary", "arbitrary"),
        name="modulation",
    )(cond, w_mod, b_mod.reshape(N_LAYERS, 1, 6 * D))


def _mod_spec(layer, tm):
    return pl.BlockSpec((None, None, 1, 6 * D), lambda i, *_: (layer, _cond_row(i, tm), 0, 0))


def _token_pair(h, tm):
    n_ctx_tiles = N_CTX // tm
    h_ctx, h_lat, lat_off = (h[0], h[1], 0) if isinstance(h, tuple) else (h, h, n_ctx_tiles)
    specs = [pl.BlockSpec((tm, D), lambda i, *_: (jnp.minimum(i, n_ctx_tiles - 1), 0)),
             pl.BlockSpec((tm, D), lambda i, *_: (jnp.maximum(i - n_ctx_tiles, 0) + lat_off, 0))]
    return (h_ctx, h_lat), specs


def _split_specs(tm, width):
    n_ctx_tiles = N_CTX // tm
    return [pl.BlockSpec((tm, width), lambda i, *_: (jnp.minimum(i, n_ctx_tiles - 1), 0)),
            pl.BlockSpec((tm, width), lambda i, *_: (jnp.maximum(i - n_ctx_tiles, 0), 0))]


def _qkv_body(hc_ref, hl_ref, mod_ref, g_ref, w_ref, qn_ref, kn_ref, bd_ref, cos_ref, sin_ref,
              q_ref, kc_ref, kl_ref, vc_ref, vl_ref, *, nq, nk, dh):
    is_lat = pl.program_id(0) >= N_CTX // TM
    mod = mod_ref[...]
    h = jnp.where(is_lat, hl_ref[...], hc_ref[...])
    xn = _norm_mod(h, g_ref[...], mod[:, 0:D], mod[:, D:2 * D]).astype(BF16)
    qkv = jnp.dot(xn, w_ref[...].astype(BF16), preferred_element_type=F32)
    rows = qkv.shape[0]
    cos = cos_ref[...]
    sin = sin_ref[...]
    bd = bd_ref[...]
    quarter = dh // 4
    lane = lax.broadcasted_iota(jnp.int32, (rows, LANES), 1)
    first = (lane % (2 * quarter)) < quarter
    inv_dh = 1.0 / dh

    def head_norm_rope(z, gain):
        ss = jnp.dot((z * z).astype(BF16), bd, preferred_element_type=F32)
        zn = z * lax.rsqrt(ss * inv_dh + EPS)
        out = []
        for j in range(2):
            zz = zn[:, LANES * j:LANES * (j + 1)] * gain
            partner = jnp.where(first, pltpu.roll(zz, LANES - quarter, 1), pltpu.roll(zz, quarter, 1))
            out.append(zz * cos + partner * sin)
        return out

    qgain = qn_ref[...] * (dh ** -0.5)
    kgain = kn_ref[...]
    for c in range(nq // 256):
        lo, hi = head_norm_rope(qkv[:, 256 * c:256 * (c + 1)], qgain)
        q_ref[:, 256 * c:256 * c + LANES] = lo.astype(BF16)
        q_ref[:, 256 * c + LANES:256 * (c + 1)] = hi.astype(BF16)
    k_cols = []
    for c in range(nk // 256):
        k_cols += head_norm_rope(qkv[:, nq + 256 * c:nq + 256 * (c + 1)], kgain)
    v = qkv[:, nq + nk:]

    def store(k_ref, v_ref):
        for j, kj in enumerate(k_cols):
            k_ref[:, LANES * j:LANES * (j + 1)] = kj
        v_ref[...] = v

    pl.when(is_lat)(functools.partial(store, kl_ref, vl_ref))
    pl.when(jnp.logical_not(is_lat))(functools.partial(store, kc_ref, vc_ref))


def _rope_tables(dh):
    half, quarter = dh // 2, dh // 4
    freqs = 1.0 / (ROPE_THETA ** (np.arange(quarter, dtype=np.float32) / quarter))
    pos = np.arange(LAT_S)
    row = (pos // GRID_W).astype(np.float32)
    col = (pos % GRID_W).astype(np.float32)
    ang_r = (row[:, None] * freqs[None, :]).astype(np.float32)
    ang_c = (col[:, None] * freqs[None, :]).astype(np.float32)
    cos = np.concatenate([np.cos(ang_r), np.cos(ang_r), np.cos(ang_c), np.cos(ang_c)], axis=1)
    sin = np.concatenate([-np.sin(ang_r), np.sin(ang_r), -np.sin(ang_c), np.sin(ang_c)], axis=1)
    reps = LANES // dh
    cos = np.tile(cos.astype(np.float32), (1, reps))
    sin = np.tile(sin.astype(np.float32), (1, reps))
    cos = np.concatenate([np.ones_like(cos), cos], axis=0)
    sin = np.concatenate([np.zeros_like(sin), sin], axis=0)
    return jnp.asarray(cos), jnp.asarray(sin)


def _block_diag_ones(dh):
    idx = np.arange(256) // dh
    return jnp.asarray((idx[:, None] == idx[None, :]).astype(np.float32), dtype=BF16)


def _qkv_proj(h, mods, layer, gain, w_qkv, q_norm, k_norm, n_heads, n_kv, dh):
    nq, nk = n_heads * dh, n_kv * dh
    nqkv = nq + 2 * nk
    cos, sin = _rope_tables(dh)
    reps = LANES // dh
    n_ctx_tiles = N_CTX // TM
    lat_tiles = LAT_S // TM

    def rope_idx(i):
        return (jnp.where(i < n_ctx_tiles, 0, lat_tiles + (i - n_ctx_tiles) % lat_tiles), 0)

    h_ops, h_specs = _token_pair(h, TM)
    kv_specs = _split_specs(TM, nk)
    kv_shapes = [jax.ShapeDtypeStruct((N_CTX, nk), F32), jax.ShapeDtypeStruct((N_LAT, nk), F32)]
    return pl.pallas_call(
        functools.partial(_qkv_body, nq=nq, nk=nk, dh=dh),
        grid=(N_TOK // TM,),
        in_specs=h_specs + [
            _mod_spec(layer, TM),
            pl.BlockSpec((1, D), lambda i: (0, 0)),
            pl.BlockSpec((D, nqkv), lambda i: (0, 0)),
            pl.BlockSpec((1, LANES), lambda i: (0, 0)),
            pl.BlockSpec((1, LANES), lambda i: (0, 0)),
            pl.BlockSpec((256, 256), lambda i: (0, 0)),
            pl.BlockSpec((TM, LANES), rope_idx),
            pl.BlockSpec((TM, LANES), rope_idx),
        ],
        out_specs=[pl.BlockSpec((TM, nq), lambda i: (i, 0))] + kv_specs + kv_specs,
        out_shape=[jax.ShapeDtypeStruct((N_TOK, nq), BF16)] + kv_shapes + kv_shapes,
        compiler_params=_params("arbitrary"),
        name=f"qkv_l{layer}",
    )(*h_ops, mods.reshape(N_LAYERS, N_COND, 1, 6 * D), gain.reshape(1, D), w_qkv,
      jnp.tile(q_norm, reps).reshape(1, LANES), jnp.tile(k_norm, reps).reshape(1, LANES),
      _block_diag_ones(dh), cos, sin)


def _attend(q, segs, sink):
    scores = []
    m = None
    for k, _, mask in segs:
        s = lax.dot_general(q, k, (((1,), (1,)), ((), ())), preferred_element_type=F32)
        if mask is not None:
            s = jnp.where(mask, s, NEG_BIG)
        scores.append(s)
        sm = jnp.max(s, axis=-1, keepdims=True)
        m = sm if m is None else jnp.maximum(m, sm)
    if sink is not None:
        m = jnp.maximum(m, sink)
    den = None
    acc = None
    for s, (_, v, _) in zip(scores, segs):
        p = jnp.exp(s - m)
        ps = jnp.sum(p, axis=-1, keepdims=True)
        pv = jnp.dot(p.astype(BF16), v, preferred_element_type=F32)
        den = ps if den is None else den + ps
        acc = pv if acc is None else acc + pv
    if sink is not None:
        den = den + jnp.exp(sink - m)
    return acc / den


def _dup_halves(x, kv):
    lane = lax.broadcasted_iota(jnp.int32, x.shape, 1)
    r = pltpu.roll(x, 64, 1)
    lo = lane < 64
    return (jnp.where(lo, x, r) if kv % 2 == 0 else jnp.where(lo, r, x)).astype(BF16)


def _attend_heads64(q_ref, o_ref, sink_ref, kv_segs, masks, tq):
    lane = lax.broadcasted_iota(jnp.int32, (tq, LANES), 1)
    lo = lane < 64
    rowsel = lax.broadcasted_iota(jnp.int32, (4 * tq, 1), 0)
    for kv in range(4):
        segs = []
        for (k, v), mask in zip(kv_segs, masks):
            c = LANES * (kv // 2)
            segs.append((_dup_halves(k[:, c:c + LANES], kv), _dup_halves(v[:, c:c + LANES], kv),
                         None if mask is None else jnp.concatenate([mask] * 4, axis=0)))
        parts = []
        sinks = None
        for pair in range(2):
            j = 2 * kv + pair
            qp = q_ref[:, LANES * j:LANES * (j + 1)]
            zero = jnp.zeros_like(qp)
            parts += [jnp.where(lo, qp, zero), jnp.where(lo, zero, qp)]
        for r in range(4):
            sv = sink_ref[4 * kv + r]
            sinks = jnp.full((4 * tq, 1), sv, F32) if sinks is None else jnp.where(rowsel >= r * tq, sv, sinks)
        out = _attend(jnp.concatenate(parts, axis=0), segs, sinks)
        for pair in range(2):
            j = 2 * kv + pair
            a = out[(2 * pair) * tq:(2 * pair + 1) * tq]
            b = out[(2 * pair + 1) * tq:(2 * pair + 2) * tq]
            o_ref[:, LANES * j:LANES * (j + 1)] = jnp.where(lo, a, b).astype(BF16)


def _attend_heads128(q_ref, o_ref, kv_segs, tq):
    for kv in range(4):
        c = LANES * kv
        segs = [(k[:, c:c + LANES].astype(BF16), v[:, c:c + LANES].astype(BF16), None) for k, v in kv_segs]
        q = jnp.concatenate([q_ref[:, LANES * (2 * kv):LANES * (2 * kv + 1)],
                             q_ref[:, LANES * (2 * kv + 1):LANES * (2 * kv + 2)]], axis=0)
        out = _attend(q, segs, None)
        o_ref[:, LANES * (2 * kv):LANES * (2 * kv + 1)] = out[:tq].astype(BF16)
        o_ref[:, LANES * (2 * kv + 1):LANES * (2 * kv + 2)] = out[tq:].astype(BF16)


def _ctx_attn_a_body(sink_ref, q_ref, k_ref, v_ref, o_ref):
    _attend_heads64(q_ref, o_ref, sink_ref, [(k_ref[...], v_ref[...])], [None], CTX_S)


def _ctx_attn_c_body(q_ref, k_ref, v_ref, o_ref):
    _attend_heads128(q_ref, o_ref, [(k_ref[...], v_ref[...])], CTX_S)


def _lat_attn_a_body(sink_ref, q_ref, k_ref, v_ref, ck_ref, cv_ref, o_ref):
    qi = pl.program_id(1)
    band = TQ + 2 * WINDOW
    ws = pl.multiple_of(jnp.clip(qi * TQ - WINDOW, 0, LAT_S - band), WINDOW)
    qpos = qi * TQ + lax.broadcasted_iota(jnp.int32, (TQ, band), 0)
    kpos = ws + lax.broadcasted_iota(jnp.int32, (TQ, band), 1)
    mask = jnp.abs(qpos - kpos) <= WINDOW
    segs = [(ck_ref[...], cv_ref[...]), (k_ref[pl.ds(ws, band), :], v_ref[pl.ds(ws, band), :])]
    _attend_heads64(q_ref, o_ref, sink_ref, segs, [None, mask], TQ)


def _lat_attn_c_body(q_ref, k_ref, v_ref, ck_ref, cv_ref, o_ref):
    _attend_heads128(q_ref, o_ref, [(ck_ref[...], cv_ref[...]), (k_ref[...], v_ref[...])], TQ)


_SMEM_SPEC = pl.BlockSpec(memory_space=pltpu.SMEM)


def _attention(q, k_ctx, k_lat, v_ctx, v_lat, cache_k, cache_v, sink, layer):
    nk = k_ctx.shape[1]
    ck = cache_k.reshape(LAT_B, PAST, nk)
    cv = cache_v.reshape(LAT_B, PAST, nk)
    ctx_specs = [
        pl.BlockSpec((CTX_S, D), lambda b: (b, 0)),
        pl.BlockSpec((CTX_S, nk), lambda b: (b, 0)),
        pl.BlockSpec((CTX_S, nk), lambda b: (b, 0)),
    ]
    qb = LAT_S // TQ
    lat_specs = [
        pl.BlockSpec((TQ, D), lambda b, i: (N_CTX // TQ + b * qb + i, 0)),
        pl.BlockSpec((LAT_S, nk), lambda b, i: (b, 0)),
        pl.BlockSpec((LAT_S, nk), lambda b, i: (b, 0)),
        pl.BlockSpec((None, PAST, nk), lambda b, i: (b, 0, 0)),
        pl.BlockSpec((None, PAST, nk), lambda b, i: (b, 0, 0)),
    ]
    ctx_out = dict(out_specs=pl.BlockSpec((CTX_S, D), lambda b: (b, 0)),
                   out_shape=jax.ShapeDtypeStruct((N_CTX, D), BF16),
                   grid=(CTX_B,), compiler_params=_params("arbitrary"))
    lat_out = dict(out_specs=pl.BlockSpec((TQ, D), lambda b, i: (b * qb + i, 0)),
                   out_shape=jax.ShapeDtypeStruct((N_LAT, D), BF16),
                   grid=(LAT_B, qb), compiler_params=_params("arbitrary", "arbitrary"))
    if sink is not None:
        o_ctx = pl.pallas_call(_ctx_attn_a_body, in_specs=[_SMEM_SPEC] + ctx_specs,
                               name=f"attn_ctx_l{layer}", **ctx_out)(sink, q, k_ctx, v_ctx)
        o_lat = pl.pallas_call(_lat_attn_a_body, in_specs=[_SMEM_SPEC] + lat_specs,
                               name=f"attn_lat_l{layer}", **lat_out)(sink, q, k_lat, v_lat, ck, cv)
    else:
        o_ctx = pl.pallas_call(_ctx_attn_c_body, in_specs=ctx_specs,
                               name=f"attn_ctx_l{layer}", **ctx_out)(q, k_ctx, v_ctx)
        o_lat = pl.pallas_call(_lat_attn_c_body, in_specs=lat_specs,
                               name=f"attn_lat_l{layer}", **lat_out)(q, k_lat, v_lat, ck, cv)
    return o_ctx, o_lat


def _oproj_body(oc_ref, ol_ref, hc_ref, hl_ref, mod_ref, w_ref, out_ref):
    is_ctx = pl.program_id(0) < N_CTX // TM
    o = jnp.where(is_ctx, oc_ref[...], ol_ref[...])
    h = jnp.where(is_ctx, hc_ref[...], hl_ref[...])
    mix = jnp.dot(o, w_ref[...].astype(BF16), preferred_element_type=F32)
    out_ref[...] = h + mod_ref[:, 2 * D:3 * D] * mix


def _out_proj(o_ctx, o_lat, h, mods, layer, w_o):
    (o_ops, o_specs), (h_ops, h_specs) = _token_pair((o_ctx, o_lat), TM), _token_pair(h, TM)
    return pl.pallas_call(
        _oproj_body,
        grid=(N_TOK // TM,),
        in_specs=o_specs + h_specs + [_mod_spec(layer, TM), pl.BlockSpec((D, D), lambda i: (0, 0))],
        out_specs=pl.BlockSpec((TM, D), lambda i: (i, 0)),
        out_shape=jax.ShapeDtypeStruct((N_TOK, D), F32),
        compiler_params=_params("arbitrary"),
        name=f"oproj_l{layer}",
    )(*o_ops, *h_ops, mods.reshape(N_LAYERS, N_COND, 1, 6 * D), w_o)


FFN_CHUNK = CTX_S
FFN_SLOT = FFN_CHUNK + 16
FFN_NCHUNK = TM_FFN // FFN_CHUNK


FFN_NF = D_FF // TF
FFN_NT = N_TOK // TM_FFN
FFN_WORK = FFN_NT * FFN_NF
FFN_PIECES = ((0, 96), (96, 176), (176, 256))


def _ffn_item(n, lag):
    m = jnp.clip(n - lag, 0, FFN_WORK - 1)
    return m // FFN_NF, m % FFN_NF


def _ffn_body(*refs, n_h, n_out):
    h_refs, refs = refs[:n_h], refs[n_h:]
    (modu_ref, modd_ref, g_ref, wg_ref, wv_ref, bg_ref, bv_ref, kg_ref, kv_ref, cg_ref, cv_ref, wd_ref) = refs[:12]
    out_refs, refs = refs[12:12 + n_out], refs[12 + n_out:]
    (xn_scr, acc_scr, hres_scr, ug0_scr, uv0_scr, ug1_scr, uv1_scr,
     a0_scr, a1_scr, wg0_scr, wv0_scr, wd0_scr, wg1_scr, wv1_scr, wd1_scr) = refs
    n = pl.program_id(0)
    tu, fu = _ffn_item(n, 1)
    tg, _ = _ffn_item(n, 2)
    td, fd = _ffn_item(n, 3)
    n_ctx_tiles = N_CTX // TM_FFN
    u_slots = ((ug0_scr, uv0_scr), (ug1_scr, uv1_scr))
    a_slots = (a0_scr, a1_scr)
    w_slots = ((wg0_scr, wv0_scr, wd0_scr), (wg1_scr, wv1_scr, wd1_scr))

    def load_h():
        if n_h == 1:
            return h_refs[0][...]
        return jnp.where(tu >= n_ctx_tiles, h_refs[1][...], h_refs[0][...])

    @pl.when(fu == 0)
    def _():
        mod = modu_ref[...]
        h = load_h()
        xn_scr[...] = _norm_mod(h, g_ref[...], mod[:, 3 * D:4 * D], mod[:, 4 * D:5 * D]).astype(BF16)

    @pl.when(fd == 0)
    def _():
        acc_scr[...] = jnp.zeros_like(acc_scr)

    @pl.when(n == 0)
    def _():
        for scr in u_slots[1] + (a_slots[0],) + w_slots[1]:
            scr[...] = jnp.zeros_like(scr)

    base = [8 + FFN_SLOT * k for k in range(FFN_NCHUNK)]
    is_lat = tg >= n_ctx_tiles

    def step(parity):
        ug_up, uv_up = u_slots[parity]
        ug_dn, uv_dn = u_slots[1 - parity]
        a_gt, a_dn = a_slots[1 - parity], a_slots[parity]
        wg_scr, wv_scr, wd_scr = w_slots[1 - parity]
        for scr, b_ref in ((ug_dn, bg_ref), (uv_dn, bv_ref)):
            pad = -b_ref[...]
            tops = [pad] + [jnp.where(is_lat, scr[base[k - 1] + FFN_CHUNK - 1:base[k - 1] + FFN_CHUNK, :], pad)
                            for k in range(1, FFN_NCHUNK)]
            bots = [jnp.where(is_lat, scr[base[k + 1]:base[k + 1] + 1, :], pad)
                    for k in range(FFN_NCHUNK - 1)] + [pad]
            for k in range(FFN_NCHUNK):
                scr[base[k] - 1:base[k], :] = tops[k]
                scr[base[k] + FFN_CHUNK:base[k] + FFN_CHUNK + 1, :] = bots[k]

        def conv(scr, b_ref, k_ref, c_ref, lo, rows):
            kk = k_ref[...]
            const = c_ref[...] + b_ref[...] * (kk[0:1] + kk[1:2] + kk[2:3])
            return (const + kk[0:1] * scr[lo - 1:lo - 1 + rows, :] + kk[1:2] * scr[lo:lo + rows, :]
                    + kk[2:3] * scr[lo + 1:lo + 1 + rows, :])

        for k in range(FFN_NCHUNK):
            rows = slice(FFN_CHUNK * k, FFN_CHUNK * (k + 1))
            xk = xn_scr[rows, :]
            ug_up[base[k]:base[k] + FFN_CHUNK, :] = jnp.dot(xk, wg_scr[...], preferred_element_type=F32)
            uv_up[base[k]:base[k] + FFN_CHUNK, :] = jnp.dot(xk, wv_scr[...], preferred_element_type=F32)
            for r0, r1 in FFN_PIECES:
                lo = base[k] + r0
                gate = conv(ug_dn, bg_ref, kg_ref, cg_ref, lo, r1 - r0)
                val = conv(uv_dn, bv_ref, kv_ref, cv_ref, lo, r1 - r0)
                a_gt[FFN_CHUNK * k + r0:FFN_CHUNK * k + r1, :] = (_silu(gate) * val).astype(BF16)
            acc_scr[rows, :] += jnp.dot(a_dn[rows, :], wd_scr[...], preferred_element_type=F32)
            for scr, ref in zip(w_slots[parity], (wg_ref, wv_ref, wd_ref)):
                share = scr.shape[0] // FFN_NCHUNK
                scr[share * k:share * (k + 1), :] = ref[share * k:share * (k + 1), :].astype(BF16)

    @pl.when(n % 2 == 0)
    def _():
        step(0)

    @pl.when(n % 2 == 1)
    def _():
        step(1)

    done = jnp.logical_and(fd == FFN_NF - 1, n >= 3)

    def write(out_ref):
        out_ref[...] = hres_scr[...] + modd_ref[:, 5 * D:6 * D] * acc_scr[...]

    if len(out_refs) == 1:
        pl.when(done)(functools.partial(write, out_refs[0]))
    else:
        pl.when(jnp.logical_and(done, td < n_ctx_tiles))(functools.partial(write, out_refs[0]))
        pl.when(jnp.logical_and(done, td >= n_ctx_tiles))(functools.partial(write, out_refs[1]))

    @pl.when(jnp.logical_and(fu == 1, n <= FFN_WORK))
    def _():
        hres_scr[...] = load_h()


def _ffn(h, mods, layer, gain, w_up, b_up, conv_k, conv_b, w_down, split_out=False):
    b3 = b_up.reshape(N_LAYERS, 1, 2 * D_FF)
    c3 = conv_b.reshape(N_LAYERS, 1, 2 * D_FF)
    n_ctx_tiles = N_CTX // TM_FFN
    up_tile = lambda n: _ffn_item(n, 1)[0]
    if isinstance(h, tuple):
        h_ops = h
        h_specs = [pl.BlockSpec((TM_FFN, D), lambda n: (jnp.minimum(up_tile(n), n_ctx_tiles - 1), 0)),
                   pl.BlockSpec((TM_FFN, D), lambda n: (jnp.maximum(up_tile(n) - n_ctx_tiles, 0), 0))]
    else:
        h_ops = (h,)
        h_specs = [pl.BlockSpec((TM_FFN, D), lambda n: (up_tile(n), 0))]
    col = lambda lag, off: (lambda n: (layer, 0, off + _ffn_item(n, lag)[1]))
    mod_of = lambda lag: pl.BlockSpec((None, None, 1, 6 * D),
                                      lambda n: (layer, _cond_row(_ffn_item(n, lag)[0], TM_FFN), 0, 0))
    slot_rows = 8 + FFN_SLOT * FFN_NCHUNK
    mods4 = mods.reshape(N_LAYERS, N_COND, 1, 6 * D)
    done_tile = lambda n: _ffn_item(n, 3)[0]
    if split_out:
        out_specs = [pl.BlockSpec((TM_FFN, D), lambda n: (jnp.minimum(done_tile(n), n_ctx_tiles - 1), 0)),
                     pl.BlockSpec((TM_FFN, D), lambda n: (jnp.maximum(done_tile(n) - n_ctx_tiles, 0), 0))]
        out_shape = [jax.ShapeDtypeStruct((N_CTX, D), F32), jax.ShapeDtypeStruct((N_LAT, D), F32)]
    else:
        out_specs = pl.BlockSpec((TM_FFN, D), lambda n: (done_tile(n), 0))
        out_shape = jax.ShapeDtypeStruct((N_TOK, D), F32)
    result = pl.pallas_call(
        functools.partial(_ffn_body, n_h=len(h_ops), n_out=2 if split_out else 1),
        grid=(FFN_WORK + 3,),
        in_specs=h_specs + [
            mod_of(1),
            mod_of(3),
            pl.BlockSpec((1, D), lambda n: (0, 0)),
            pl.BlockSpec((None, D, TF), col(0, 0)),
            pl.BlockSpec((None, D, TF), col(0, FFN_NF)),
            pl.BlockSpec((None, 1, TF), col(2, 0)),
            pl.BlockSpec((None, 1, TF), col(2, FFN_NF)),
            pl.BlockSpec((None, 3, TF), col(2, 0)),
            pl.BlockSpec((None, 3, TF), col(2, FFN_NF)),
            pl.BlockSpec((None, 1, TF), col(2, 0)),
            pl.BlockSpec((None, 1, TF), col(2, FFN_NF)),
            pl.BlockSpec((None, TF, D), lambda n: (layer, _ffn_item(n, 2)[1], 0)),
        ],
        out_specs=out_specs,
        out_shape=out_shape,
        scratch_shapes=[pltpu.VMEM((TM_FFN, D), BF16), pltpu.VMEM((TM_FFN, D), F32), pltpu.VMEM((TM_FFN, D), F32),
                        pltpu.VMEM((slot_rows, TF), F32), pltpu.VMEM((slot_rows, TF), F32),
                        pltpu.VMEM((slot_rows, TF), F32), pltpu.VMEM((slot_rows, TF), F32),
                        pltpu.VMEM((TM_FFN, TF), BF16), pltpu.VMEM((TM_FFN, TF), BF16),
                        pltpu.VMEM((D, TF), BF16), pltpu.VMEM((D, TF), BF16), pltpu.VMEM((TF, D), BF16),
                        pltpu.VMEM((D, TF), BF16), pltpu.VMEM((D, TF), BF16), pltpu.VMEM((TF, D), BF16)],
        compiler_params=_params("arbitrary"),
        name=f"ffn_l{layer}",
    )(*h_ops, mods4, mods4, gain.reshape(1, D), w_up, w_up, b3, b3, conv_k, conv_k, c3, c3, w_down)
    return tuple(result) if split_out else result


S5_SEQ = 8
S5_CB = 4
CTX_SETS = CTX_B // S5_SEQ
CTX_CHUNKS = CTX_S // S5_L
LAT_CHUNKS = LAT_S // S5_L
S5_CTX_STEPS = CTX_SETS * CTX_CHUNKS // S5_CB
S5_LAT_STEPS = LAT_CHUNKS // S5_CB
S5_STEPS = S5_CTX_STEPS + S5_LAT_STEPS
S5_ROWS = (N_CTX + N_LAT) // S5_L
S5_CTX_ROWS = N_CTX // S5_L


def _s5_tile_specs():
    per = CTX_CHUNKS // S5_CB
    blk = (S5_SEQ, S5_CB, S5_L, D)

    def ctx_idx(n):
        m = jnp.minimum(n, S5_CTX_STEPS - 1)
        return (m // per, m % per, 0, 0)

    return blk, ctx_idx, (lambda n: (jnp.maximum(n - S5_CTX_STEPS, 0)))


def _s5_mod8(mods, layer):
    m = mods[layer]
    return jnp.stack([jnp.broadcast_to(m[0:1], (S5_SEQ, 6 * D)), m[LAT_ROW0:LAT_ROW0 + LAT_B]]).reshape(
        2, S5_SEQ, 1, 6 * D)


def _s5_in_body(hc_ref, hl_ref, mod_ref, g_ref, u_ref):
    is_lat = pl.program_id(0) >= S5_CTX_STEPS
    mod = mod_ref[...]
    gain = g_ref[...]
    for c in range(S5_CB):
        x = jnp.where(is_lat, hl_ref[:, c], hc_ref[:, c])
        u_ref[c] = _norm_mod(x, gain, mod[:, :, 0:D], mod[:, :, D:2 * D]).astype(BF16)


def _s5_in(h, mod8, gain):
    blk, ctx_idx, lat_idx = _s5_tile_specs()
    return pl.pallas_call(
        _s5_in_body,
        grid=(S5_STEPS,),
        in_specs=[pl.BlockSpec(blk, ctx_idx),
                  pl.BlockSpec(blk, lambda n: (N_CTX // N_LAT, lat_idx(n), 0, 0)),
                  pl.BlockSpec((None, S5_SEQ, 1, 6 * D), lambda n: (jnp.where(n >= S5_CTX_STEPS, 1, 0), 0, 0, 0)),
                  pl.BlockSpec((1, D), lambda n: (0, 0))],
        out_specs=pl.BlockSpec((None, S5_CB, S5_SEQ, S5_L, D), lambda n: (n, 0, 0, 0, 0)),
        out_shape=jax.ShapeDtypeStruct((S5_STEPS, S5_CB, S5_SEQ, S5_L, D), BF16),
        compiler_params=_params("arbitrary"),
        name="s5_in",
    )(h.reshape(N_TOK // CTX_S, CTX_CHUNKS, S5_L, D), h.reshape(N_TOK // LAT_S, LAT_CHUNKS, S5_L, D),
      mod8, gain.reshape(1, D))


def _s5_weights_body(lamc_re_ref, lamc_im_ref, lamr_re_ref, lamr_im_ref, ldt_ref,
                     bt_re_ref, bt_im_ref, btr_re_ref, btr_im_ref, ct_re_ref, ct_im_ref,
                     w_ref, a_ref):
    blk = (lax.broadcasted_iota(jnp.int32, (1, 256), 1) // S5_L).astype(F32)
    lane256 = lax.broadcasted_iota(jnp.int32, (S5_C, 256), 1)
    hi = lax.Precision.HIGHEST
    krow = []
    st_rows = {}
    w_rows = {}
    for d in range(2):
        dt = jnp.exp(ldt_ref[d])
        lr = lamc_re_ref[d]
        li = lamc_im_ref[d]

        def powers(expo):
            mag = jnp.exp((lr * dt) * expo)
            ang = (li * dt) * expo
            return mag * jnp.cos(ang), mag * jnp.sin(ang)

        asc_r, asc_i = powers(blk)
        dsc_r, dsc_i = powers(15.0 - blk)
        ar = asc_r[:, S5_L:S5_L + 1]
        ai = asc_i[:, S5_L:S5_L + 1]
        den = lr * lr + li * li
        n_re = ar - 1.0
        f_re = (n_re * lr + ai * li) / den
        f_im = (ai * lr - n_re * li) / den
        bbr = f_re * bt_re_ref[d] - f_im * bt_im_ref[d]
        bbi = f_re * bt_im_ref[d] + f_im * bt_re_ref[d]
        e0r, e0i = (asc_r, asc_i) if d == 0 else (dsc_r, dsc_i)
        e1r = e0r * ar - e0i * ai
        e1i = e0r * ai + e0i * ar
        pr, pi = (dsc_r, dsc_i) if d == 0 else (asc_r, asc_i)
        st_rows[("re", d)] = pr * bbr - pi * bbi
        st_rows[("im", d)] = pr * bbi + pi * bbr
        ctr = ct_re_ref[d]
        cti = ct_im_ref[d]
        k_re = ctr * e0r - cti * e0i
        k_imneg = -(ctr * e0i + cti * e0r)
        w_rows[("re", d)] = ctr * e1r - cti * e1i
        w_rows[("im", d)] = -(ctr * e1i + cti * e1r)
        lrr = lamr_re_ref[d:d + 1, :]
        lir = lamr_im_ref[d:d + 1, :]
        magr = jnp.exp(lrr * dt)
        arr = magr * jnp.cos(lir * dt)
        air = magr * jnp.sin(lir * dt)
        denr = lrr * lrr + lir * lir
        nr = arr - 1.0
        fr = (nr * lrr + air * lir) / denr
        fi = (air * lrr - nr * lir) / denr
        bbr_row = fr * btr_re_ref[d] - fi * btr_im_ref[d]
        bbi_row = fr * btr_im_ref[d] + fi * btr_re_ref[d]
        krow.append(jnp.dot(bbr_row, k_re, precision=hi, preferred_element_type=F32)
                    + jnp.dot(bbi_row, k_imneg, precision=hi, preferred_element_type=F32))
        mag16 = jnp.exp(lrr * dt * 16.0)
        a_ref[2 * d:2 * d + 1, :] = mag16 * jnp.cos(lir * dt * 16.0)
        a_ref[2 * d + 1:2 * d + 2, :] = mag16 * jnp.sin(lir * dt * 16.0)
    a_ref[4:8, :] = jnp.zeros((4, S5_P), F32)

    t_rows = []
    for s in range(S5_L):
        fwd = krow[0] if s == 0 else jnp.where(lane256 >= S5_C * s, pltpu.roll(krow[0], S5_C * s, 1), 0.0)
        bwd = krow[1] if s == S5_L - 1 else jnp.where(lane256 < S5_C * (s + 1),
                                                      pltpu.roll(krow[1], S5_C * (s + 1), 1), 0.0)
        t_rows.append(fwd + bwd)
    order = [("re", 0), ("re", 1), ("im", 0), ("im", 1)]
    w_ref[0:256, :] = jnp.concatenate(t_rows, axis=0).T.astype(BF16)
    w_ref[256:512, :] = jnp.concatenate([st_rows[o] for o in order], axis=0).astype(BF16)
    w_ref[512:768, :] = jnp.concatenate([w_rows[o] for o in order], axis=0).T.astype(BF16)


def _s5_weights(lam_re, lam_im, log_dt, b_re, b_im, c_re, c_im):
    col = lambda x: x.transpose(1, 0, 2).reshape(S5_G, 2, S5_P, 1)
    row = lambda x: x.transpose(1, 0, 2)
    ldt = log_dt.transpose(1, 0).reshape(S5_G, 2, 1, 1)
    b_tiled = lambda x: jnp.tile(x.transpose(1, 0, 2, 3), (1, 1, 1, S5_L))
    b_rowf = lambda x: x.transpose(1, 0, 3, 2)
    c_tiled = lambda x: jnp.tile(x.transpose(1, 0, 3, 2), (1, 1, 1, S5_L))
    g4 = lambda *tail: pl.BlockSpec((None,) + tail, lambda g: (g,) + (0,) * len(tail))
    return pl.pallas_call(
        _s5_weights_body,
        grid=(S5_G,),
        in_specs=[g4(2, S5_P, 1), g4(2, S5_P, 1), g4(2, S5_P), g4(2, S5_P), g4(2, 1, 1),
                  g4(2, S5_P, 256), g4(2, S5_P, 256), g4(2, S5_C, S5_P), g4(2, S5_C, S5_P),
                  g4(2, S5_P, 256), g4(2, S5_P, 256)],
        out_specs=[g4(768, 256), g4(8, S5_P)],
        out_shape=[jax.ShapeDtypeStruct((S5_G, 768, 256), BF16),
                   jax.ShapeDtypeStruct((S5_G, 8, S5_P), F32)],
        compiler_params=_params("arbitrary"),
        name="s5_weights",
    )(col(lam_re), col(lam_im), row(lam_re), row(lam_im), ldt,
      b_tiled(b_re), b_tiled(b_im), b_rowf(b_re), b_rowf(b_im), c_tiled(c_re), c_tiled(c_im))


def _s5_scan(d_scr, r0, a_re, a_im, init_re, init_im, hin_scr, n_chunks):
    lane = lax.broadcasted_iota(jnp.int32, (S5_SEQ, LANES), 1)
    fwd = lane < S5_P
    hr, hi = init_re, init_im
    for c in range(n_chunks):
        cf = slice(r0 + c * S5_SEQ, r0 + (c + 1) * S5_SEQ)
        cb = slice(r0 + (n_chunks - 1 - c) * S5_SEQ, r0 + (n_chunks - c) * S5_SEQ)
        hin_scr[cf, 0:S5_P] = hr[:, 0:S5_P]
        hin_scr[cb, S5_P:LANES] = hr[:, S5_P:LANES]
        hin_scr[cf, LANES:LANES + S5_P] = hi[:, 0:S5_P]
        hin_scr[cb, LANES + S5_P:2 * LANES] = hi[:, S5_P:LANES]
        dr = jnp.where(fwd, d_scr[cf, 0:LANES], d_scr[cb, 0:LANES])
        di = jnp.where(fwd, d_scr[cf, LANES:2 * LANES], d_scr[cb, LANES:2 * LANES])
        hr, hi = hr * a_re - hi * a_im + dr, hr * a_im + hi * a_re + di
    return hr, hi


def _s5_core_body(ut_ref, w_ref, a_ref, s0_ref, yt_ref, fin_ref, ts_scr, d_scr, hin_scr):
    a_re = a_ref[0:1, :]
    a_im = a_ref[1:2, :]
    rhs = ut_ref[...].reshape(S5_L * S5_C, S5_ROWS)
    ts_scr[...] = jnp.dot(w_ref[0:512, :], rhs, preferred_element_type=F32)
    d_scr[...] = ts_scr[256:512, :].T
    zero = jnp.zeros((S5_SEQ, LANES), F32)
    for hb in range(CTX_SETS):
        fr, fi = _s5_scan(d_scr, hb * CTX_CHUNKS * S5_SEQ, a_re, a_im, zero, zero, hin_scr, CTX_CHUNKS)
        fin_ref[S5_SEQ * hb:S5_SEQ * (hb + 1), 0:LANES] = fr
        fin_ref[S5_SEQ * hb:S5_SEQ * (hb + 1), LANES:2 * LANES] = fi
    _s5_scan(d_scr, S5_CTX_ROWS, a_re, a_im, s0_ref[:, 0:LANES], s0_ref[:, LANES:2 * LANES], hin_scr, LAT_CHUNKS)
    y = ts_scr[0:256, :] + lax.dot_general(w_ref[512:768, :], hin_scr[...].astype(BF16),
                                           (((1,), (1,)), ((), ())), preferred_element_type=F32)
    yt_ref[...] = y.reshape(S5_L, S5_C, S5_ROWS)


def _s5_core(ut, wall, avec, s0):
    g3 = lambda a, b: pl.BlockSpec((None, a, b), lambda g: (g, 0, 0))
    tok = pl.BlockSpec((S5_L, S5_C, S5_ROWS), lambda g: (0, g, 0))
    return pl.pallas_call(
        _s5_core_body,
        grid=(S5_G,),
        in_specs=[tok, g3(768, 256), g3(8, LANES), g3(LAT_B, 256)],
        out_specs=[tok, g3(CTX_B, 256)],
        out_shape=[jax.ShapeDtypeStruct((S5_L, D, S5_ROWS), F32),
                   jax.ShapeDtypeStruct((S5_G, CTX_B, 256), F32)],
        scratch_shapes=[pltpu.VMEM((512, S5_ROWS), F32), pltpu.VMEM((S5_ROWS, 256), F32),
                        pltpu.VMEM((S5_ROWS, 256), F32)],
        compiler_params=_params("arbitrary"),
        name="s5_core",
    )(ut, wall, avec, s0)


def _gelu_tanh(x):
    return 0.5 * x * (1.0 + jnp.tanh(math.sqrt(2.0 / math.pi) * (x + 0.044715 * (x * x * x))))


def _s5_out_body(hc_ref, hl_ref, y_ref, mod_ref, g_ref, dskip_ref, w_ref, oc_ref, ol_ref, w_scr):
    n = pl.program_id(0)
    is_lat = n >= S5_CTX_STEPS

    @pl.when(n == 0)
    def _():
        w_scr[...] = w_ref[...].astype(BF16)

    mod = mod_ref[...]
    gain = g_ref[...]
    dskip = dskip_ref[...]
    hs, gs = [], []
    for c in range(S5_CB):
        h = jnp.where(is_lat, hl_ref[:, c], hc_ref[:, c])
        u = _norm_mod(h, gain, mod[:, :, 0:D], mod[:, :, D:2 * D])
        y = u * dskip + y_ref[c]
        hs.append(h)
        gs.append(_gelu_tanh(y).astype(BF16).reshape(S5_SEQ * S5_L, D))
    hh = jnp.dot(jnp.concatenate(gs, axis=0), w_scr[...], preferred_element_type=F32)
    rows = S5_SEQ * S5_L
    for c in range(S5_CB):
        blk = hh[rows * c:rows * (c + 1)]
        mix = (blk[:, 0:D] * jax.nn.sigmoid(blk[:, D:2 * D])).reshape(S5_SEQ, S5_L, D)
        out = hs[c] + mod[:, :, 2 * D:3 * D] * mix

        @pl.when(is_lat)
        def _():
            ol_ref[:, c] = out

        @pl.when(jnp.logical_not(is_lat))
        def _():
            oc_ref[:, c] = out


def _s5_out(h, y, mod8, gain, d_skip, w_glu):
    blk, ctx_idx, lat_idx = _s5_tile_specs()
    h_ctx, h_lat = pl.pallas_call(
        _s5_out_body,
        grid=(S5_STEPS,),
        in_specs=[pl.BlockSpec(blk, ctx_idx),
                  pl.BlockSpec(blk, lambda n: (N_CTX // N_LAT, lat_idx(n), 0, 0)),
                  pl.BlockSpec((None, S5_CB, S5_SEQ, S5_L, D), lambda n: (n, 0, 0, 0, 0)),
                  pl.BlockSpec((None, S5_SEQ, 1, 6 * D), lambda n: (jnp.where(n >= S5_CTX_STEPS, 1, 0), 0, 0, 0)),
                  pl.BlockSpec((1, D), lambda n: (0, 0)),
                  pl.BlockSpec((1, D), lambda n: (0, 0)),
                  pl.BlockSpec((D, 2 * D), lambda n: (0, 0))],
        out_specs=[pl.BlockSpec(blk, ctx_idx),
                   pl.BlockSpec(blk, lambda n: (0, lat_idx(n), 0, 0))],
        out_shape=[jax.ShapeDtypeStruct((CTX_B, CTX_CHUNKS, S5_L, D), F32),
                   jax.ShapeDtypeStruct((LAT_B, LAT_CHUNKS, S5_L, D), F32)],
        scratch_shapes=[pltpu.VMEM((D, 2 * D), BF16)],
        compiler_params=_params("arbitrary"),
        name="s5_out",
    )(h.reshape(N_TOK // CTX_S, CTX_CHUNKS, S5_L, D), h.reshape(N_TOK // LAT_S, LAT_CHUNKS, S5_L, D),
      y, mod8, gain.reshape(1, D), d_skip.reshape(1, D), w_glu)
    return h_ctx.reshape(N_CTX, D), h_lat.reshape(N_LAT, D)


def _s5_mixer(h, mods, layer, gain, state, lam_re, lam_im, log_dt, b_re, b_im, c_re, c_im, d_skip, w_glu):
    mod8 = _s5_mod8(mods, layer)
    u = _s5_in(h, mod8, gain)
    ut = u.reshape(S5_ROWS, S5_L, D).transpose(1, 2, 0)
    wall, avec = _s5_weights(lam_re, lam_im, log_dt, b_re, b_im, c_re, c_im)
    a2 = jnp.stack([jnp.concatenate([avec[:, 0], avec[:, 2]], axis=-1),
                    jnp.concatenate([avec[:, 1], avec[:, 3]], axis=-1)], axis=1)
    a2 = jnp.pad(a2, ((0, 0), (0, 6), (0, 0)))
    s0 = state.transpose(3, 0, 2, 1, 4).reshape(S5_G, LAT_B, 4 * S5_P)
    yt, fin = _s5_core(ut, wall, a2, s0)
    y = yt.transpose(2, 0, 1).reshape(S5_STEPS, S5_CB, S5_SEQ, S5_L, D)
    new_state = fin.reshape(S5_G, CTX_B, 2, 2, S5_P).transpose(1, 3, 2, 0, 4)
    return _s5_out(h, y, mod8, gain, d_skip, w_glu), new_state


def kernel(x_prompt, x_sample, cache_l0_k, cache_l0_v, state_l1, cache_l2_k, cache_l2_v, cache_l3_k, cache_l3_v, c, c_ctx, norm1, norm2, w_mod, b_mod, w_up, b_up, conv_k, conv_b, w_down, l0_w_qkv, l0_q_norm, l0_k_norm, l0_sink, l0_w_o, l1_lam_re, l1_lam_im, l1_log_dt, l1_b_re, l1_b_im, l1_c_re, l1_c_im, l1_d_skip, l1_w_glu, l2_w_qkv, l2_q_norm, l2_k_norm, l2_w_o, l3_w_qkv, l3_q_norm, l3_k_norm, l3_sink, l3_w_o):
    h = (x_prompt.reshape(N_CTX, D), x_sample.reshape(N_LAT, D))
    cond = jnp.concatenate([c_ctx[None, :], jnp.zeros((LAT_ROW0 - 1, D), F32), c], axis=0)
    mods = _modulation(cond, w_mod, b_mod)

    attn_layers = {
        0: (l0_w_qkv, l0_q_norm, l0_k_norm, l0_sink, l0_w_o, cache_l0_k, cache_l0_v, 16, 4, 64),
        2: (l2_w_qkv, l2_q_norm, l2_k_norm, None, l2_w_o, cache_l2_k, cache_l2_v, 8, 4, 128),
        3: (l3_w_qkv, l3_q_norm, l3_k_norm, l3_sink, l3_w_o, cache_l3_k, cache_l3_v, 16, 4, 64),
    }
    new_kv = {}
    new_state = None
    for layer in range(N_LAYERS):
        if layer in attn_layers:
            w_qkv, q_norm, k_norm, sink, w_o, ck, cv, n_heads, n_kv, dh = attn_layers[layer]
            q, k_ctx, k_lat, v_ctx, v_lat = _qkv_proj(h, mods, layer, norm1[layer], w_qkv, q_norm, k_norm,
                                                      n_heads, n_kv, dh)
            new_kv[layer] = (k_ctx.reshape(CTX_B, CTX_S, n_kv, dh), v_ctx.reshape(CTX_B, CTX_S, n_kv, dh))
            o_ctx, o_lat = _attention(q, k_ctx, k_lat, v_ctx, v_lat, ck, cv, sink, layer)
            h = _out_proj(o_ctx, o_lat, h, mods, layer, w_o)
        else:
            h, new_state = _s5_mixer(h, mods, layer, norm1[layer], state_l1, l1_lam_re, l1_lam_im, l1_log_dt,
                                     l1_b_re, l1_b_im, l1_c_re, l1_c_im, l1_d_skip, l1_w_glu)
        h = _ffn(h, mods, layer, norm2[layer], w_up, b_up, conv_k, conv_b, w_down,
                 split_out=layer == N_LAYERS - 1)

    y_prompt = h[0].reshape(CTX_B, CTX_S, D)
    y_sample = h[1].reshape(LAT_B, LAT_S, D)
    return (y_prompt, y_sample, new_kv[0][0], new_kv[0][1], new_state,
            new_kv[2][0], new_kv[2][1], new_kv[3][0], new_kv[3][1])
```

```python
import functools
import math

import jax
import jax.numpy as jnp
import numpy as np
from jax import lax
from jax.experimental import pallas as pl
from jax.experimental.pallas import tpu as pltpu

F32 = jnp.float32
BF16 = jnp.bfloat16

D = 1024
N_LAYERS = 4
CTX_B, CTX_S = 32, 256
LAT_B, LAT_S = 8, 1024
PAST = 512
N_CTX = CTX_B * CTX_S
N_LAT = LAT_B * LAT_S
N_TOK = N_CTX + N_LAT
GRID_W = 64
WINDOW = 128
ROPE_THETA = 10000.0
EPS = 1e-6
D_FF = 2816
N_COND = 16
LAT_ROW0 = 8

S5_G = 64
S5_C = 16
S5_P = 64
S5_L = 16

VMEM_LIMIT = 56 * 1024 * 1024
LANES = 128
NEG_BIG = -1e30
LOG2E = math.log2(math.e)

TM = 512
TM_FFN = 1024
TF = 256
TQ = 256
ATT_RB = 32
STAGE_RB = 128
QKV_RB = 64


def _params(*sem):
    return pltpu.CompilerParams(dimension_semantics=sem, vmem_limit_bytes=VMEM_LIMIT)


def _cond_row(i, tm):
    n_ctx_tiles = N_CTX // tm
    return jnp.where(i < n_ctx_tiles, 0, LAT_ROW0 + (i - n_ctx_tiles) // (LAT_S // tm))


def _norm_mod(x, gain, shift, scale):
    ms = jnp.mean(x * x, axis=-1, keepdims=True)
    return (x * lax.rsqrt(ms + EPS) * gain) * (1.0 + scale) + shift


def _silu(x):
    return x * jax.nn.sigmoid(x)


def _mod_body(cond_ref, w_ref, b_ref, o_ref):
    s = _silu(cond_ref[...]).astype(BF16)
    o_ref[...] = jnp.dot(s, w_ref[...].astype(BF16), preferred_element_type=F32) + b_ref[...]


def _modulation(cond, w_mod, b_mod):
    tn = 1536
    return pl.pallas_call(
        _mod_body,
        grid=(N_LAYERS, 6 * D // tn),
        in_specs=[
            pl.BlockSpec((N_COND, D), lambda l, n: (0, 0)),
            pl.BlockSpec((None, D, tn), lambda l, n: (l, 0, n)),
            pl.BlockSpec((None, 1, tn), lambda l, n: (l, 0, n)),
        ],
        out_specs=pl.BlockSpec((None, N_COND, tn), lambda l, n: (l, 0, n)),
        out_shape=jax.ShapeDtypeStruct((N_LAYERS, N_COND, 6 * D), F32),
        compiler_params=_params("arbitrary", "arbitrary"),
        name="modulation",
    )(cond, w_mod, b_mod.reshape(N_LAYERS, 1, 6 * D))


def _mod_spec(layer, tm):
    return pl.BlockSpec((None, None, 1, 6 * D), lambda i, *_: (layer, _cond_row(i, tm), 0, 0))


def _token_pair(h, tm):
    n_ctx_tiles = N_CTX // tm
    h_ctx, h_lat, lat_off = (h[0], h[1], 0) if isinstance(h, tuple) else (h, h, n_ctx_tiles)
    specs = [pl.BlockSpec((tm, D), lambda i, *_: (jnp.minimum(i, n_ctx_tiles - 1), 0)),
             pl.BlockSpec((tm, D), lambda i, *_: (jnp.maximum(i - n_ctx_tiles, 0) + lat_off, 0))]
    return (h_ctx, h_lat), specs


def _split_specs(tm, width):
    n_ctx_tiles = N_CTX // tm
    return [pl.BlockSpec((tm, width), lambda i, *_: (jnp.minimum(i, n_ctx_tiles - 1), 0)),
            pl.BlockSpec((tm, width), lambda i, *_: (jnp.maximum(i - n_ctx_tiles, 0), 0))]


def _qkv_body(hc_ref, hl_ref, mod_ref, g_ref, w_ref, qn_ref, kn_ref, bd_ref, cos_ref, sin_ref,
              q_ref, kc_ref, kl_ref, vc_ref, vl_ref, w_scr, xn_scr, qkv_scr, sq_scr, ss_scr, *, nq, nk, dh):
    is_lat = pl.program_id(0) >= N_CTX // TM

    @pl.when(pl.program_id(0) == 0)
    def _():
        w_scr[...] = w_ref[...].astype(BF16)

    mod = mod_ref[...]
    row_blocks = [slice(r, r + QKV_RB) for r in range(0, TM, QKV_RB)]
    for rb in row_blocks:
        h = jnp.where(is_lat, hl_ref[rb, :], hc_ref[rb, :])
        xn_scr[rb, :] = _norm_mod(h, g_ref[...], mod[:, 0:D], mod[:, D:2 * D]).astype(BF16)
    qkv_scr[...] = jnp.dot(xn_scr[...], w_scr[...], preferred_element_type=F32)

    bd = bd_ref[...]
    quarter = dh // 4
    lane = lax.broadcasted_iota(jnp.int32, (QKV_RB, LANES), 1)
    first = (lane % (2 * quarter)) < quarter
    inv_dh = 1.0 / dh

    def head_norm_rope(c0, gain, out_ref, o0, rope):
        for rb in row_blocks:
            z = qkv_scr[rb, c0:c0 + 256]
            sq_scr[rb, :] = (z * z).astype(BF16)
        ss_scr[...] = jnp.dot(sq_scr[...], bd, preferred_element_type=F32)
        for rb in row_blocks:
            zn = qkv_scr[rb, c0:c0 + 256] * lax.rsqrt(ss_scr[rb, :] * inv_dh + EPS)
            for j in range(2):
                zz = zn[:, LANES * j:LANES * (j + 1)] * gain
                if rope:
                    partner = jnp.where(first, pltpu.roll(zz, LANES - quarter, 1), pltpu.roll(zz, quarter, 1))
                    zz = zz * cos_ref[rb, :] + partner * sin_ref[rb, :]
                out_ref[rb, o0 + LANES * j:o0 + LANES * (j + 1)] = zz.astype(out_ref.dtype)

    qgain = qn_ref[...] * (dh ** -0.5 * LOG2E)
    kgain = kn_ref[...]

    def finish(k_ref, v_ref, rope):
        for c in range(nq // 256):
            head_norm_rope(256 * c, qgain, q_ref, 256 * c, rope)
        for c in range(nk // 256):
            head_norm_rope(nq + 256 * c, kgain, k_ref, 256 * c, rope)
        for rb in row_blocks:
            v_ref[rb, :] = qkv_scr[rb, nq + nk:]

    pl.when(is_lat)(functools.partial(finish, kl_ref, vl_ref, True))
    pl.when(jnp.logical_not(is_lat))(functools.partial(finish, kc_ref, vc_ref, False))


def _rope_tables(dh):
    half, quarter = dh // 2, dh // 4
    freqs = 1.0 / (ROPE_THETA ** (np.arange(quarter, dtype=np.float32) / quarter))
    pos = np.arange(LAT_S)
    row = (pos // GRID_W).astype(np.float32)
    col = (pos % GRID_W).astype(np.float32)
    ang_r = (row[:, None] * freqs[None, :]).astype(np.float32)
    ang_c = (col[:, None] * freqs[None, :]).astype(np.float32)
    cos = np.concatenate([np.cos(ang_r), np.cos(ang_r), np.cos(ang_c), np.cos(ang_c)], axis=1)
    sin = np.concatenate([-np.sin(ang_r), np.sin(ang_r), -np.sin(ang_c), np.sin(ang_c)], axis=1)
    reps = LANES // dh
    cos = np.tile(cos.astype(np.float32), (1, reps))
    sin = np.tile(sin.astype(np.float32), (1, reps))
    return jnp.asarray(cos), jnp.asarray(sin)


def _block_diag_ones(dh):
    idx = np.arange(256) // dh
    return jnp.asarray((idx[:, None] == idx[None, :]).astype(np.float32), dtype=BF16)


def _qkv_proj(h, mods, layer, gain, w_qkv, q_norm, k_norm, n_heads, n_kv, dh):
    nq, nk = n_heads * dh, n_kv * dh
    nqkv = nq + 2 * nk
    cos, sin = _rope_tables(dh)
    reps = LANES // dh
    n_ctx_tiles = N_CTX // TM
    lat_tiles = LAT_S // TM

    def rope_idx(i):
        return (jnp.where(i < n_ctx_tiles, 0, (i - n_ctx_tiles) % lat_tiles), 0)

    h_ops, h_specs = _token_pair(h, TM)
    kv_specs = _split_specs(TM, nk)
    kv_shapes = [jax.ShapeDtypeStruct((N_CTX, nk), F32), jax.ShapeDtypeStruct((N_LAT, nk), F32)]
    return pl.pallas_call(
        functools.partial(_qkv_body, nq=nq, nk=nk, dh=dh),
        grid=(N_TOK // TM,),
        in_specs=h_specs + [
            _mod_spec(layer, TM),
            pl.BlockSpec((1, D), lambda i: (0, 0)),
            pl.BlockSpec((D, nqkv), lambda i: (0, 0)),
            pl.BlockSpec((1, LANES), lambda i: (0, 0)),
            pl.BlockSpec((1, LANES), lambda i: (0, 0)),
            pl.BlockSpec((256, 256), lambda i: (0, 0)),
            pl.BlockSpec((TM, LANES), rope_idx),
            pl.BlockSpec((TM, LANES), rope_idx),
        ],
        out_specs=[pl.BlockSpec((TM, nq), lambda i: (i, 0))] + kv_specs + kv_specs,
        out_shape=[jax.ShapeDtypeStruct((N_TOK, nq), BF16)] + kv_shapes + kv_shapes,
        scratch_shapes=[pltpu.VMEM((D, nqkv), BF16), pltpu.VMEM((TM, D), BF16), pltpu.VMEM((TM, nqkv), F32),
                        pltpu.VMEM((TM, 256), BF16),
                        pltpu.VMEM((TM, 256), F32)],
        compiler_params=_params("arbitrary"),
        name=f"qkv_l{layer}",
    )(*h_ops, mods.reshape(N_LAYERS, N_COND, 1, 6 * D), gain.reshape(1, D), w_qkv,
      jnp.tile(q_norm, reps).reshape(1, LANES), jnp.tile(k_norm, reps).reshape(1, LANES),
      _block_diag_ones(dh), cos, sin)


def _attend(q, segs, sink_of, tq, scr):
    s_scr, p_scr, r_scr = scr
    m_rows = q.shape[0]
    cols = []
    col = 0
    for k, _, _ in segs:
        t = k.shape[0]
        s_scr[0:m_rows, col:col + t] = lax.dot_general(q, k, (((1,), (1,)), ((), ())), preferred_element_type=F32)
        cols.append((col, t))
        col += t
    for r0 in range(0, m_rows, ATT_RB):
        rb = slice(r0, r0 + ATT_RB)
        parts = []
        for (c0, t), (_, _, bias) in zip(cols, segs):
            s = s_scr[rb, c0:c0 + t]
            if bias is not None:
                s = s + bias[r0 % tq:r0 % tq + ATT_RB, :]
            parts.append(s)
        m = functools.reduce(jnp.maximum, [jnp.max(s, axis=-1, keepdims=True) for s in parts])
        sink = sink_of(r0)
        if sink is not None:
            m = jnp.maximum(m, sink)
        den = None if sink is None else jnp.exp2(sink - m)
        for (c0, t), s in zip(cols, parts):
            p = jnp.exp2(s - m)
            ps = jnp.sum(p, axis=-1, keepdims=True)
            den = ps if den is None else den + ps
            p_scr[rb, c0:c0 + t] = p.astype(BF16)
        r_scr[rb, :] = jnp.broadcast_to(1.0 / den, (ATT_RB, LANES))
    acc = None
    for (c0, t), (_, v, _) in zip(cols, segs):
        pv = jnp.dot(p_scr[0:m_rows, c0:c0 + t], v, preferred_element_type=F32)
        acc = pv if acc is None else acc + pv
    return acc * r_scr[0:m_rows, :]


def _dup_halves(x, kv):
    lane = lax.broadcasted_iota(jnp.int32, x.shape, 1)
    r = pltpu.roll(x, 64, 1)
    lo = lane < 64
    return (jnp.where(lo, x, r) if kv % 2 == 0 else jnp.where(lo, r, x)).astype(BF16)


def _dup_segment(k, v, kv):
    c = LANES * (kv // 2)
    return _dup_halves(k[:, c:c + LANES], kv), _dup_halves(v[:, c:c + LANES], kv)


def _stage_rows(dst_ref, rows, fn):
    for kv in range(4):
        for r in range(0, rows, STAGE_RB):
            dst_ref[kv, r:r + STAGE_RB, :] = fn(slice(r, r + STAGE_RB), kv)


def _attend_heads64(q_ref, o_ref, sink_ref, segs_of, biases, tq, scr):
    lane = lax.broadcasted_iota(jnp.int32, (tq, LANES), 1)
    lo = lane < 64
    for kv in range(4):
        segs = [(k, v, b) for (k, v), b in zip(segs_of(kv), biases)]
        parts = []
        for pair in range(2):
            j = 2 * kv + pair
            qp = q_ref[:, LANES * j:LANES * (j + 1)]
            zero = jnp.zeros_like(qp)
            parts += [jnp.where(lo, qp, zero), jnp.where(lo, zero, qp)]
        sinks = [sink_ref[4 * kv + r] * LOG2E for r in range(4)]
        out = _attend(jnp.concatenate(parts, axis=0), segs, lambda r0: sinks[r0 // tq], tq, scr)
        for pair in range(2):
            j = 2 * kv + pair
            a = out[(2 * pair) * tq:(2 * pair + 1) * tq]
            b = out[(2 * pair + 1) * tq:(2 * pair + 2) * tq]
            o_ref[:, LANES * j:LANES * (j + 1)] = jnp.where(lo, a, b).astype(BF16)


def _attend_heads128(q_ref, o_ref, segs_of, tq, scr):
    for kv in range(4):
        segs = [(k, v, None) for k, v in segs_of(kv)]
        q = jnp.concatenate([q_ref[:, LANES * (2 * kv):LANES * (2 * kv + 1)],
                             q_ref[:, LANES * (2 * kv + 1):LANES * (2 * kv + 2)]], axis=0)
        out = _attend(q, segs, lambda r0: None, tq, scr)
        o_ref[:, LANES * (2 * kv):LANES * (2 * kv + 1)] = out[:tq].astype(BF16)
        o_ref[:, LANES * (2 * kv + 1):LANES * (2 * kv + 2)] = out[tq:].astype(BF16)


def _ctx_attn_a_body(sink_ref, q_ref, k_ref, v_ref, o_ref, *scr):
    _attend_heads64(q_ref, o_ref, sink_ref, lambda kv: [_dup_segment(k_ref[...], v_ref[...], kv)], [None],
                    CTX_S, scr)


def _cols128(ref, kv):
    return ref[:, LANES * kv:LANES * (kv + 1)].astype(BF16)


def _ctx_attn_c_body(q_ref, k_ref, v_ref, o_ref, *scr):
    _attend_heads128(q_ref, o_ref, lambda kv: [(_cols128(k_ref, kv), _cols128(v_ref, kv))], CTX_S, scr)


def _lat_attn_a_body(sink_ref, q_ref, k_ref, v_ref, ck_ref, cv_ref, o_ref, ck_scr, cv_scr, k_scr, v_scr,
                     bias_scr, *scr):
    qi = pl.program_id(1)

    @pl.when(qi == 0)
    def _():
        for src, dst, rows in ((ck_ref, ck_scr, PAST), (cv_ref, cv_scr, PAST),
                               (k_ref, k_scr, LAT_S), (v_ref, v_scr, LAT_S)):
            _stage_rows(dst, rows, lambda rb, kv, src=src: _dup_halves(
                src[rb, LANES * (kv // 2):LANES * (kv // 2 + 1)], kv))

    band = TQ + 2 * WINDOW
    ws = pl.multiple_of(jnp.clip(qi * TQ - WINDOW, 0, LAT_S - band), WINDOW)
    qpos = qi * TQ + lax.broadcasted_iota(jnp.int32, (TQ, band), 0)
    kpos = ws + lax.broadcasted_iota(jnp.int32, (TQ, band), 1)
    bias_scr[...] = jnp.where(jnp.abs(qpos - kpos) <= WINDOW, 0.0, NEG_BIG)
    segs_of = lambda kv: [(ck_scr[kv], cv_scr[kv]), (k_scr[kv, pl.ds(ws, band), :], v_scr[kv, pl.ds(ws, band), :])]
    _attend_heads64(q_ref, o_ref, sink_ref, segs_of, [None, bias_scr], TQ, scr)


def _lat_attn_c_body(q_ref, k_ref, v_ref, ck_ref, cv_ref, o_ref, ck_scr, cv_scr, k_scr, v_scr, *scr):
    @pl.when(pl.program_id(1) == 0)
    def _():
        for src, dst, rows in ((ck_ref, ck_scr, PAST), (cv_ref, cv_scr, PAST),
                               (k_ref, k_scr, LAT_S), (v_ref, v_scr, LAT_S)):
            _stage_rows(dst, rows, lambda rb, kv, src=src: src[rb, LANES * kv:LANES * (kv + 1)].astype(BF16))

    _attend_heads128(q_ref, o_ref, lambda kv: [(ck_scr[kv], cv_scr[kv]), (k_scr[kv], v_scr[kv])], TQ, scr)


_SMEM_SPEC = pl.BlockSpec(memory_space=pltpu.SMEM)


def _attention(q, k_ctx, k_lat, v_ctx, v_lat, cache_k, cache_v, sink, layer):
    nk = k_ctx.shape[1]
    ck = cache_k.reshape(LAT_B, PAST, nk)
    cv = cache_v.reshape(LAT_B, PAST, nk)
    ctx_specs = [
        pl.BlockSpec((CTX_S, D), lambda b: (b, 0)),
        pl.BlockSpec((CTX_S, nk), lambda b: (b, 0)),
        pl.BlockSpec((CTX_S, nk), lambda b: (b, 0)),
    ]
    qb = LAT_S // TQ
    lat_specs = [
        pl.BlockSpec((TQ, D), lambda b, i: (N_CTX // TQ + b * qb + i, 0)),
        pl.BlockSpec((LAT_S, nk), lambda b, i: (b, 0)),
        pl.BlockSpec((LAT_S, nk), lambda b, i: (b, 0)),
        pl.BlockSpec((None, PAST, nk), lambda b, i: (b, 0, 0)),
        pl.BlockSpec((None, PAST, nk), lambda b, i: (b, 0, 0)),
    ]
    heads_per_kv = D // nk
    band = TQ + 2 * WINDOW

    def softmax_scratch(m_rows, keys):
        return [pltpu.VMEM((m_rows, keys), F32), pltpu.VMEM((m_rows, keys), BF16), pltpu.VMEM((m_rows, LANES), F32)]

    lat_keys = PAST + (band if sink is not None else LAT_S)
    staged = [pltpu.VMEM((4, PAST, LANES), BF16), pltpu.VMEM((4, PAST, LANES), BF16),
              pltpu.VMEM((4, LAT_S, LANES), BF16), pltpu.VMEM((4, LAT_S, LANES), BF16)]
    if sink is not None:
        staged.append(pltpu.VMEM((TQ, band), F32))
    ctx_out = dict(out_specs=pl.BlockSpec((CTX_S, D), lambda b: (b, 0)),
                   out_shape=jax.ShapeDtypeStruct((N_CTX, D), BF16),
                   scratch_shapes=softmax_scratch(heads_per_kv * CTX_S, CTX_S),
                   grid=(CTX_B,), compiler_params=_params("arbitrary"))
    lat_out = dict(out_specs=pl.BlockSpec((TQ, D), lambda b, i: (b * qb + i, 0)),
                   out_shape=jax.ShapeDtypeStruct((N_LAT, D), BF16),
                   scratch_shapes=staged + softmax_scratch(heads_per_kv * TQ, lat_keys),
                   grid=(LAT_B, qb), compiler_params=_params("arbitrary", "arbitrary"))
    if sink is not None:
        o_ctx = pl.pallas_call(_ctx_attn_a_body, in_specs=[_SMEM_SPEC] + ctx_specs,
                               name=f"attn_ctx_l{layer}", **ctx_out)(sink, q, k_ctx, v_ctx)
        o_lat = pl.pallas_call(_lat_attn_a_body, in_specs=[_SMEM_SPEC] + lat_specs,
                               name=f"attn_lat_l{layer}", **lat_out)(sink, q, k_lat, v_lat, ck, cv)
    else:
        o_ctx = pl.pallas_call(_ctx_attn_c_body, in_specs=ctx_specs,
                               name=f"attn_ctx_l{layer}", **ctx_out)(q, k_ctx, v_ctx)
        o_lat = pl.pallas_call(_lat_attn_c_body, in_specs=lat_specs,
                               name=f"attn_lat_l{layer}", **lat_out)(q, k_lat, v_lat, ck, cv)
    return o_ctx, o_lat


def _oproj_body(oc_ref, ol_ref, hc_ref, hl_ref, mod_ref, w_ref, out_ref, w_scr):
    @pl.when(pl.program_id(0) == 0)
    def _():
        w_scr[...] = w_ref[...].astype(BF16)

    is_ctx = pl.program_id(0) < N_CTX // TM
    o = jnp.where(is_ctx, oc_ref[...], ol_ref[...])
    h = jnp.where(is_ctx, hc_ref[...], hl_ref[...])
    mix = jnp.dot(o, w_scr[...], preferred_element_type=F32)
    out_ref[...] = h + mod_ref[:, 2 * D:3 * D] * mix


def _out_proj(o_ctx, o_lat, h, mods, layer, w_o):
    (o_ops, o_specs), (h_ops, h_specs) = _token_pair((o_ctx, o_lat), TM), _token_pair(h, TM)
    return pl.pallas_call(
        _oproj_body,
        grid=(N_TOK // TM,),
        in_specs=o_specs + h_specs + [_mod_spec(layer, TM), pl.BlockSpec((D, D), lambda i: (0, 0))],
        out_specs=pl.BlockSpec((TM, D), lambda i: (i, 0)),
        out_shape=jax.ShapeDtypeStruct((N_TOK, D), F32),
        scratch_shapes=[pltpu.VMEM((D, D), BF16)],
        compiler_params=_params("arbitrary"),
        name=f"oproj_l{layer}",
    )(*o_ops, *h_ops, mods.reshape(N_LAYERS, N_COND, 1, 6 * D), w_o)


FFN_CHUNK = CTX_S
FFN_SLOT = FFN_CHUNK + 16
FFN_NCHUNK = TM_FFN // FFN_CHUNK


def _ffn_body(*refs, n_h, n_out):
    h_refs, refs = refs[:n_h], refs[n_h:]
    (mod_ref, g_ref, wg_ref, wv_ref, bg_ref, bv_ref, kg_ref, kv_ref, cg_ref, cv_ref, wd_ref) = refs[:11]
    out_refs, refs = refs[11:11 + n_out], refs[11 + n_out:]
    xn_scr, acc_scr, ug_scr, uv_scr, wg_scr, wv_scr, wd_scr = refs
    i = pl.program_id(0)
    f = pl.program_id(1)
    is_lat = i >= N_CTX // TM_FFN

    def load_h():
        if n_h == 1:
            return h_refs[0][...]
        return jnp.where(is_lat, h_refs[1][...], h_refs[0][...])

    @pl.when(f == 0)
    def _():
        mod = mod_ref[...]
        xn_scr[...] = _norm_mod(load_h(), g_ref[...], mod[:, 3 * D:4 * D], mod[:, 4 * D:5 * D]).astype(BF16)
        acc_scr[...] = jnp.zeros_like(acc_scr)

    wg_scr[...] = wg_ref[...].astype(BF16)
    wv_scr[...] = wv_ref[...].astype(BF16)
    wd_scr[...] = wd_ref[...].astype(BF16)
    base = [8 + FFN_SLOT * k for k in range(FFN_NCHUNK)]

    for k in range(FFN_NCHUNK):
        xk = xn_scr[FFN_CHUNK * k:FFN_CHUNK * (k + 1), :]
        ug_scr[base[k]:base[k] + FFN_CHUNK, :] = jnp.dot(xk, wg_scr[...], preferred_element_type=F32)
        uv_scr[base[k]:base[k] + FFN_CHUNK, :] = jnp.dot(xk, wv_scr[...], preferred_element_type=F32)

    for scr, b_ref in ((ug_scr, bg_ref), (uv_scr, bv_ref)):
        pad = -b_ref[...]
        tops = [pad] + [jnp.where(is_lat, scr[base[k - 1] + FFN_CHUNK - 1:base[k - 1] + FFN_CHUNK, :], pad)
                        for k in range(1, FFN_NCHUNK)]
        bots = [jnp.where(is_lat, scr[base[k + 1]:base[k + 1] + 1, :], pad)
                for k in range(FFN_NCHUNK - 1)] + [pad]
        for k in range(FFN_NCHUNK):
            scr[base[k] - 1:base[k], :] = tops[k]
            scr[base[k] + FFN_CHUNK:base[k] + FFN_CHUNK + 1, :] = bots[k]

    def conv(scr, b_ref, k_ref, c_ref, k):
        kk = k_ref[...]
        const = c_ref[...] + b_ref[...] * (kk[0:1] + kk[1:2] + kk[2:3])
        lo = base[k]
        return (const + kk[0:1] * scr[lo - 1:lo - 1 + FFN_CHUNK, :] + kk[1:2] * scr[lo:lo + FFN_CHUNK, :]
                + kk[2:3] * scr[lo + 1:lo + 1 + FFN_CHUNK, :])

    for k in range(FFN_NCHUNK):
        gate = conv(ug_scr, bg_ref, kg_ref, cg_ref, k)
        val = conv(uv_scr, bv_ref, kv_ref, cv_ref, k)
        a = (_silu(gate) * val).astype(BF16)
        rows = slice(FFN_CHUNK * k, FFN_CHUNK * (k + 1))
        acc_scr[rows, :] += jnp.dot(a, wd_scr[...], preferred_element_type=F32)

    done = f == pl.num_programs(1) - 1

    def write(out_ref):
        out_ref[...] = load_h() + mod_ref[:, 5 * D:6 * D] * acc_scr[...]

    if n_out == 1:
        pl.when(done)(functools.partial(write, out_refs[0]))
    else:
        pl.when(jnp.logical_and(done, jnp.logical_not(is_lat)))(functools.partial(write, out_refs[0]))
        pl.when(jnp.logical_and(done, is_lat))(functools.partial(write, out_refs[1]))


def _ffn(h, mods, layer, gain, w_up, b_up, conv_k, conv_b, w_down, split_out=False):
    nf = D_FF // TF
    b3 = b_up.reshape(N_LAYERS, 1, 2 * D_FF)
    c3 = conv_b.reshape(N_LAYERS, 1, 2 * D_FF)
    col = lambda off: (lambda i, f: (layer, 0, off + f))
    if isinstance(h, tuple):
        h_ops, h_specs = _token_pair(h, TM_FFN)
    else:
        h_ops, h_specs = (h,), [pl.BlockSpec((TM_FFN, D), lambda i, f: (i, 0))]
    if split_out:
        out_specs = _split_specs(TM_FFN, D)
        out_shape = [jax.ShapeDtypeStruct((N_CTX, D), F32), jax.ShapeDtypeStruct((N_LAT, D), F32)]
    else:
        out_specs = pl.BlockSpec((TM_FFN, D), lambda i, f: (i, 0))
        out_shape = jax.ShapeDtypeStruct((N_TOK, D), F32)
    slot_rows = 8 + FFN_SLOT * FFN_NCHUNK
    result = pl.pallas_call(
        functools.partial(_ffn_body, n_h=len(h_ops), n_out=2 if split_out else 1),
        grid=(N_TOK // TM_FFN, nf),
        in_specs=h_specs + [
            _mod_spec(layer, TM_FFN),
            pl.BlockSpec((1, D), lambda i, f: (0, 0)),
            pl.BlockSpec((None, D, TF), col(0)),
            pl.BlockSpec((None, D, TF), col(nf)),
            pl.BlockSpec((None, 1, TF), col(0)),
            pl.BlockSpec((None, 1, TF), col(nf)),
            pl.BlockSpec((None, 3, TF), col(0)),
            pl.BlockSpec((None, 3, TF), col(nf)),
            pl.BlockSpec((None, 1, TF), col(0)),
            pl.BlockSpec((None, 1, TF), col(nf)),
            pl.BlockSpec((None, TF, D), lambda i, f: (layer, f, 0)),
        ],
        out_specs=out_specs,
        out_shape=out_shape,
        scratch_shapes=[pltpu.VMEM((TM_FFN, D), BF16), pltpu.VMEM((TM_FFN, D), F32),
                        pltpu.VMEM((slot_rows, TF), F32), pltpu.VMEM((slot_rows, TF), F32),
                        pltpu.VMEM((D, TF), BF16), pltpu.VMEM((D, TF), BF16), pltpu.VMEM((TF, D), BF16)],
        compiler_params=_params("arbitrary", "arbitrary"),
        name=f"ffn_l{layer}",
    )(*h_ops, mods.reshape(N_LAYERS, N_COND, 1, 6 * D), gain.reshape(1, D), w_up, w_up, b3, b3,
      conv_k, conv_k, c3, c3, w_down)
    return tuple(result) if split_out else result


S5_SEQ = 8
S5_CB = 4
CTX_SETS = CTX_B // S5_SEQ
CTX_CHUNKS = CTX_S // S5_L
LAT_CHUNKS = LAT_S // S5_L
S5_CTX_STEPS = CTX_SETS * CTX_CHUNKS // S5_CB
S5_LAT_STEPS = LAT_CHUNKS // S5_CB
S5_STEPS = S5_CTX_STEPS + S5_LAT_STEPS
S5_ROWS = (N_CTX + N_LAT) // S5_L
S5_CTX_ROWS = N_CTX // S5_L


def _s5_tile_specs():
    per = CTX_CHUNKS // S5_CB
    blk = (S5_SEQ, S5_CB, S5_L, D)

    def ctx_idx(n):
        m = jnp.minimum(n, S5_CTX_STEPS - 1)
        return (m // per, m % per, 0, 0)

    return blk, ctx_idx, (lambda n: (jnp.maximum(n - S5_CTX_STEPS, 0)))


def _s5_mod8(mods, layer):
    m = mods[layer]
    return jnp.stack([jnp.broadcast_to(m[0:1], (S5_SEQ, 6 * D)), m[LAT_ROW0:LAT_ROW0 + LAT_B]]).reshape(
        2, S5_SEQ, 1, 6 * D)


def _s5_in_body(hc_ref, hl_ref, mod_ref, g_ref, u_ref):
    is_lat = pl.program_id(0) >= S5_CTX_STEPS
    mod = mod_ref[...]
    gain = g_ref[...]
    for c in range(S5_CB):
        x = jnp.where(is_lat, hl_ref[:, c], hc_ref[:, c])
        u_ref[c] = _norm_mod(x, gain, mod[:, :, 0:D], mod[:, :, D:2 * D]).astype(BF16)


def _s5_in(h, mod8, gain):
    blk, ctx_idx, lat_idx = _s5_tile_specs()
    return pl.pallas_call(
        _s5_in_body,
        grid=(S5_STEPS,),
        in_specs=[pl.BlockSpec(blk, ctx_idx),
                  pl.BlockSpec(blk, lambda n: (N_CTX // N_LAT, lat_idx(n), 0, 0)),
                  pl.BlockSpec((None, S5_SEQ, 1, 6 * D), lambda n: (jnp.where(n >= S5_CTX_STEPS, 1, 0), 0, 0, 0)),
                  pl.BlockSpec((1, D), lambda n: (0, 0))],
        out_specs=pl.BlockSpec((None, S5_CB, S5_SEQ, S5_L, D), lambda n: (n, 0, 0, 0, 0)),
        out_shape=jax.ShapeDtypeStruct((S5_STEPS, S5_CB, S5_SEQ, S5_L, D), BF16),
        compiler_params=_params("arbitrary"),
        name="s5_in",
    )(h.reshape(N_TOK // CTX_S, CTX_CHUNKS, S5_L, D), h.reshape(N_TOK // LAT_S, LAT_CHUNKS, S5_L, D),
      mod8, gain.reshape(1, D))


def _s5_weights_body(lamc_re_ref, lamc_im_ref, lamr_re_ref, lamr_im_ref, ldt_ref,
                     bt_re_ref, bt_im_ref, btr_re_ref, btr_im_ref, ct_re_ref, ct_im_ref,
                     w_ref, a_ref):
    blk = (lax.broadcasted_iota(jnp.int32, (1, 256), 1) // S5_L).astype(F32)
    lane256 = lax.broadcasted_iota(jnp.int32, (S5_C, 256), 1)
    hi = lax.Precision.HIGHEST
    krow = []
    st_rows = {}
    w_rows = {}
    for d in range(2):
        dt = jnp.exp(ldt_ref[d])
        lr = lamc_re_ref[d]
        li = lamc_im_ref[d]

        def powers(expo):
            mag = jnp.exp((lr * dt) * expo)
            ang = (li * dt) * expo
            return mag * jnp.cos(ang), mag * jnp.sin(ang)

        asc_r, asc_i = powers(blk)
        dsc_r, dsc_i = powers(15.0 - blk)
        ar = asc_r[:, S5_L:S5_L + 1]
        ai = asc_i[:, S5_L:S5_L + 1]
        den = lr * lr + li * li
        n_re = ar - 1.0
        f_re = (n_re * lr + ai * li) / den
        f_im = (ai * lr - n_re * li) / den
        bbr = f_re * bt_re_ref[d] - f_im * bt_im_ref[d]
        bbi = f_re * bt_im_ref[d] + f_im * bt_re_ref[d]
        e0r, e0i = (asc_r, asc_i) if d == 0 else (dsc_r, dsc_i)
        e1r = e0r * ar - e0i * ai
        e1i = e0r * ai + e0i * ar
        pr, pi = (dsc_r, dsc_i) if d == 0 else (asc_r, asc_i)
        st_rows[("re", d)] = pr * bbr - pi * bbi
        st_rows[("im", d)] = pr * bbi + pi * bbr
        ctr = ct_re_ref[d]
        cti = ct_im_ref[d]
        k_re = ctr * e0r - cti * e0i
        k_imneg = -(ctr * e0i + cti * e0r)
        w_rows[("re", d)] = ctr * e1r - cti * e1i
        w_rows[("im", d)] = -(ctr * e1i + cti * e1r)
        lrr = lamr_re_ref[d:d + 1, :]
        lir = lamr_im_ref[d:d + 1, :]
        magr = jnp.exp(lrr * dt)
        arr = magr * jnp.cos(lir * dt)
        air = magr * jnp.sin(lir * dt)
        denr = lrr * lrr + lir * lir
        nr = arr - 1.0
        fr = (nr * lrr + air * lir) / denr
        fi = (air * lrr - nr * lir) / denr
        bbr_row = fr * btr_re_ref[d] - fi * btr_im_ref[d]
        bbi_row = fr * btr_im_ref[d] + fi * btr_re_ref[d]
        krow.append(jnp.dot(bbr_row, k_re, precision=hi, preferred_element_type=F32)
                    + jnp.dot(bbi_row, k_imneg, precision=hi, preferred_element_type=F32))
        mag16 = jnp.exp(lrr * dt * 16.0)
        a_ref[2 * d:2 * d + 1, :] = mag16 * jnp.cos(lir * dt * 16.0)
        a_ref[2 * d + 1:2 * d + 2, :] = mag16 * jnp.sin(lir * dt * 16.0)
    a_ref[4:8, :] = jnp.zeros((4, S5_P), F32)

    t_rows = []
    for s in range(S5_L):
        fwd = krow[0] if s == 0 else jnp.where(lane256 >= S5_C * s, pltpu.roll(krow[0], S5_C * s, 1), 0.0)
        bwd = krow[1] if s == S5_L - 1 else jnp.where(lane256 < S5_C * (s + 1),
                                                      pltpu.roll(krow[1], S5_C * (s + 1), 1), 0.0)
        t_rows.append(fwd + bwd)
    order = [("re", 0), ("re", 1), ("im", 0), ("im", 1)]
    w_ref[0:256, :] = jnp.concatenate(t_rows, axis=0).T.astype(BF16)
    w_ref[256:512, :] = jnp.concatenate([st_rows[o] for o in order], axis=0).astype(BF16)
    w_ref[512:768, :] = jnp.concatenate([w_rows[o] for o in order], axis=0).T.astype(BF16)


def _s5_weights(lam_re, lam_im, log_dt, b_re, b_im, c_re, c_im):
    col = lambda x: x.transpose(1, 0, 2).reshape(S5_G, 2, S5_P, 1)
    row = lambda x: x.transpose(1, 0, 2)
    ldt = log_dt.transpose(1, 0).reshape(S5_G, 2, 1, 1)
    b_tiled = lambda x: jnp.tile(x.transpose(1, 0, 2, 3), (1, 1, 1, S5_L))
    b_rowf = lambda x: x.transpose(1, 0, 3, 2)
    c_tiled = lambda x: jnp.tile(x.transpose(1, 0, 3, 2), (1, 1, 1, S5_L))
    g4 = lambda *tail: pl.BlockSpec((None,) + tail, lambda g: (g,) + (0,) * len(tail))
    return pl.pallas_call(
        _s5_weights_body,
        grid=(S5_G,),
        in_specs=[g4(2, S5_P, 1), g4(2, S5_P, 1), g4(2, S5_P), g4(2, S5_P), g4(2, 1, 1),
                  g4(2, S5_P, 256), g4(2, S5_P, 256), g4(2, S5_C, S5_P), g4(2, S5_C, S5_P),
                  g4(2, S5_P, 256), g4(2, S5_P, 256)],
        out_specs=[g4(768, 256), g4(8, S5_P)],
        out_shape=[jax.ShapeDtypeStruct((S5_G, 768, 256), BF16),
                   jax.ShapeDtypeStruct((S5_G, 8, S5_P), F32)],
        compiler_params=_params("arbitrary"),
        name="s5_weights",
    )(col(lam_re), col(lam_im), row(lam_re), row(lam_im), ldt,
      b_tiled(b_re), b_tiled(b_im), b_rowf(b_re), b_rowf(b_im), c_tiled(c_re), c_tiled(c_im))


def _s5_scan(d_scr, r0, a_re, a_im, init_re, init_im, hin_scr, n_chunks):
    lane = lax.broadcasted_iota(jnp.int32, (S5_SEQ, LANES), 1)
    fwd = lane < S5_P
    hr, hi = init_re, init_im
    for c in range(n_chunks):
        cf = slice(r0 + c * S5_SEQ, r0 + (c + 1) * S5_SEQ)
        cb = slice(r0 + (n_chunks - 1 - c) * S5_SEQ, r0 + (n_chunks - c) * S5_SEQ)
        hin_scr[cf, 0:S5_P] = hr[:, 0:S5_P]
        hin_scr[cb, S5_P:LANES] = hr[:, S5_P:LANES]
        hin_scr[cf, LANES:LANES + S5_P] = hi[:, 0:S5_P]
        hin_scr[cb, LANES + S5_P:2 * LANES] = hi[:, S5_P:LANES]
        dr = jnp.where(fwd, d_scr[cf, 0:LANES], d_scr[cb, 0:LANES])
        di = jnp.where(fwd, d_scr[cf, LANES:2 * LANES], d_scr[cb, LANES:2 * LANES])
        hr, hi = hr * a_re - hi * a_im + dr, hr * a_im + hi * a_re + di
    return hr, hi


def _s5_core_body(ut_ref, w_ref, a_ref, s0_ref, yt_ref, fin_ref, ts_scr, d_scr, hin_scr):
    a_re = a_ref[0:1, :]
    a_im = a_ref[1:2, :]
    rhs = ut_ref[...].reshape(S5_L * S5_C, S5_ROWS)
    ts_scr[...] = jnp.dot(w_ref[0:512, :], rhs, preferred_element_type=F32)
    d_scr[...] = ts_scr[256:512, :].T
    zero = jnp.zeros((S5_SEQ, LANES), F32)
    for hb in range(CTX_SETS):
        fr, fi = _s5_scan(d_scr, hb * CTX_CHUNKS * S5_SEQ, a_re, a_im, zero, zero, hin_scr, CTX_CHUNKS)
        fin_ref[S5_SEQ * hb:S5_SEQ * (hb + 1), 0:LANES] = fr
        fin_ref[S5_SEQ * hb:S5_SEQ * (hb + 1), LANES:2 * LANES] = fi
    _s5_scan(d_scr, S5_CTX_ROWS, a_re, a_im, s0_ref[:, 0:LANES], s0_ref[:, LANES:2 * LANES], hin_scr, LAT_CHUNKS)
    y = ts_scr[0:256, :] + lax.dot_general(w_ref[512:768, :], hin_scr[...].astype(BF16),
                                           (((1,), (1,)), ((), ())), preferred_element_type=F32)
    yt_ref[...] = y.reshape(S5_L, S5_C, S5_ROWS)


def _s5_core(ut, wall, avec, s0):
    g3 = lambda a, b: pl.BlockSpec((None, a, b), lambda g: (g, 0, 0))
    tok = pl.BlockSpec((S5_L, S5_C, S5_ROWS), lambda g: (0, g, 0))
    return pl.pallas_call(
        _s5_core_body,
        grid=(S5_G,),
        in_specs=[tok, g3(768, 256), g3(8, LANES), g3(LAT_B, 256)],
        out_specs=[tok, g3(CTX_B, 256)],
        out_shape=[jax.ShapeDtypeStruct((S5_L, D, S5_ROWS), F32),
                   jax.ShapeDtypeStruct((S5_G, CTX_B, 256), F32)],
        scratch_shapes=[pltpu.VMEM((512, S5_ROWS), F32), pltpu.VMEM((S5_ROWS, 256), F32),
                        pltpu.VMEM((S5_ROWS, 256), F32)],
        compiler_params=_params("arbitrary"),
        name="s5_core",
    )(ut, wall, avec, s0)


def _gelu_tanh(x):
    return 0.5 * x * (1.0 + jnp.tanh(math.sqrt(2.0 / math.pi) * (x + 0.044715 * (x * x * x))))


def _s5_out_body(hc_ref, hl_ref, y_ref, mod_ref, g_ref, dskip_ref, w_ref, oc_ref, ol_ref, w_scr):
    n = pl.program_id(0)
    is_lat = n >= S5_CTX_STEPS

    @pl.when(n == 0)
    def _():
        w_scr[...] = w_ref[...].astype(BF16)

    mod = mod_ref[...]
    gain = g_ref[...]
    dskip = dskip_ref[...]
    hs, gs = [], []
    for c in range(S5_CB):
        h = jnp.where(is_lat, hl_ref[:, c], hc_ref[:, c])
        u = _norm_mod(h, gain, mod[:, :, 0:D], mod[:, :, D:2 * D])
        y = u * dskip + y_ref[c]
        hs.append(h)
        gs.append(_gelu_tanh(y).astype(BF16).reshape(S5_SEQ * S5_L, D))
    hh = jnp.dot(jnp.concatenate(gs, axis=0), w_scr[...], preferred_element_type=F32)
    rows = S5_SEQ * S5_L
    for c in range(S5_CB):
        blk = hh[rows * c:rows * (c + 1)]
        mix = (blk[:, 0:D] * jax.nn.sigmoid(blk[:, D:2 * D])).reshape(S5_SEQ, S5_L, D)
        out = hs[c] + mod[:, :, 2 * D:3 * D] * mix

        @pl.when(is_lat)
        def _():
            ol_ref[:, c] = out

        @pl.when(jnp.logical_not(is_lat))
        def _():
            oc_ref[:, c] = out


def _s5_out(h, y, mod8, gain, d_skip, w_glu):
    blk, ctx_idx, lat_idx = _s5_tile_specs()
    h_ctx, h_lat = pl.pallas_call(
        _s5_out_body,
        grid=(S5_STEPS,),
        in_specs=[pl.BlockSpec(blk, ctx_idx),
                  pl.BlockSpec(blk, lambda n: (N_CTX // N_LAT, lat_idx(n), 0, 0)),
                  pl.BlockSpec((None, S5_CB, S5_SEQ, S5_L, D), lambda n: (n, 0, 0, 0, 0)),
                  pl.BlockSpec((None, S5_SEQ, 1, 6 * D), lambda n: (jnp.where(n >= S5_CTX_STEPS, 1, 0), 0, 0, 0)),
                  pl.BlockSpec((1, D), lambda n: (0, 0)),
                  pl.BlockSpec((1, D), lambda n: (0, 0)),
                  pl.BlockSpec((D, 2 * D), lambda n: (0, 0))],
        out_specs=[pl.BlockSpec(blk, ctx_idx),
                   pl.BlockSpec(blk, lambda n: (0, lat_idx(n), 0, 0))],
        out_shape=[jax.ShapeDtypeStruct((CTX_B, CTX_CHUNKS, S5_L, D), F32),
                   jax.ShapeDtypeStruct((LAT_B, LAT_CHUNKS, S5_L, D), F32)],
        scratch_shapes=[pltpu.VMEM((D, 2 * D), BF16)],
        compiler_params=_params("arbitrary"),
        name="s5_out",
    )(h.reshape(N_TOK // CTX_S, CTX_CHUNKS, S5_L, D), h.reshape(N_TOK // LAT_S, LAT_CHUNKS, S5_L, D),
      y, mod8, gain.reshape(1, D), d_skip.reshape(1, D), w_glu)
    return h_ctx.reshape(N_CTX, D), h_lat.reshape(N_LAT, D)


def _s5_mixer(h, mods, layer, gain, state, lam_re, lam_im, log_dt, b_re, b_im, c_re, c_im, d_skip, w_glu):
    mod8 = _s5_mod8(mods, layer)
    u = _s5_in(h, mod8, gain)
    ut = u.reshape(S5_ROWS, S5_L, D).transpose(1, 2, 0)
    wall, avec = _s5_weights(lam_re, lam_im, log_dt, b_re, b_im, c_re, c_im)
    a2 = jnp.stack([jnp.concatenate([avec[:, 0], avec[:, 2]], axis=-1),
                    jnp.concatenate([avec[:, 1], avec[:, 3]], axis=-1)], axis=1)
    a2 = jnp.pad(a2, ((0, 0), (0, 6), (0, 0)))
    s0 = state.transpose(3, 0, 2, 1, 4).reshape(S5_G, LAT_B, 4 * S5_P)
    yt, fin = _s5_core(ut, wall, a2, s0)
    y = yt.transpose(2, 0, 1).reshape(S5_STEPS, S5_CB, S5_SEQ, S5_L, D)
    new_state = fin.reshape(S5_G, CTX_B, 2, 2, S5_P).transpose(1, 3, 2, 0, 4)
    return _s5_out(h, y, mod8, gain, d_skip, w_glu), new_state


def kernel(x_prompt, x_sample, cache_l0_k, cache_l0_v, state_l1, cache_l2_k, cache_l2_v, cache_l3_k, cache_l3_v, c, c_ctx, norm1, norm2, w_mod, b_mod, w_up, b_up, conv_k, conv_b, w_down, l0_w_qkv, l0_q_norm, l0_k_norm, l0_sink, l0_w_o, l1_lam_re, l1_lam_im, l1_log_dt, l1_b_re, l1_b_im, l1_c_re, l1_c_im, l1_d_skip, l1_w_glu, l2_w_qkv, l2_q_norm, l2_k_norm, l2_w_o, l3_w_qkv, l3_q_norm, l3_k_norm, l3_sink, l3_w_o):
    h = (x_prompt.reshape(N_CTX, D), x_sample.reshape(N_LAT, D))
    cond = jnp.concatenate([c_ctx[None, :], jnp.zeros((LAT_ROW0 - 1, D), F32), c], axis=0)
    mods = _modulation(cond, w_mod, b_mod)

    attn_layers = {
        0: (l0_w_qkv, l0_q_norm, l0_k_norm, l0_sink, l0_w_o, cache_l0_k, cache_l0_v, 16, 4, 64),
        2: (l2_w_qkv, l2_q_norm, l2_k_norm, None, l2_w_o, cache_l2_k, cache_l2_v, 8, 4, 128),
        3: (l3_w_qkv, l3_q_norm, l3_k_norm, l3_sink, l3_w_o, cache_l3_k, cache_l3_v, 16, 4, 64),
    }
    new_kv = {}
    new_state = None
    for layer in range(N_LAYERS):
        if layer in attn_layers:
            w_qkv, q_norm, k_norm, sink, w_o, ck, cv, n_heads, n_kv, dh = attn_layers[layer]
            q, k_ctx, k_lat, v_ctx, v_lat = _qkv_proj(h, mods, layer, norm1[layer], w_qkv, q_norm, k_norm,
                                                      n_heads, n_kv, dh)
            new_kv[layer] = (k_ctx.reshape(CTX_B, CTX_S, n_kv, dh), v_ctx.reshape(CTX_B, CTX_S, n_kv, dh))
            o_ctx, o_lat = _attention(q, k_ctx, k_lat, v_ctx, v_lat, ck, cv, sink, layer)
            h = _out_proj(o_ctx, o_lat, h, mods, layer, w_o)
        else:
            h, new_state = _s5_mixer(h, mods, layer, norm1[layer], state_l1, l1_lam_re, l1_lam_im, l1_log_dt,
                                     l1_b_re, l1_b_im, l1_c_re, l1_c_im, l1_d_skip, l1_w_glu)
        h = _ffn(h, mods, layer, norm2[layer], w_up, b_up, conv_k, conv_b, w_down,
                 split_out=layer == N_LAYERS - 1)

    y_prompt = h[0].reshape(CTX_B, CTX_S, D)
    y_sample = h[1].reshape(LAT_B, LAT_S, D)
    return (y_prompt, y_sample, new_kv[0][0], new_kv[0][1], new_state,
            new_kv[2][0], new_kv[2][1], new_kv[3][0], new_kv[3][1])
```

```python
import functools
import math

import jax
import jax.numpy as jnp
import numpy as np
from jax import lax
from jax.experimental import pallas as pl
from jax.experimental.pallas import tpu as pltpu

F32 = jnp.float32
BF16 = jnp.bfloat16

D = 1024
N_LAYERS = 4
CTX_B, CTX_S = 32, 256
LAT_B, LAT_S = 8, 1024
PAST = 512
N_CTX = CTX_B * CTX_S
N_LAT = LAT_B * LAT_S
N_TOK = N_CTX + N_LAT
GRID_W = 64
WINDOW = 128
ROPE_THETA = 10000.0
EPS = 1e-6
D_FF = 2816
N_COND = 16
LAT_ROW0 = 8

S5_G = 64
S5_C = 16
S5_P = 64
S5_L = 16

VMEM_LIMIT = 56 * 1024 * 1024
LANES = 128
NEG_BIG = -1e30
LOG2E = math.log2(math.e)

TM = 512
TM_FFN = 1024
TF = 256
TQ = 256
ATT_RB = 32
ATT_KEYS = 1024
STAGE_RB = 128
QKV_RB = 64


def _params(*sem):
    return pltpu.CompilerParams(dimension_semantics=sem, vmem_limit_bytes=VMEM_LIMIT)


def _cond_row(i, tm):
    n_ctx_tiles = N_CTX // tm
    return jnp.where(i < n_ctx_tiles, 0, LAT_ROW0 + (i - n_ctx_tiles) // (LAT_S // tm))


def _norm_mod(x, gain, shift, scale):
    ms = jnp.mean(x * x, axis=-1, keepdims=True)
    return (x * lax.rsqrt(ms + EPS) * gain) * (1.0 + scale) + shift


def _silu(x):
    return x * jax.nn.sigmoid(x)


def _mod_body(cond_ref, w_ref, b_ref, o_ref):
    s = _silu(cond_ref[...]).astype(BF16)
    o_ref[...] = jnp.dot(s, w_ref[...].astype(BF16), preferred_element_type=F32) + b_ref[...]


def _modulation(cond, w_mod, b_mod):
    tn = 1536
    return pl.pallas_call(
        _mod_body,
        grid=(N_LAYERS, 6 * D // tn),
        in_specs=[
            pl.BlockSpec((N_COND, D), lambda l, n: (0, 0)),
            pl.BlockSpec((None, D, tn), lambda l, n: (l, 0, n)),
            pl.BlockSpec((None, 1, tn), lambda l, n: (l, 0, n)),
        ],
        out_specs=pl.BlockSpec((None, N_COND, tn), lambda l, n: (l, 0, n)),
        out_shape=jax.ShapeDtypeStruct((N_LAYERS, N_COND, 6 * D), F32),
        compiler_params=_params("arbitrary", "arbitrary"),
        name="modulation",
    )(cond, w_mod, b_mod.reshape(N_LAYERS, 1, 6 * D))


def _mod_spec(layer, tm):
    return pl.BlockSpec((None, None, 1, 6 * D), lambda i, *_: (layer, _cond_row(i, tm), 0, 0))


def _token_pair(h, tm):
    n_ctx_tiles = N_CTX // tm
    h_ctx, h_lat, lat_off = (h[0], h[1], 0) if isinstance(h, tuple) else (h, h, n_ctx_tiles)
    specs = [pl.BlockSpec((tm, D), lambda i, *_: (jnp.minimum(i, n_ctx_tiles - 1), 0)),
             pl.BlockSpec((tm, D), lambda i, *_: (jnp.maximum(i - n_ctx_tiles, 0) + lat_off, 0))]
    return (h_ctx, h_lat), specs


def _split_specs(tm, width):
    n_ctx_tiles = N_CTX // tm
    return [pl.BlockSpec((tm, width), lambda i, *_: (jnp.minimum(i, n_ctx_tiles - 1), 0)),
            pl.BlockSpec((tm, width), lambda i, *_: (jnp.maximum(i - n_ctx_tiles, 0), 0))]


def _qkv_body(hc_ref, hl_ref, mod_ref, g_ref, w_ref, qn_ref, kn_ref, bd_ref, cos_ref, sin_ref,
              q_ref, kc_ref, kl_ref, vc_ref, vl_ref, w_scr, xn_scr, qkv_scr, sq_scr, ss_scr, *, nq, nk, dh):
    is_lat = pl.program_id(0) >= N_CTX // TM

    @pl.when(pl.program_id(0) == 0)
    def _():
        w_scr[...] = w_ref[...].astype(BF16)

    mod = mod_ref[...]
    row_blocks = [slice(r, r + QKV_RB) for r in range(0, TM, QKV_RB)]
    for rb in row_blocks:
        h = jnp.where(is_lat, hl_ref[rb, :], hc_ref[rb, :])
        xn_scr[rb, :] = _norm_mod(h, g_ref[...], mod[:, 0:D], mod[:, D:2 * D]).astype(BF16)
    qkv_scr[...] = jnp.dot(xn_scr[...], w_scr[...], preferred_element_type=F32)

    bd = bd_ref[...]
    quarter = dh // 4
    lane = lax.broadcasted_iota(jnp.int32, (QKV_RB, LANES), 1)
    first = (lane % (2 * quarter)) < quarter
    inv_dh = 1.0 / dh

    def head_norm_rope(c0, gain, out_ref, o0, rope):
        for rb in row_blocks:
            z = qkv_scr[rb, c0:c0 + 256]
            sq_scr[rb, :] = (z * z).astype(BF16)
        ss_scr[...] = jnp.dot(sq_scr[...], bd, preferred_element_type=F32)
        for rb in row_blocks:
            zn = qkv_scr[rb, c0:c0 + 256] * lax.rsqrt(ss_scr[rb, :] * inv_dh + EPS)
            for j in range(2):
                zz = zn[:, LANES * j:LANES * (j + 1)] * gain
                if rope:
                    partner = jnp.where(first, pltpu.roll(zz, LANES - quarter, 1), pltpu.roll(zz, quarter, 1))
                    zz = zz * cos_ref[rb, :] + partner * sin_ref[rb, :]
                out_ref[rb, o0 + LANES * j:o0 + LANES * (j + 1)] = zz.astype(out_ref.dtype)

    qgain = qn_ref[...] * (dh ** -0.5 * LOG2E)
    kgain = kn_ref[...]

    def finish(k_ref, v_ref, rope):
        for c in range(nq // 256):
            head_norm_rope(256 * c, qgain, q_ref, 256 * c, rope)
        for c in range(nk // 256):
            head_norm_rope(nq + 256 * c, kgain, k_ref, 256 * c, rope)
        for rb in row_blocks:
            v_ref[rb, :] = qkv_scr[rb, nq + nk:]

    pl.when(is_lat)(functools.partial(finish, kl_ref, vl_ref, True))
    pl.when(jnp.logical_not(is_lat))(functools.partial(finish, kc_ref, vc_ref, False))


def _rope_tables(dh):
    half, quarter = dh // 2, dh // 4
    freqs = 1.0 / (ROPE_THETA ** (np.arange(quarter, dtype=np.float32) / quarter))
    pos = np.arange(LAT_S)
    row = (pos // GRID_W).astype(np.float32)
    col = (pos % GRID_W).astype(np.float32)
    ang_r = (row[:, None] * freqs[None, :]).astype(np.float32)
    ang_c = (col[:, None] * freqs[None, :]).astype(np.float32)
    cos = np.concatenate([np.cos(ang_r), np.cos(ang_r), np.cos(ang_c), np.cos(ang_c)], axis=1)
    sin = np.concatenate([-np.sin(ang_r), np.sin(ang_r), -np.sin(ang_c), np.sin(ang_c)], axis=1)
    reps = LANES // dh
    cos = np.tile(cos.astype(np.float32), (1, reps))
    sin = np.tile(sin.astype(np.float32), (1, reps))
    return jnp.asarray(cos), jnp.asarray(sin)


def _block_diag_ones(dh):
    idx = np.arange(256) // dh
    return jnp.asarray((idx[:, None] == idx[None, :]).astype(np.float32), dtype=BF16)


def _qkv_proj(h, mods, layer, gain, w_qkv, q_norm, k_norm, n_heads, n_kv, dh):
    nq, nk = n_heads * dh, n_kv * dh
    nqkv = nq + 2 * nk
    cos, sin = _rope_tables(dh)
    reps = LANES // dh
    n_ctx_tiles = N_CTX // TM
    lat_tiles = LAT_S // TM

    def rope_idx(i):
        return (jnp.where(i < n_ctx_tiles, 0, (i - n_ctx_tiles) % lat_tiles), 0)

    h_ops, h_specs = _token_pair(h, TM)
    kv_specs = _split_specs(TM, nk)
    kv_shapes = [jax.ShapeDtypeStruct((N_CTX, nk), F32), jax.ShapeDtypeStruct((N_LAT, nk), F32)]
    return pl.pallas_call(
        functools.partial(_qkv_body, nq=nq, nk=nk, dh=dh),
        grid=(N_TOK // TM,),
        in_specs=h_specs + [
            _mod_spec(layer, TM),
            pl.BlockSpec((1, D), lambda i: (0, 0)),
            pl.BlockSpec((D, nqkv), lambda i: (0, 0)),
            pl.BlockSpec((1, LANES), lambda i: (0, 0)),
            pl.BlockSpec((1, LANES), lambda i: (0, 0)),
            pl.BlockSpec((256, 256), lambda i: (0, 0)),
            pl.BlockSpec((TM, LANES), rope_idx),
            pl.BlockSpec((TM, LANES), rope_idx),
        ],
        out_specs=[pl.BlockSpec((TM, nq), lambda i: (i, 0))] + kv_specs + kv_specs,
        out_shape=[jax.ShapeDtypeStruct((N_TOK, nq), BF16)] + kv_shapes + kv_shapes,
        scratch_shapes=[pltpu.VMEM((D, nqkv), BF16), pltpu.VMEM((TM, D), BF16), pltpu.VMEM((TM, nqkv), F32),
                        pltpu.VMEM((TM, 256), BF16),
                        pltpu.VMEM((TM, 256), F32)],
        compiler_params=_params("arbitrary"),
        name=f"qkv_l{layer}",
    )(*h_ops, mods.reshape(N_LAYERS, N_COND, 1, 6 * D), gain.reshape(1, D), w_qkv,
      jnp.tile(q_norm, reps).reshape(1, LANES), jnp.tile(k_norm, reps).reshape(1, LANES),
      _block_diag_ones(dh), cos, sin)


def _attend(q, segs, sink_of, tq, scr):
    s_scr, p_scr, r_scr = scr
    m_rows = q.shape[0]
    cols = []
    col = 0
    for k, _, _ in segs:
        t = k.shape[0]
        s_scr[0:m_rows, col:col + t] = lax.dot_general(q, k, (((1,), (1,)), ((), ())), preferred_element_type=F32)
        cols.append((col, t))
        col += t
    rows = ATT_RB * max(1, ATT_KEYS // col)
    for r0 in range(0, m_rows, rows):
        rb = slice(r0, r0 + rows)
        parts = []
        for (c0, t), (_, _, bias) in zip(cols, segs):
            s = s_scr[rb, c0:c0 + t]
            if bias is not None:
                s = s + bias[r0 % tq:r0 % tq + rows, :]
            parts.append(s)
        m = functools.reduce(jnp.maximum, [jnp.max(s, axis=-1, keepdims=True) for s in parts])
        sink = sink_of(r0)
        if sink is not None:
            m = jnp.maximum(m, sink)
        den = None if sink is None else jnp.exp2(sink - m)
        for (c0, t), s in zip(cols, parts):
            p = jnp.exp2(s - m)
            ps = jnp.sum(p, axis=-1, keepdims=True)
            den = ps if den is None else den + ps
            p_scr[rb, c0:c0 + t] = p.astype(BF16)
        r_scr[rb, :] = jnp.broadcast_to(1.0 / den, (rows, LANES))
    acc = None
    for (c0, t), (_, v, _) in zip(cols, segs):
        pv = jnp.dot(p_scr[0:m_rows, c0:c0 + t], v, preferred_element_type=F32)
        acc = pv if acc is None else acc + pv
    return acc * r_scr[0:m_rows, :]


def _dup_halves(x, kv):
    lane = lax.broadcasted_iota(jnp.int32, x.shape, 1)
    r = pltpu.roll(x, 64, 1)
    lo = lane < 64
    return (jnp.where(lo, x, r) if kv % 2 == 0 else jnp.where(lo, r, x)).astype(BF16)


def _dup_segment(k, v, kv):
    c = LANES * (kv // 2)
    return _dup_halves(k[:, c:c + LANES], kv), _dup_halves(v[:, c:c + LANES], kv)


def _stage_rows(dst_ref, rows, fn):
    for kv in range(4):
        for r in range(0, rows, STAGE_RB):
            dst_ref[kv, r:r + STAGE_RB, :] = fn(slice(r, r + STAGE_RB), kv)


def _attend_heads64(q_ref, o_ref, sink_ref, segs_of, biases, tq, scr):
    lane = lax.broadcasted_iota(jnp.int32, (tq, LANES), 1)
    lo = lane < 64
    for kv in range(4):
        segs = [(k, v, b) for (k, v), b in zip(segs_of(kv), biases)]
        parts = []
        for pair in range(2):
            j = 2 * kv + pair
            qp = q_ref[:, LANES * j:LANES * (j + 1)]
            zero = jnp.zeros_like(qp)
            parts += [jnp.where(lo, qp, zero), jnp.where(lo, zero, qp)]
        sinks = [sink_ref[4 * kv + r] * LOG2E for r in range(4)]
        out = _attend(jnp.concatenate(parts, axis=0), segs, lambda r0: sinks[r0 // tq], tq, scr)
        for pair in range(2):
            j = 2 * kv + pair
            a = out[(2 * pair) * tq:(2 * pair + 1) * tq]
            b = out[(2 * pair + 1) * tq:(2 * pair + 2) * tq]
            o_ref[:, LANES * j:LANES * (j + 1)] = jnp.where(lo, a, b).astype(BF16)


def _attend_heads128(q_ref, o_ref, segs_of, tq, scr):
    for kv in range(4):
        segs = [(k, v, None) for k, v in segs_of(kv)]
        q = jnp.concatenate([q_ref[:, LANES * (2 * kv):LANES * (2 * kv + 1)],
                             q_ref[:, LANES * (2 * kv + 1):LANES * (2 * kv + 2)]], axis=0)
        out = _attend(q, segs, lambda r0: None, tq, scr)
        o_ref[:, LANES * (2 * kv):LANES * (2 * kv + 1)] = out[:tq].astype(BF16)
        o_ref[:, LANES * (2 * kv + 1):LANES * (2 * kv + 2)] = out[tq:].astype(BF16)


def _ctx_attn_a_body(sink_ref, q_ref, k_ref, v_ref, o_ref, *scr):
    _attend_heads64(q_ref, o_ref, sink_ref, lambda kv: [_dup_segment(k_ref[...], v_ref[...], kv)], [None],
                    CTX_S, scr)


def _cols128(ref, kv):
    return ref[:, LANES * kv:LANES * (kv + 1)].astype(BF16)


def _ctx_attn_c_body(q_ref, k_ref, v_ref, o_ref, *scr):
    _attend_heads128(q_ref, o_ref, lambda kv: [(_cols128(k_ref, kv), _cols128(v_ref, kv))], CTX_S, scr)


def _lat_attn_a_body(sink_ref, q_ref, k_ref, v_ref, ck_ref, cv_ref, o_ref, ck_scr, cv_scr, k_scr, v_scr,
                     bias_scr, *scr):
    qi = pl.program_id(1)

    @pl.when(qi == 0)
    def _():
        for src, dst, rows in ((ck_ref, ck_scr, PAST), (cv_ref, cv_scr, PAST),
                               (k_ref, k_scr, LAT_S), (v_ref, v_scr, LAT_S)):
            _stage_rows(dst, rows, lambda rb, kv, src=src: _dup_halves(
                src[rb, LANES * (kv // 2):LANES * (kv // 2 + 1)], kv))

    band = TQ + 2 * WINDOW
    ws = pl.multiple_of(jnp.clip(qi * TQ - WINDOW, 0, LAT_S - band), WINDOW)
    qpos = qi * TQ + lax.broadcasted_iota(jnp.int32, (TQ, band), 0)
    kpos = ws + lax.broadcasted_iota(jnp.int32, (TQ, band), 1)
    bias_scr[...] = jnp.where(jnp.abs(qpos - kpos) <= WINDOW, 0.0, NEG_BIG)
    segs_of = lambda kv: [(ck_scr[kv], cv_scr[kv]), (k_scr[kv, pl.ds(ws, band), :], v_scr[kv, pl.ds(ws, band), :])]
    _attend_heads64(q_ref, o_ref, sink_ref, segs_of, [None, bias_scr], TQ, scr)


def _lat_attn_c_body(q_ref, k_ref, v_ref, ck_ref, cv_ref, o_ref, ck_scr, cv_scr, k_scr, v_scr, *scr):
    @pl.when(pl.program_id(1) == 0)
    def _():
        for src, dst, rows in ((ck_ref, ck_scr, PAST), (cv_ref, cv_scr, PAST),
                               (k_ref, k_scr, LAT_S), (v_ref, v_scr, LAT_S)):
            _stage_rows(dst, rows, lambda rb, kv, src=src: src[rb, LANES * kv:LANES * (kv + 1)].astype(BF16))

    _attend_heads128(q_ref, o_ref, lambda kv: [(ck_scr[kv], cv_scr[kv]), (k_scr[kv], v_scr[kv])], TQ, scr)


_SMEM_SPEC = pl.BlockSpec(memory_space=pltpu.SMEM)


def _attention(q, k_ctx, k_lat, v_ctx, v_lat, cache_k, cache_v, sink, layer):
    nk = k_ctx.shape[1]
    ck = cache_k.reshape(LAT_B, PAST, nk)
    cv = cache_v.reshape(LAT_B, PAST, nk)
    ctx_specs = [
        pl.BlockSpec((CTX_S, D), lambda b: (b, 0)),
        pl.BlockSpec((CTX_S, nk), lambda b: (b, 0)),
        pl.BlockSpec((CTX_S, nk), lambda b: (b, 0)),
    ]
    qb = LAT_S // TQ
    lat_specs = [
        pl.BlockSpec((TQ, D), lambda b, i: (N_CTX // TQ + b * qb + i, 0)),
        pl.BlockSpec((LAT_S, nk), lambda b, i: (b, 0)),
        pl.BlockSpec((LAT_S, nk), lambda b, i: (b, 0)),
        pl.BlockSpec((None, PAST, nk), lambda b, i: (b, 0, 0)),
        pl.BlockSpec((None, PAST, nk), lambda b, i: (b, 0, 0)),
    ]
    heads_per_kv = D // nk
    band = TQ + 2 * WINDOW

    def softmax_scratch(m_rows, keys):
        return [pltpu.VMEM((m_rows, keys), F32), pltpu.VMEM((m_rows, keys), BF16), pltpu.VMEM((m_rows, LANES), F32)]

    lat_keys = PAST + (band if sink is not None else LAT_S)
    staged = [pltpu.VMEM((4, PAST, LANES), BF16), pltpu.VMEM((4, PAST, LANES), BF16),
              pltpu.VMEM((4, LAT_S, LANES), BF16), pltpu.VMEM((4, LAT_S, LANES), BF16)]
    if sink is not None:
        staged.append(pltpu.VMEM((TQ, band), F32))
    ctx_out = dict(out_specs=pl.BlockSpec((CTX_S, D), lambda b: (b, 0)),
                   out_shape=jax.ShapeDtypeStruct((N_CTX, D), BF16),
                   scratch_shapes=softmax_scratch(heads_per_kv * CTX_S, CTX_S),
                   grid=(CTX_B,), compiler_params=_params("arbitrary"))
    lat_out = dict(out_specs=pl.BlockSpec((TQ, D), lambda b, i: (b * qb + i, 0)),
                   out_shape=jax.ShapeDtypeStruct((N_LAT, D), BF16),
                   scratch_shapes=staged + softmax_scratch(heads_per_kv * TQ, lat_keys),
                   grid=(LAT_B, qb), compiler_params=_params("arbitrary", "arbitrary"))
    if sink is not None:
        o_ctx = pl.pallas_call(_ctx_attn_a_body, in_specs=[_SMEM_SPEC] + ctx_specs,
                               name=f"attn_ctx_l{layer}", **ctx_out)(sink, q, k_ctx, v_ctx)
        o_lat = pl.pallas_call(_lat_attn_a_body, in_specs=[_SMEM_SPEC] + lat_specs,
                               name=f"attn_lat_l{layer}", **lat_out)(sink, q, k_lat, v_lat, ck, cv)
    else:
        o_ctx = pl.pallas_call(_ctx_attn_c_body, in_specs=ctx_specs,
                               name=f"attn_ctx_l{layer}", **ctx_out)(q, k_ctx, v_ctx)
        o_lat = pl.pallas_call(_lat_attn_c_body, in_specs=lat_specs,
                               name=f"attn_lat_l{layer}", **lat_out)(q, k_lat, v_lat, ck, cv)
    return o_ctx, o_lat


def _oproj_body(oc_ref, ol_ref, hc_ref, hl_ref, mod_ref, w_ref, out_ref, w_scr):
    @pl.when(pl.program_id(0) == 0)
    def _():
        w_scr[...] = w_ref[...].astype(BF16)

    is_ctx = pl.program_id(0) < N_CTX // TM
    o = jnp.where(is_ctx, oc_ref[...], ol_ref[...])
    h = jnp.where(is_ctx, hc_ref[...], hl_ref[...])
    mix = jnp.dot(o, w_scr[...], preferred_element_type=F32)
    out_ref[...] = h + mod_ref[:, 2 * D:3 * D] * mix


def _out_proj(o_ctx, o_lat, h, mods, layer, w_o):
    (o_ops, o_specs), (h_ops, h_specs) = _token_pair((o_ctx, o_lat), TM), _token_pair(h, TM)
    return pl.pallas_call(
        _oproj_body,
        grid=(N_TOK // TM,),
        in_specs=o_specs + h_specs + [_mod_spec(layer, TM), pl.BlockSpec((D, D), lambda i: (0, 0))],
        out_specs=pl.BlockSpec((TM, D), lambda i: (i, 0)),
        out_shape=jax.ShapeDtypeStruct((N_TOK, D), F32),
        scratch_shapes=[pltpu.VMEM((D, D), BF16)],
        compiler_params=_params("arbitrary"),
        name=f"oproj_l{layer}",
    )(*o_ops, *h_ops, mods.reshape(N_LAYERS, N_COND, 1, 6 * D), w_o)


FFN_CHUNK = CTX_S
FFN_SLOT = FFN_CHUNK + 16
FFN_NCHUNK = TM_FFN // FFN_CHUNK


def _ffn_body(*refs, n_h, n_out):
    h_refs, refs = refs[:n_h], refs[n_h:]
    (mod_ref, g_ref, wg_ref, wv_ref, bg_ref, bv_ref, kg_ref, kv_ref, cg_ref, cv_ref, wd_ref) = refs[:11]
    out_refs, refs = refs[11:11 + n_out], refs[11 + n_out:]
    xn_scr, acc_scr, ug_scr, uv_scr, wg_scr, wv_scr, wd_scr = refs
    i = pl.program_id(0)
    f = pl.program_id(1)
    is_lat = i >= N_CTX // TM_FFN

    def load_h():
        if n_h == 1:
            return h_refs[0][...]
        return jnp.where(is_lat, h_refs[1][...], h_refs[0][...])

    @pl.when(f == 0)
    def _():
        mod = mod_ref[...]
        xn_scr[...] = _norm_mod(load_h(), g_ref[...], mod[:, 3 * D:4 * D], mod[:, 4 * D:5 * D]).astype(BF16)
        acc_scr[...] = jnp.zeros_like(acc_scr)

    wg_scr[...] = wg_ref[...].astype(BF16)
    wv_scr[...] = wv_ref[...].astype(BF16)
    wd_scr[...] = wd_ref[...].astype(BF16)
    base = [8 + FFN_SLOT * k for k in range(FFN_NCHUNK)]

    for k in range(FFN_NCHUNK):
        xk = xn_scr[FFN_CHUNK * k:FFN_CHUNK * (k + 1), :]
        ug_scr[base[k]:base[k] + FFN_CHUNK, :] = jnp.dot(xk, wg_scr[...], preferred_element_type=F32)
        uv_scr[base[k]:base[k] + FFN_CHUNK, :] = jnp.dot(xk, wv_scr[...], preferred_element_type=F32)

    for scr, b_ref in ((ug_scr, bg_ref), (uv_scr, bv_ref)):
        pad = -b_ref[...]
        tops = [pad] + [jnp.where(is_lat, scr[base[k - 1] + FFN_CHUNK - 1:base[k - 1] + FFN_CHUNK, :], pad)
                        for k in range(1, FFN_NCHUNK)]
        bots = [jnp.where(is_lat, scr[base[k + 1]:base[k + 1] + 1, :], pad)
                for k in range(FFN_NCHUNK - 1)] + [pad]
        for k in range(FFN_NCHUNK):
            scr[base[k] - 1:base[k], :] = tops[k]
            scr[base[k] + FFN_CHUNK:base[k] + FFN_CHUNK + 1, :] = bots[k]

    def conv(scr, b_ref, k_ref, c_ref, k):
        kk = k_ref[...]
        const = c_ref[...] + b_ref[...] * (kk[0:1] + kk[1:2] + kk[2:3])
        lo = base[k]
        return (const + kk[0:1] * scr[lo - 1:lo - 1 + FFN_CHUNK, :] + kk[1:2] * scr[lo:lo + FFN_CHUNK, :]
                + kk[2:3] * scr[lo + 1:lo + 1 + FFN_CHUNK, :])

    for k in range(FFN_NCHUNK):
        gate = conv(ug_scr, bg_ref, kg_ref, cg_ref, k)
        val = conv(uv_scr, bv_ref, kv_ref, cv_ref, k)
        a = (_silu(gate) * val).astype(BF16)
        rows = slice(FFN_CHUNK * k, FFN_CHUNK * (k + 1))
        acc_scr[rows, :] += jnp.dot(a, wd_scr[...], preferred_element_type=F32)

    done = f == pl.num_programs(1) - 1

    def write(out_ref):
        out_ref[...] = load_h() + mod_ref[:, 5 * D:6 * D] * acc_scr[...]

    if n_out == 1:
        pl.when(done)(functools.partial(write, out_refs[0]))
    else:
        pl.when(jnp.logical_and(done, jnp.logical_not(is_lat)))(functools.partial(write, out_refs[0]))
        pl.when(jnp.logical_and(done, is_lat))(functools.partial(write, out_refs[1]))


def _ffn(h, mods, layer, gain, w_up, b_up, conv_k, conv_b, w_down, split_out=False):
    nf = D_FF // TF
    b3 = b_up.reshape(N_LAYERS, 1, 2 * D_FF)
    c3 = conv_b.reshape(N_LAYERS, 1, 2 * D_FF)
    col = lambda off: (lambda i, f: (layer, 0, off + f))
    if isinstance(h, tuple):
        h_ops, h_specs = _token_pair(h, TM_FFN)
    else:
        h_ops, h_specs = (h,), [pl.BlockSpec((TM_FFN, D), lambda i, f: (i, 0))]
    if split_out:
        out_specs = _split_specs(TM_FFN, D)
        out_shape = [jax.ShapeDtypeStruct((N_CTX, D), F32), jax.ShapeDtypeStruct((N_LAT, D), F32)]
    else:
        out_specs = pl.BlockSpec((TM_FFN, D), lambda i, f: (i, 0))
        out_shape = jax.ShapeDtypeStruct((N_TOK, D), F32)
    slot_rows = 8 + FFN_SLOT * FFN_NCHUNK
    result = pl.pallas_call(
        functools.partial(_ffn_body, n_h=len(h_ops), n_out=2 if split_out else 1),
        grid=(N_TOK // TM_FFN, nf),
        in_specs=h_specs + [
            _mod_spec(layer, TM_FFN),
            pl.BlockSpec((1, D), lambda i, f: (0, 0)),
            pl.BlockSpec((None, D, TF), col(0)),
            pl.BlockSpec((None, D, TF), col(nf)),
            pl.BlockSpec((None, 1, TF), col(0)),
            pl.BlockSpec((None, 1, TF), col(nf)),
            pl.BlockSpec((None, 3, TF), col(0)),
            pl.BlockSpec((None, 3, TF), col(nf)),
            pl.BlockSpec((None, 1, TF), col(0)),
            pl.BlockSpec((None, 1, TF), col(nf)),
            pl.BlockSpec((None, TF, D), lambda i, f: (layer, f, 0)),
        ],
        out_specs=out_specs,
        out_shape=out_shape,
        scratch_shapes=[pltpu.VMEM((TM_FFN, D), BF16), pltpu.VMEM((TM_FFN, D), F32),
                        pltpu.VMEM((slot_rows, TF), F32), pltpu.VMEM((slot_rows, TF), F32),
                        pltpu.VMEM((D, TF), BF16), pltpu.VMEM((D, TF), BF16), pltpu.VMEM((TF, D), BF16)],
        compiler_params=_params("arbitrary", "arbitrary"),
        name=f"ffn_l{layer}",
    )(*h_ops, mods.reshape(N_LAYERS, N_COND, 1, 6 * D), gain.reshape(1, D), w_up, w_up, b3, b3,
      conv_k, conv_k, c3, c3, w_down)
    return tuple(result) if split_out else result


S5_SEQ = 8
S5_CB = 4
CTX_SETS = CTX_B // S5_SEQ
CTX_CHUNKS = CTX_S // S5_L
LAT_CHUNKS = LAT_S // S5_L
S5_CTX_STEPS = CTX_SETS * CTX_CHUNKS // S5_CB
S5_LAT_STEPS = LAT_CHUNKS // S5_CB
S5_STEPS = S5_CTX_STEPS + S5_LAT_STEPS
S5_ROWS = (N_CTX + N_LAT) // S5_L
S5_CTX_ROWS = N_CTX // S5_L


def _s5_tile_specs():
    per = CTX_CHUNKS // S5_CB
    blk = (S5_SEQ, S5_CB, S5_L, D)

    def ctx_idx(n):
        m = jnp.minimum(n, S5_CTX_STEPS - 1)
        return (m // per, m % per, 0, 0)

    return blk, ctx_idx, (lambda n: (jnp.maximum(n - S5_CTX_STEPS, 0)))


def _s5_mod8(mods, layer):
    m = mods[layer]
    return jnp.stack([jnp.broadcast_to(m[0:1], (S5_SEQ, 6 * D)), m[LAT_ROW0:LAT_ROW0 + LAT_B]]).reshape(
        2, S5_SEQ, 1, 6 * D)


def _s5_in_body(hc_ref, hl_ref, mod_ref, g_ref, u_ref):
    is_lat = pl.program_id(0) >= S5_CTX_STEPS
    mod = mod_ref[...]
    gain = g_ref[...]
    for c in range(S5_CB):
        x = jnp.where(is_lat, hl_ref[:, c], hc_ref[:, c])
        u_ref[c] = _norm_mod(x, gain, mod[:, :, 0:D], mod[:, :, D:2 * D]).astype(BF16)


def _s5_in(h, mod8, gain):
    blk, ctx_idx, lat_idx = _s5_tile_specs()
    return pl.pallas_call(
        _s5_in_body,
        grid=(S5_STEPS,),
        in_specs=[pl.BlockSpec(blk, ctx_idx),
                  pl.BlockSpec(blk, lambda n: (N_CTX // N_LAT, lat_idx(n), 0, 0)),
                  pl.BlockSpec((None, S5_SEQ, 1, 6 * D), lambda n: (jnp.where(n >= S5_CTX_STEPS, 1, 0), 0, 0, 0)),
                  pl.BlockSpec((1, D), lambda n: (0, 0))],
        out_specs=pl.BlockSpec((None, S5_CB, S5_SEQ, S5_L, D), lambda n: (n, 0, 0, 0, 0)),
        out_shape=jax.ShapeDtypeStruct((S5_STEPS, S5_CB, S5_SEQ, S5_L, D), BF16),
        compiler_params=_params("arbitrary"),
        name="s5_in",
    )(h.reshape(N_TOK // CTX_S, CTX_CHUNKS, S5_L, D), h.reshape(N_TOK // LAT_S, LAT_CHUNKS, S5_L, D),
      mod8, gain.reshape(1, D))


def _s5_weights_body(lamc_re_ref, lamc_im_ref, lamr_re_ref, lamr_im_ref, ldt_ref,
                     bt_re_ref, bt_im_ref, btr_re_ref, btr_im_ref, ct_re_ref, ct_im_ref,
                     w_ref, a_ref):
    blk = (lax.broadcasted_iota(jnp.int32, (1, 256), 1) // S5_L).astype(F32)
    lane256 = lax.broadcasted_iota(jnp.int32, (S5_C, 256), 1)
    hi = lax.Precision.HIGHEST
    krow = []
    st_rows = {}
    w_rows = {}
    for d in range(2):
        dt = jnp.exp(ldt_ref[d])
        lr = lamc_re_ref[d]
        li = lamc_im_ref[d]

        ang = (li * dt) * blk
        ph_r, ph_i = jnp.cos(ang), jnp.sin(ang)
        mag = jnp.exp((lr * dt) * blk)
        asc_r, asc_i = mag * ph_r, mag * ph_i
        top_r = ph_r[:, S5_L * (S5_L - 1):S5_L * (S5_L - 1) + 1]
        top_i = ph_i[:, S5_L * (S5_L - 1):S5_L * (S5_L - 1) + 1]
        mag = jnp.exp((lr * dt) * (15.0 - blk))
        dsc_r = mag * (top_r * ph_r + top_i * ph_i)
        dsc_i = mag * (top_i * ph_r - top_r * ph_i)
        ar = asc_r[:, S5_L:S5_L + 1]
        ai = asc_i[:, S5_L:S5_L + 1]
        den = lr * lr + li * li
        n_re = ar - 1.0
        f_re = (n_re * lr + ai * li) / den
        f_im = (ai * lr - n_re * li) / den
        btr = jnp.tile(bt_re_ref[d], (1, S5_L))
        bti = jnp.tile(bt_im_ref[d], (1, S5_L))
        bbr = f_re * btr - f_im * bti
        bbi = f_re * bti + f_im * btr
        e0r, e0i = (asc_r, asc_i) if d == 0 else (dsc_r, dsc_i)
        e1r = e0r * ar - e0i * ai
        e1i = e0r * ai + e0i * ar
        pr, pi = (dsc_r, dsc_i) if d == 0 else (asc_r, asc_i)
        st_rows[("re", d)] = pr * bbr - pi * bbi
        st_rows[("im", d)] = pr * bbi + pi * bbr
        ctr = jnp.tile(ct_re_ref[d], (1, S5_L))
        cti = jnp.tile(ct_im_ref[d], (1, S5_L))
        k_re = ctr * e0r - cti * e0i
        k_imneg = -(ctr * e0i + cti * e0r)
        w_rows[("re", d)] = ctr * e1r - cti * e1i
        w_rows[("im", d)] = -(ctr * e1i + cti * e1r)
        lrr = lamr_re_ref[d:d + 1, :]
        lir = lamr_im_ref[d:d + 1, :]
        magr = jnp.exp(lrr * dt)
        arr = magr * jnp.cos(lir * dt)
        air = magr * jnp.sin(lir * dt)
        denr = lrr * lrr + lir * lir
        nr = arr - 1.0
        fr = (nr * lrr + air * lir) / denr
        fi = (air * lrr - nr * lir) / denr
        bbr_row = fr * btr_re_ref[d] - fi * btr_im_ref[d]
        bbi_row = fr * btr_im_ref[d] + fi * btr_re_ref[d]
        krow.append(jnp.dot(bbr_row, k_re, precision=hi, preferred_element_type=F32)
                    + jnp.dot(bbi_row, k_imneg, precision=hi, preferred_element_type=F32))
        mag16 = jnp.exp(lrr * dt * 16.0)
        a_ref[2 * d:2 * d + 1, :] = mag16 * jnp.cos(lir * dt * 16.0)
        a_ref[2 * d + 1:2 * d + 2, :] = mag16 * jnp.sin(lir * dt * 16.0)
    a_ref[4:8, :] = jnp.zeros((4, S5_P), F32)

    t_rows = []
    for s in range(S5_L):
        fwd = krow[0] if s == 0 else jnp.where(lane256 >= S5_C * s, pltpu.roll(krow[0], S5_C * s, 1), 0.0)
        bwd = krow[1] if s == S5_L - 1 else jnp.where(lane256 < S5_C * (s + 1),
                                                      pltpu.roll(krow[1], S5_C * (s + 1), 1), 0.0)
        t_rows.append(fwd + bwd)
    order = [("re", 0), ("re", 1), ("im", 0), ("im", 1)]
    w_ref[0:256, :] = jnp.concatenate(t_rows, axis=0).T.astype(BF16)
    w_ref[256:512, :] = jnp.concatenate([st_rows[o] for o in order], axis=0).astype(BF16)
    w_ref[512:768, :] = jnp.concatenate([w_rows[o] for o in order], axis=0).T.astype(BF16)


def _s5_weights(lam_re, lam_im, log_dt, b_re, b_im, c_re, c_im):
    col = lambda x: x.transpose(1, 0, 2).reshape(S5_G, 2, S5_P, 1)
    row = lambda x: x.transpose(1, 0, 2)
    ldt = log_dt.transpose(1, 0).reshape(S5_G, 2, 1, 1)
    b_colf = lambda x: x.transpose(1, 0, 2, 3)
    b_rowf = lambda x: x.transpose(1, 0, 3, 2)
    c_colf = lambda x: x.transpose(1, 0, 3, 2)
    g4 = lambda *tail: pl.BlockSpec((None,) + tail, lambda g: (g,) + (0,) * len(tail))
    return pl.pallas_call(
        _s5_weights_body,
        grid=(S5_G,),
        in_specs=[g4(2, S5_P, 1), g4(2, S5_P, 1), g4(2, S5_P), g4(2, S5_P), g4(2, 1, 1),
                  g4(2, S5_P, S5_C), g4(2, S5_P, S5_C), g4(2, S5_C, S5_P), g4(2, S5_C, S5_P),
                  g4(2, S5_P, S5_C), g4(2, S5_P, S5_C)],
        out_specs=[g4(768, 256), g4(8, S5_P)],
        out_shape=[jax.ShapeDtypeStruct((S5_G, 768, 256), BF16),
                   jax.ShapeDtypeStruct((S5_G, 8, S5_P), F32)],
        compiler_params=_params("arbitrary"),
        name="s5_weights",
    )(col(lam_re), col(lam_im), row(lam_re), row(lam_im), ldt,
      b_colf(b_re), b_colf(b_im), b_rowf(b_re), b_rowf(b_im), c_colf(c_re), c_colf(c_im))


def _s5_scan(d_scr, r0, a_re, a_im, init_re, init_im, hin_scr, n_chunks):
    lane = lax.broadcasted_iota(jnp.int32, (S5_SEQ, LANES), 1)
    fwd = lane < S5_P
    hr, hi = init_re, init_im
    for c in range(n_chunks):
        cf = slice(r0 + c * S5_SEQ, r0 + (c + 1) * S5_SEQ)
        cb = slice(r0 + (n_chunks - 1 - c) * S5_SEQ, r0 + (n_chunks - c) * S5_SEQ)
        hin_scr[cf, 0:S5_P] = hr[:, 0:S5_P]
        hin_scr[cb, S5_P:LANES] = hr[:, S5_P:LANES]
        hin_scr[cf, LANES:LANES + S5_P] = hi[:, 0:S5_P]
        hin_scr[cb, LANES + S5_P:2 * LANES] = hi[:, S5_P:LANES]
        dr = jnp.where(fwd, d_scr[cf, 0:LANES], d_scr[cb, 0:LANES])
        di = jnp.where(fwd, d_scr[cf, LANES:2 * LANES], d_scr[cb, LANES:2 * LANES])
        hr, hi = hr * a_re - hi * a_im + dr, hr * a_im + hi * a_re + di
    return hr, hi


def _s5_core_body(ut_ref, w_ref, a_ref, s0_ref, yt_ref, fin_ref, ts_scr, d_scr, hin_scr):
    a_re = a_ref[0:1, :]
    a_im = a_ref[1:2, :]
    rhs = ut_ref[...].reshape(S5_L * S5_C, S5_ROWS)
    ts_scr[...] = jnp.dot(w_ref[0:512, :], rhs, preferred_element_type=F32)
    d_scr[...] = ts_scr[256:512, :].T
    zero = jnp.zeros((S5_SEQ, LANES), F32)
    for hb in range(CTX_SETS):
        fr, fi = _s5_scan(d_scr, hb * CTX_CHUNKS * S5_SEQ, a_re, a_im, zero, zero, hin_scr, CTX_CHUNKS)
        fin_ref[S5_SEQ * hb:S5_SEQ * (hb + 1), 0:LANES] = fr
        fin_ref[S5_SEQ * hb:S5_SEQ * (hb + 1), LANES:2 * LANES] = fi
    _s5_scan(d_scr, S5_CTX_ROWS, a_re, a_im, s0_ref[:, 0:LANES], s0_ref[:, LANES:2 * LANES], hin_scr, LAT_CHUNKS)
    y = ts_scr[0:256, :] + lax.dot_general(w_ref[512:768, :], hin_scr[...].astype(BF16),
                                           (((1,), (1,)), ((), ())), preferred_element_type=F32)
    yt_ref[...] = y.reshape(S5_L, S5_C, S5_ROWS)


def _s5_core(ut, wall, avec, s0):
    g3 = lambda a, b: pl.BlockSpec((None, a, b), lambda g: (g, 0, 0))
    tok = pl.BlockSpec((S5_L, S5_C, S5_ROWS), lambda g: (0, g, 0))
    return pl.pallas_call(
        _s5_core_body,
        grid=(S5_G,),
        in_specs=[tok, g3(768, 256), g3(8, LANES), g3(LAT_B, 256)],
        out_specs=[tok, g3(CTX_B, 256)],
        out_shape=[jax.ShapeDtypeStruct((S5_L, D, S5_ROWS), F32),
                   jax.ShapeDtypeStruct((S5_G, CTX_B, 256), F32)],
        scratch_shapes=[pltpu.VMEM((512, S5_ROWS), F32), pltpu.VMEM((S5_ROWS, 256), F32),
                        pltpu.VMEM((S5_ROWS, 256), F32)],
        compiler_params=_params("arbitrary"),
        name="s5_core",
    )(ut, wall, avec, s0)


def _gelu_tanh(x):
    return 0.5 * x * (1.0 + jnp.tanh(math.sqrt(2.0 / math.pi) * (x + 0.044715 * (x * x * x))))


def _s5_out_body(hc_ref, hl_ref, y_ref, mod_ref, g_ref, dskip_ref, w_ref, oc_ref, ol_ref, w_scr):
    n = pl.program_id(0)
    is_lat = n >= S5_CTX_STEPS

    @pl.when(n == 0)
    def _():
        w_scr[...] = w_ref[...].astype(BF16)

    mod = mod_ref[...]
    gain = g_ref[...]
    dskip = dskip_ref[...]
    hs, gs = [], []
    for c in range(S5_CB):
        h = jnp.where(is_lat, hl_ref[:, c], hc_ref[:, c])
        u = _norm_mod(h, gain, mod[:, :, 0:D], mod[:, :, D:2 * D])
        y = u * dskip + y_ref[c]
        hs.append(h)
        gs.append(_gelu_tanh(y).astype(BF16).reshape(S5_SEQ * S5_L, D))
    hh = jnp.dot(jnp.concatenate(gs, axis=0), w_scr[...], preferred_element_type=F32)
    rows = S5_SEQ * S5_L
    for c in range(S5_CB):
        blk = hh[rows * c:rows * (c + 1)]
        mix = (blk[:, 0:D] * jax.nn.sigmoid(blk[:, D:2 * D])).reshape(S5_SEQ, S5_L, D)
        out = hs[c] + mod[:, :, 2 * D:3 * D] * mix

        @pl.when(is_lat)
        def _():
            ol_ref[:, c] = out

        @pl.when(jnp.logical_not(is_lat))
        def _():
            oc_ref[:, c] = out


def _s5_out(h, y, mod8, gain, d_skip, w_glu):
    blk, ctx_idx, lat_idx = _s5_tile_specs()
    h_ctx, h_lat = pl.pallas_call(
        _s5_out_body,
        grid=(S5_STEPS,),
        in_specs=[pl.BlockSpec(blk, ctx_idx),
                  pl.BlockSpec(blk, lambda n: (N_CTX // N_LAT, lat_idx(n), 0, 0)),
                  pl.BlockSpec((None, S5_CB, S5_SEQ, S5_L, D), lambda n: (n, 0, 0, 0, 0)),
                  pl.BlockSpec((None, S5_SEQ, 1, 6 * D), lambda n: (jnp.where(n >= S5_CTX_STEPS, 1, 0), 0, 0, 0)),
                  pl.BlockSpec((1, D), lambda n: (0, 0)),
                  pl.BlockSpec((1, D), lambda n: (0, 0)),
                  pl.BlockSpec((D, 2 * D), lambda n: (0, 0))],
        out_specs=[pl.BlockSpec(blk, ctx_idx),
                   pl.BlockSpec(blk, lambda n: (0, lat_idx(n), 0, 0))],
        out_shape=[jax.ShapeDtypeStruct((CTX_B, CTX_CHUNKS, S5_L, D), F32),
                   jax.ShapeDtypeStruct((LAT_B, LAT_CHUNKS, S5_L, D), F32)],
        scratch_shapes=[pltpu.VMEM((D, 2 * D), BF16)],
        compiler_params=_params("arbitrary"),
        name="s5_out",
    )(h.reshape(N_TOK // CTX_S, CTX_CHUNKS, S5_L, D), h.reshape(N_TOK // LAT_S, LAT_CHUNKS, S5_L, D),
      y, mod8, gain.reshape(1, D), d_skip.reshape(1, D), w_glu)
    return h_ctx.reshape(N_CTX, D), h_lat.reshape(N_LAT, D)


def _s5_mixer(h, mods, layer, gain, state, lam_re, lam_im, log_dt, b_re, b_im, c_re, c_im, d_skip, w_glu):
    mod8 = _s5_mod8(mods, layer)
    u = _s5_in(h, mod8, gain)
    ut = u.reshape(S5_ROWS, S5_L, D).transpose(1, 2, 0)
    wall, avec = _s5_weights(lam_re, lam_im, log_dt, b_re, b_im, c_re, c_im)
    a2 = jnp.stack([jnp.concatenate([avec[:, 0], avec[:, 2]], axis=-1),
                    jnp.concatenate([avec[:, 1], avec[:, 3]], axis=-1)], axis=1)
    a2 = jnp.pad(a2, ((0, 0), (0, 6), (0, 0)))
    s0 = state.transpose(3, 0, 2, 1, 4).reshape(S5_G, LAT_B, 4 * S5_P)
    yt, fin = _s5_core(ut, wall, a2, s0)
    y = yt.transpose(2, 0, 1).reshape(S5_STEPS, S5_CB, S5_SEQ, S5_L, D)
    new_state = fin.reshape(S5_G, CTX_B, 2, 2, S5_P).transpose(1, 3, 2, 0, 4)
    return _s5_out(h, y, mod8, gain, d_skip, w_glu), new_state


def kernel(x_prompt, x_sample, cache_l0_k, cache_l0_v, state_l1, cache_l2_k, cache_l2_v, cache_l3_k, cache_l3_v, c, c_ctx, norm1, norm2, w_mod, b_mod, w_up, b_up, conv_k, conv_b, w_down, l0_w_qkv, l0_q_norm, l0_k_norm, l0_sink, l0_w_o, l1_lam_re, l1_lam_im, l1_log_dt, l1_b_re, l1_b_im, l1_c_re, l1_c_im, l1_d_skip, l1_w_glu, l2_w_qkv, l2_q_norm, l2_k_norm, l2_w_o, l3_w_qkv, l3_q_norm, l3_k_norm, l3_sink, l3_w_o):
    h = (x_prompt.reshape(N_CTX, D), x_sample.reshape(N_LAT, D))
    cond = jnp.concatenate([c_ctx[None, :], jnp.zeros((LAT_ROW0 - 1, D), F32), c], axis=0)
    mods = _modulation(cond, w_mod, b_mod)

    attn_layers = {
        0: (l0_w_qkv, l0_q_norm, l0_k_norm, l0_sink, l0_w_o, cache_l0_k, cache_l0_v, 16, 4, 64),
        2: (l2_w_qkv, l2_q_norm, l2_k_norm, None, l2_w_o, cache_l2_k, cache_l2_v, 8, 4, 128),
        3: (l3_w_qkv, l3_q_norm, l3_k_norm, l3_sink, l3_w_o, cache_l3_k, cache_l3_v, 16, 4, 64),
    }
    new_kv = {}
    new_state = None
    for layer in range(N_LAYERS):
        if layer in attn_layers:
            w_qkv, q_norm, k_norm, sink, w_o, ck, cv, n_heads, n_kv, dh = attn_layers[layer]
            q, k_ctx, k_lat, v_ctx, v_lat = _qkv_proj(h, mods, layer, norm1[layer], w_qkv, q_norm, k_norm,
                                                      n_heads, n_kv, dh)
            new_kv[layer] = (k_ctx.reshape(CTX_B, CTX_S, n_kv, dh), v_ctx.reshape(CTX_B, CTX_S, n_kv, dh))
            o_ctx, o_lat = _attention(q, k_ctx, k_lat, v_ctx, v_lat, ck, cv, sink, layer)
            h = _out_proj(o_ctx, o_lat, h, mods, layer, w_o)
        else:
            h, new_state = _s5_mixer(h, mods, layer, norm1[layer], state_l1, l1_lam_re, l1_lam_im, l1_log_dt,
                                     l1_b_re, l1_b_im, l1_c_re, l1_c_im, l1_d_skip, l1_w_glu)
        h = _ffn(h, mods, layer, norm2[layer], w_up, b_up, conv_k, conv_b, w_down,
                 split_out=layer == N_LAYERS - 1)

    y_prompt = h[0].reshape(CTX_B, CTX_S, D)
    y_sample = h[1].reshape(LAT_B, LAT_S, D)
    return (y_prompt, y_sample, new_kv[0][0], new_kv[0][1], new_state,
            new_kv[2][0], new_kv[2][1], new_kv[3][0], new_kv[3][1])
```

```python
import functools
import math

import jax
import jax.numpy as jnp
import numpy as np
from jax import lax
from jax.experimental import pallas as pl
from jax.experimental.pallas import tpu as pltpu

F32 = jnp.float32
BF16 = jnp.bfloat16

D = 1024
N_LAYERS = 4
CTX_B, CTX_S = 32, 256
LAT_B, LAT_S = 8, 1024
PAST = 512
N_CTX = CTX_B * CTX_S
N_LAT = LAT_B * LAT_S
N_TOK = N_CTX + N_LAT
GRID_W = 64
WINDOW = 128
ROPE_THETA = 10000.0
EPS = 1e-6
D_FF = 2816
N_COND = 16
LAT_ROW0 = 8

S5_G = 64
S5_C = 16
S5_P = 64
S5_L = 16

VMEM_LIMIT = 56 * 1024 * 1024
LANES = 128
NEG_BIG = -1e30
LOG2E = math.log2(math.e)

TM = 512
TM_FFN = 1024
TF = 256
TQ = 256
ATT_RB = 32
ATT_KEYS = 1024
STAGE_RB = 128
QKV_RB = 64


def _params(*sem):
    return pltpu.CompilerParams(dimension_semantics=sem, vmem_limit_bytes=VMEM_LIMIT)


def _cond_row(i, tm):
    n_ctx_tiles = N_CTX // tm
    return jnp.where(i < n_ctx_tiles, 0, LAT_ROW0 + (i - n_ctx_tiles) // (LAT_S // tm))


def _norm_mod(x, gain, shift, scale):
    ms = jnp.mean(x * x, axis=-1, keepdims=True)
    return (x * lax.rsqrt(ms + EPS) * gain) * (1.0 + scale) + shift


def _silu(x):
    return x * jax.nn.sigmoid(x)


def _mod_body(cond_ref, w_ref, b_ref, o_ref):
    s = _silu(cond_ref[...]).astype(BF16)
    o_ref[...] = jnp.dot(s, w_ref[...].astype(BF16), preferred_element_type=F32) + b_ref[...]


def _modulation(cond, w_mod, b_mod):
    tn = 1536
    return pl.pallas_call(
        _mod_body,
        grid=(N_LAYERS, 6 * D // tn),
        in_specs=[
            pl.BlockSpec((N_COND, D), lambda l, n: (0, 0)),
            pl.BlockSpec((None, D, tn), lambda l, n: (l, 0, n)),
            pl.BlockSpec((None, 1, tn), lambda l, n: (l, 0, n)),
        ],
        out_specs=pl.BlockSpec((None, N_COND, tn), lambda l, n: (l, 0, n)),
        out_shape=jax.ShapeDtypeStruct((N_LAYERS, N_COND, 6 * D), F32),
        compiler_params=_params("arbitrary", "arbitrary"),
        name="modulation",
    )(cond, w_mod, b_mod.reshape(N_LAYERS, 1, 6 * D))


def _mod_spec(layer, tm):
    return pl.BlockSpec((None, None, 1, 6 * D), lambda i, *_: (layer, _cond_row(i, tm), 0, 0))


def _token_pair(h, tm):
    n_ctx_tiles = N_CTX // tm
    h_ctx, h_lat, lat_off = (h[0], h[1], 0) if isinstance(h, tuple) else (h, h, n_ctx_tiles)
    specs = [pl.BlockSpec((tm, D), lambda i, *_: (jnp.minimum(i, n_ctx_tiles - 1), 0)),
             pl.BlockSpec((tm, D), lambda i, *_: (jnp.maximum(i - n_ctx_tiles, 0) + lat_off, 0))]
    return (h_ctx, h_lat), specs


def _split_specs(tm, width):
    n_ctx_tiles = N_CTX // tm
    return [pl.BlockSpec((tm, width), lambda i, *_: (jnp.minimum(i, n_ctx_tiles - 1), 0)),
            pl.BlockSpec((tm, width), lambda i, *_: (jnp.maximum(i - n_ctx_tiles, 0), 0))]


def _qkv_body(hc_ref, hl_ref, mod_ref, g_ref, w_ref, qn_ref, kn_ref, bd_ref, cos_ref, sin_ref,
              q_ref, kc_ref, kl_ref, vc_ref, vl_ref, w_scr, xn_scr, qkv_scr, sq_scr, ss_scr, *, nq, nk, dh):
    is_lat = pl.program_id(0) >= N_CTX // TM

    @pl.when(pl.program_id(0) == 0)
    def _():
        w_scr[...] = w_ref[...].astype(BF16)

    mod = mod_ref[...]
    row_blocks = [slice(r, r + QKV_RB) for r in range(0, TM, QKV_RB)]
    for rb in row_blocks:
        h = jnp.where(is_lat, hl_ref[rb, :], hc_ref[rb, :])
        xn_scr[rb, :] = _norm_mod(h, g_ref[...], mod[:, 0:D], mod[:, D:2 * D]).astype(BF16)
    qkv_scr[...] = jnp.dot(xn_scr[...], w_scr[...], preferred_element_type=F32)

    bd = bd_ref[...]
    quarter = dh // 4
    lane = lax.broadcasted_iota(jnp.int32, (QKV_RB, LANES), 1)
    first = (lane % (2 * quarter)) < quarter
    inv_dh = 1.0 / dh

    def head_norm_rope(c0, gain, out_ref, o0):
        for rb in row_blocks:
            z = qkv_scr[rb, c0:c0 + 256]
            sq_scr[rb, :] = (z * z).astype(BF16)
        ss_scr[...] = jnp.dot(sq_scr[...], bd, preferred_element_type=F32)
        for rb in row_blocks:
            zn = qkv_scr[rb, c0:c0 + 256] * lax.rsqrt(ss_scr[rb, :] * inv_dh + EPS)
            for j in range(2):
                zz = zn[:, LANES * j:LANES * (j + 1)] * gain
                partner = jnp.where(first, pltpu.roll(zz, LANES - quarter, 1), pltpu.roll(zz, quarter, 1))
                zz = zz * cos_ref[rb, :] + partner * sin_ref[rb, :]
                out_ref[rb, o0 + LANES * j:o0 + LANES * (j + 1)] = zz.astype(out_ref.dtype)

    qgain = qn_ref[...] * (dh ** -0.5 * LOG2E)
    kgain = kn_ref[...]

    def finish(k_ref, v_ref):
        for c in range(nq // 256):
            head_norm_rope(256 * c, qgain, q_ref, 256 * c)
        for c in range(nk // 256):
            head_norm_rope(nq + 256 * c, kgain, k_ref, 256 * c)
        for rb in row_blocks:
            v_ref[rb, :] = qkv_scr[rb, nq + nk:]

    pl.when(is_lat)(functools.partial(finish, kl_ref, vl_ref))
    pl.when(jnp.logical_not(is_lat))(functools.partial(finish, kc_ref, vc_ref))


def _rope_tables(dh):
    half, quarter = dh // 2, dh // 4
    freqs = 1.0 / (ROPE_THETA ** (np.arange(quarter, dtype=np.float32) / quarter))
    pos = np.arange(LAT_S)
    row = (pos // GRID_W).astype(np.float32)
    col = (pos % GRID_W).astype(np.float32)
    ang_r = (row[:, None] * freqs[None, :]).astype(np.float32)
    ang_c = (col[:, None] * freqs[None, :]).astype(np.float32)
    cos = np.concatenate([np.cos(ang_r), np.cos(ang_r), np.cos(ang_c), np.cos(ang_c)], axis=1)
    sin = np.concatenate([-np.sin(ang_r), np.sin(ang_r), -np.sin(ang_c), np.sin(ang_c)], axis=1)
    reps = LANES // dh
    cos = np.tile(cos.astype(np.float32), (1, reps))
    sin = np.tile(sin.astype(np.float32), (1, reps))
    cos = np.concatenate([np.ones_like(cos), cos], axis=0)
    sin = np.concatenate([np.zeros_like(sin), sin], axis=0)
    return jnp.asarray(cos), jnp.asarray(sin)


def _block_diag_ones(dh):
    idx = np.arange(256) // dh
    return jnp.asarray((idx[:, None] == idx[None, :]).astype(np.float32), dtype=BF16)


def _qkv_proj(h, mods, layer, gain, w_qkv, q_norm, k_norm, n_heads, n_kv, dh):
    nq, nk = n_heads * dh, n_kv * dh
    nqkv = nq + 2 * nk
    cos, sin = _rope_tables(dh)
    reps = LANES // dh
    n_ctx_tiles = N_CTX // TM
    lat_tiles = LAT_S // TM

    def rope_idx(i):
        return (jnp.where(i < n_ctx_tiles, 0, lat_tiles + (i - n_ctx_tiles) % lat_tiles), 0)

    h_ops, h_specs = _token_pair(h, TM)
    kv_specs = _split_specs(TM, nk)
    kv_shapes = [jax.ShapeDtypeStruct((N_CTX, nk), F32), jax.ShapeDtypeStruct((N_LAT, nk), F32)]
    return pl.pallas_call(
        functools.partial(_qkv_body, nq=nq, nk=nk, dh=dh),
        grid=(N_TOK // TM,),
        in_specs=h_specs + [
            _mod_spec(layer, TM),
            pl.BlockSpec((1, D), lambda i: (0, 0)),
            pl.BlockSpec((D, nqkv), lambda i: (0, 0)),
            pl.BlockSpec((1, LANES), lambda i: (0, 0)),
            pl.BlockSpec((1, LANES), lambda i: (0, 0)),
            pl.BlockSpec((256, 256), lambda i: (0, 0)),
            pl.BlockSpec((TM, LANES), rope_idx),
            pl.BlockSpec((TM, LANES), rope_idx),
        ],
        out_specs=[pl.BlockSpec((TM, nq), lambda i: (i, 0))] + kv_specs + kv_specs,
        out_shape=[jax.ShapeDtypeStruct((N_TOK, nq), BF16)] + kv_shapes + kv_shapes,
        scratch_shapes=[pltpu.VMEM((D, nqkv), BF16), pltpu.VMEM((TM, D), BF16), pltpu.VMEM((TM, nqkv), F32),
                        pltpu.VMEM((TM, 256), BF16),
                        pltpu.VMEM((TM, 256), F32)],
        compiler_params=_params("arbitrary"),
        name=f"qkv_l{layer}",
    )(*h_ops, mods.reshape(N_LAYERS, N_COND, 1, 6 * D), gain.reshape(1, D), w_qkv,
      jnp.tile(q_norm, reps).reshape(1, LANES), jnp.tile(k_norm, reps).reshape(1, LANES),
      _block_diag_ones(dh), cos, sin)


def _attend(q, segs, sink_of, tq, scr):
    m_rows = q.shape[0]
    if not scr:
        assert all(bias is None for _, _, bias in segs)
        scores = [lax.dot_general(q, k, (((1,), (1,)), ((), ())), preferred_element_type=F32) for k, _, _ in segs]
        m = functools.reduce(jnp.maximum, [jnp.max(s, axis=-1, keepdims=True) for s in scores])
        head = lax.broadcasted_iota(jnp.int32, (m_rows, 1), 0) // tq
        sink = None
        if sink_of(0) is not None:
            sink = functools.reduce(lambda acc, r: jnp.where(head >= r, sink_of(r * tq), acc),
                                    range(1, m_rows // tq), jnp.full((m_rows, 1), sink_of(0), F32))
            m = jnp.maximum(m, sink)
        den = None if sink is None else jnp.exp2(sink - m)
        acc = None
        for s, (_, v, _) in zip(scores, segs):
            p = jnp.exp2(s - m)
            ps = jnp.sum(p, axis=-1, keepdims=True)
            pv = jnp.dot(p.astype(BF16), v, preferred_element_type=F32)
            den = ps if den is None else den + ps
            acc = pv if acc is None else acc + pv
        return acc / den
    s_scr, p_scr, r_scr = scr
    cols = []
    col = 0
    for k, _, _ in segs:
        t = k.shape[0]
        s_scr[0:m_rows, col:col + t] = lax.dot_general(q, k, (((1,), (1,)), ((), ())), preferred_element_type=F32)
        cols.append((col, t))
        col += t
    rows = ATT_RB * max(1, ATT_KEYS // col)
    for r0 in range(0, m_rows, rows):
        rb = slice(r0, r0 + rows)
        parts = []
        for (c0, t), (_, _, bias) in zip(cols, segs):
            s = s_scr[rb, c0:c0 + t]
            if bias is not None:
                s = s + bias[r0 % tq:r0 % tq + rows, :]
            parts.append(s)
        m = functools.reduce(jnp.maximum, [jnp.max(s, axis=-1, keepdims=True) for s in parts])
        sink = sink_of(r0)
        if sink is not None:
            m = jnp.maximum(m, sink)
        den = None if sink is None else jnp.exp2(sink - m)
        for (c0, t), s in zip(cols, parts):
            p = jnp.exp2(s - m)
            ps = jnp.sum(p, axis=-1, keepdims=True)
            den = ps if den is None else den + ps
            p_scr[rb, c0:c0 + t] = p.astype(BF16)
        r_scr[rb, :] = jnp.broadcast_to(den, (rows, LANES))
    acc = None
    for (c0, t), (_, v, _) in zip(cols, segs):
        pv = jnp.dot(p_scr[0:m_rows, c0:c0 + t], v, preferred_element_type=F32)
        acc = pv if acc is None else acc + pv
    return acc / r_scr[0:m_rows, :]


def _dup_halves(x, kv):
    lane = lax.broadcasted_iota(jnp.int32, x.shape, 1)
    r = pltpu.roll(x, 64, 1)
    lo = lane < 64
    return (jnp.where(lo, x, r) if kv % 2 == 0 else jnp.where(lo, r, x)).astype(BF16)


def _dup_segment(k, v, kv):
    c = LANES * (kv // 2)
    return _dup_halves(k[:, c:c + LANES], kv), _dup_halves(v[:, c:c + LANES], kv)


def _stage_rows(dst_ref, rows, fn):
    for kv in range(4):
        for r in range(0, rows, STAGE_RB):
            dst_ref[kv, r:r + STAGE_RB, :] = fn(slice(r, r + STAGE_RB), kv)


def _attend_heads64(q_ref, o_ref, sink_ref, segs_of, biases, tq, scr):
    lane = lax.broadcasted_iota(jnp.int32, (tq, LANES), 1)
    lo = lane < 64
    for kv in range(4):
        segs = [(k, v, b) for (k, v), b in zip(segs_of(kv), biases)]
        parts = []
        for pair in range(2):
            j = 2 * kv + pair
            qp = q_ref[:, LANES * j:LANES * (j + 1)]
            zero = jnp.zeros_like(qp)
            parts += [jnp.where(lo, qp, zero), jnp.where(lo, zero, qp)]
        sinks = [sink_ref[4 * kv + r] * LOG2E for r in range(4)]
        out = _attend(jnp.concatenate(parts, axis=0), segs, lambda r0: sinks[r0 // tq], tq, scr)
        for pair in range(2):
            j = 2 * kv + pair
            a = out[(2 * pair) * tq:(2 * pair + 1) * tq]
            b = out[(2 * pair + 1) * tq:(2 * pair + 2) * tq]
            o_ref[:, LANES * j:LANES * (j + 1)] = jnp.where(lo, a, b).astype(BF16)


def _attend_heads128(q_ref, o_ref, segs_of, tq, scr):
    for kv in range(4):
        segs = [(k, v, None) for k, v in segs_of(kv)]
        q = jnp.concatenate([q_ref[:, LANES * (2 * kv):LANES * (2 * kv + 1)],
                             q_ref[:, LANES * (2 * kv + 1):LANES * (2 * kv + 2)]], axis=0)
        out = _attend(q, segs, lambda r0: None, tq, scr)
        o_ref[:, LANES * (2 * kv):LANES * (2 * kv + 1)] = out[:tq].astype(BF16)
        o_ref[:, LANES * (2 * kv + 1):LANES * (2 * kv + 2)] = out[tq:].astype(BF16)


def _ctx_attn_a_body(sink_ref, q_ref, k_ref, v_ref, o_ref, *scr):
    _attend_heads64(q_ref, o_ref, sink_ref, lambda kv: [_dup_segment(k_ref[...], v_ref[...], kv)], [None],
                    CTX_S, scr)


def _cols128(ref, kv):
    return ref[:, LANES * kv:LANES * (kv + 1)].astype(BF16)


def _ctx_attn_c_body(q_ref, k_ref, v_ref, o_ref, *scr):
    _attend_heads128(q_ref, o_ref, lambda kv: [(_cols128(k_ref, kv), _cols128(v_ref, kv))], CTX_S, scr)


def _lat_attn_a_body(sink_ref, q_ref, k_ref, v_ref, ck_ref, cv_ref, o_ref, ck_scr, cv_scr, k_scr, v_scr,
                     bias_scr, *scr):
    qi = pl.program_id(1)

    @pl.when(qi == 0)
    def _():
        for src, dst, rows in ((ck_ref, ck_scr, PAST), (cv_ref, cv_scr, PAST),
                               (k_ref, k_scr, LAT_S), (v_ref, v_scr, LAT_S)):
            _stage_rows(dst, rows, lambda rb, kv, src=src: _dup_halves(
                src[rb, LANES * (kv // 2):LANES * (kv // 2 + 1)], kv))

    band = TQ + 2 * WINDOW
    ws = pl.multiple_of(jnp.clip(qi * TQ - WINDOW, 0, LAT_S - band), WINDOW)
    qpos = qi * TQ + lax.broadcasted_iota(jnp.int32, (TQ, band), 0)
    kpos = ws + lax.broadcasted_iota(jnp.int32, (TQ, band), 1)
    bias_scr[...] = jnp.where(jnp.abs(qpos - kpos) <= WINDOW, 0.0, NEG_BIG)
    segs_of = lambda kv: [(ck_scr[kv], cv_scr[kv]), (k_scr[kv, pl.ds(ws, band), :], v_scr[kv, pl.ds(ws, band), :])]
    _attend_heads64(q_ref, o_ref, sink_ref, segs_of, [None, bias_scr], TQ, scr)


def _lat_attn_c_body(q_ref, k_ref, v_ref, ck_ref, cv_ref, o_ref, ck_scr, cv_scr, k_scr, v_scr, *scr):
    @pl.when(pl.program_id(1) == 0)
    def _():
        for src, dst, rows in ((ck_ref, ck_scr, PAST), (cv_ref, cv_scr, PAST),
                               (k_ref, k_scr, LAT_S), (v_ref, v_scr, LAT_S)):
            _stage_rows(dst, rows, lambda rb, kv, src=src: src[rb, LANES * kv:LANES * (kv + 1)].astype(BF16))

    _attend_heads128(q_ref, o_ref, lambda kv: [(ck_scr[kv], cv_scr[kv]), (k_scr[kv], v_scr[kv])], TQ, scr)


_SMEM_SPEC = pl.BlockSpec(memory_space=pltpu.SMEM)


def _attention(q, k_ctx, k_lat, v_ctx, v_lat, cache_k, cache_v, sink, layer):
    nk = k_ctx.shape[1]
    ck = cache_k.reshape(LAT_B, PAST, nk)
    cv = cache_v.reshape(LAT_B, PAST, nk)
    ctx_specs = [
        pl.BlockSpec((CTX_S, D), lambda b: (b, 0)),
        pl.BlockSpec((CTX_S, nk), lambda b: (b, 0)),
        pl.BlockSpec((CTX_S, nk), lambda b: (b, 0)),
    ]
    qb = LAT_S // TQ
    lat_specs = [
        pl.BlockSpec((TQ, D), lambda b, i: (N_CTX // TQ + b * qb + i, 0)),
        pl.BlockSpec((LAT_S, nk), lambda b, i: (b, 0)),
        pl.BlockSpec((LAT_S, nk), lambda b, i: (b, 0)),
        pl.BlockSpec((None, PAST, nk), lambda b, i: (b, 0, 0)),
        pl.BlockSpec((None, PAST, nk), lambda b, i: (b, 0, 0)),
    ]
    heads_per_kv = D // nk
    band = TQ + 2 * WINDOW

    def softmax_scratch(m_rows, keys):
        return [pltpu.VMEM((m_rows, keys), F32), pltpu.VMEM((m_rows, keys), BF16), pltpu.VMEM((m_rows, LANES), F32)]

    lat_keys = PAST + (band if sink is not None else LAT_S)
    staged = [pltpu.VMEM((4, PAST, LANES), BF16), pltpu.VMEM((4, PAST, LANES), BF16),
              pltpu.VMEM((4, LAT_S, LANES), BF16), pltpu.VMEM((4, LAT_S, LANES), BF16)]
    if sink is not None:
        staged.append(pltpu.VMEM((TQ, band), F32))
    ctx_out = dict(out_specs=pl.BlockSpec((CTX_S, D), lambda b: (b, 0)),
                   out_shape=jax.ShapeDtypeStruct((N_CTX, D), BF16),
                   grid=(CTX_B,), compiler_params=_params("arbitrary"))
    lat_out = dict(out_specs=pl.BlockSpec((TQ, D), lambda b, i: (b * qb + i, 0)),
                   out_shape=jax.ShapeDtypeStruct((N_LAT, D), BF16),
                   scratch_shapes=staged + softmax_scratch(heads_per_kv * TQ, lat_keys),
                   grid=(LAT_B, qb), compiler_params=_params("arbitrary", "arbitrary"))
    if sink is not None:
        o_ctx = pl.pallas_call(_ctx_attn_a_body, in_specs=[_SMEM_SPEC] + ctx_specs,
                               name=f"attn_ctx_l{layer}", **ctx_out)(sink, q, k_ctx, v_ctx)
        o_lat = pl.pallas_call(_lat_attn_a_body, in_specs=[_SMEM_SPEC] + lat_specs,
                               name=f"attn_lat_l{layer}", **lat_out)(sink, q, k_lat, v_lat, ck, cv)
    else:
        o_ctx = pl.pallas_call(_ctx_attn_c_body, in_specs=ctx_specs,
                               name=f"attn_ctx_l{layer}", **ctx_out)(q, k_ctx, v_ctx)
        o_lat = pl.pallas_call(_lat_attn_c_body, in_specs=lat_specs,
                               name=f"attn_lat_l{layer}", **lat_out)(q, k_lat, v_lat, ck, cv)
    return o_ctx, o_lat


def _oproj_body(oc_ref, ol_ref, hc_ref, hl_ref, mod_ref, w_ref, out_ref, w_scr):
    @pl.when(pl.program_id(0) == 0)
    def _():
        w_scr[...] = w_ref[...].astype(BF16)

    is_ctx = pl.program_id(0) < N_CTX // TM
    o = jnp.where(is_ctx, oc_ref[...], ol_ref[...])
    h = jnp.where(is_ctx, hc_ref[...], hl_ref[...])
    mix = jnp.dot(o, w_scr[...], preferred_element_type=F32)
    out_ref[...] = h + mod_ref[:, 2 * D:3 * D] * mix


def _out_proj(o_ctx, o_lat, h, mods, layer, w_o):
    (o_ops, o_specs), (h_ops, h_specs) = _token_pair((o_ctx, o_lat), TM), _token_pair(h, TM)
    return pl.pallas_call(
        _oproj_body,
        grid=(N_TOK // TM,),
        in_specs=o_specs + h_specs + [_mod_spec(layer, TM), pl.BlockSpec((D, D), lambda i: (0, 0))],
        out_specs=pl.BlockSpec((TM, D), lambda i: (i, 0)),
        out_shape=jax.ShapeDtypeStruct((N_TOK, D), F32),
        scratch_shapes=[pltpu.VMEM((D, D), BF16)],
        compiler_params=_params("arbitrary"),
        name=f"oproj_l{layer}",
    )(*o_ops, *h_ops, mods.reshape(N_LAYERS, N_COND, 1, 6 * D), w_o)


FFN_CHUNK = CTX_S
FFN_SLOT = FFN_CHUNK + 16
FFN_NCHUNK = TM_FFN // FFN_CHUNK


def _ffn_body(*refs, n_h, n_out):
    h_refs, refs = refs[:n_h], refs[n_h:]
    (mod_ref, g_ref, wg_ref, wv_ref, bg_ref, bv_ref, kg_ref, kv_ref, cg_ref, cv_ref, wd_ref) = refs[:11]
    out_refs, refs = refs[11:11 + n_out], refs[11 + n_out:]
    xn_scr, acc_scr, ug_scr, uv_scr, wg_scr, wv_scr, wd_scr = refs
    i = pl.program_id(0)
    f = pl.program_id(1)
    is_lat = i >= N_CTX // TM_FFN

    def load_h():
        if n_h == 1:
            return h_refs[0][...]
        return jnp.where(is_lat, h_refs[1][...], h_refs[0][...])

    @pl.when(f == 0)
    def _():
        mod = mod_ref[...]
        xn_scr[...] = _norm_mod(load_h(), g_ref[...], mod[:, 3 * D:4 * D], mod[:, 4 * D:5 * D]).astype(BF16)
        acc_scr[...] = jnp.zeros_like(acc_scr)

    wg_scr[...] = wg_ref[...].astype(BF16)
    wv_scr[...] = wv_ref[...].astype(BF16)
    wd_scr[...] = wd_ref[...].astype(BF16)
    base = [8 + FFN_SLOT * k for k in range(FFN_NCHUNK)]

    for k in range(FFN_NCHUNK):
        xk = xn_scr[FFN_CHUNK * k:FFN_CHUNK * (k + 1), :]
        ug_scr[base[k]:base[k] + FFN_CHUNK, :] = jnp.dot(xk, wg_scr[...], preferred_element_type=F32)
        uv_scr[base[k]:base[k] + FFN_CHUNK, :] = jnp.dot(xk, wv_scr[...], preferred_element_type=F32)

    for scr, b_ref in ((ug_scr, bg_ref), (uv_scr, bv_ref)):
        pad = -b_ref[...]
        tops = [pad] + [jnp.where(is_lat, scr[base[k - 1] + FFN_CHUNK - 1:base[k - 1] + FFN_CHUNK, :], pad)
                        for k in range(1, FFN_NCHUNK)]
        bots = [jnp.where(is_lat, scr[base[k + 1]:base[k + 1] + 1, :], pad)
                for k in range(FFN_NCHUNK - 1)] + [pad]
        for k in range(FFN_NCHUNK):
            scr[base[k] - 1:base[k], :] = tops[k]
            scr[base[k] + FFN_CHUNK:base[k] + FFN_CHUNK + 1, :] = bots[k]

    def conv(scr, b_ref, k_ref, c_ref, k):
        kk = k_ref[...]
        const = c_ref[...] + b_ref[...] * (kk[0:1] + kk[1:2] + kk[2:3])
        lo = base[k]
        return (const + kk[0:1] * scr[lo - 1:lo - 1 + FFN_CHUNK, :] + kk[1:2] * scr[lo:lo + FFN_CHUNK, :]
                + kk[2:3] * scr[lo + 1:lo + 1 + FFN_CHUNK, :])

    for k in range(FFN_NCHUNK):
        gate = conv(ug_scr, bg_ref, kg_ref, cg_ref, k)
        val = conv(uv_scr, bv_ref, kv_ref, cv_ref, k)
        a = (_silu(gate) * val).astype(BF16)
        rows = slice(FFN_CHUNK * k, FFN_CHUNK * (k + 1))
        acc_scr[rows, :] += jnp.dot(a, wd_scr[...], preferred_element_type=F32)

    done = f == pl.num_programs(1) - 1

    def write(out_ref):
        out_ref[...] = load_h() + mod_ref[:, 5 * D:6 * D] * acc_scr[...]

    if n_out == 1:
        pl.when(done)(functools.partial(write, out_refs[0]))
    else:
        pl.when(jnp.logical_and(done, jnp.logical_not(is_lat)))(functools.partial(write, out_refs[0]))
        pl.when(jnp.logical_and(done, is_lat))(functools.partial(write, out_refs[1]))


def _ffn(h, mods, layer, gain, w_up, b_up, conv_k, conv_b, w_down, split_out=False):
    nf = D_FF // TF
    b3 = b_up.reshape(N_LAYERS, 1, 2 * D_FF)
    c3 = conv_b.reshape(N_LAYERS, 1, 2 * D_FF)
    col = lambda off: (lambda i, f: (layer, 0, off + f))
    if isinstance(h, tuple):
        h_ops, h_specs = _token_pair(h, TM_FFN)
    else:
        h_ops, h_specs = (h,), [pl.BlockSpec((TM_FFN, D), lambda i, f: (i, 0))]
    if split_out:
        out_specs = _split_specs(TM_FFN, D)
        out_shape = [jax.ShapeDtypeStruct((N_CTX, D), F32), jax.ShapeDtypeStruct((N_LAT, D), F32)]
    else:
        out_specs = pl.BlockSpec((TM_FFN, D), lambda i, f: (i, 0))
        out_shape = jax.ShapeDtypeStruct((N_TOK, D), F32)
    slot_rows = 8 + FFN_SLOT * FFN_NCHUNK
    result = pl.pallas_call(
        functools.partial(_ffn_body, n_h=len(h_ops), n_out=2 if split_out else 1),
        grid=(N_TOK // TM_FFN, nf),
        in_specs=h_specs + [
            _mod_spec(layer, TM_FFN),
            pl.BlockSpec((1, D), lambda i, f: (0, 0)),
            pl.BlockSpec((None, D, TF), col(0)),
            pl.BlockSpec((None, D, TF), col(nf)),
            pl.BlockSpec((None, 1, TF), col(0)),
            pl.BlockSpec((None, 1, TF), col(nf)),
            pl.BlockSpec((None, 3, TF), col(0)),
            pl.BlockSpec((None, 3, TF), col(nf)),
            pl.BlockSpec((None, 1, TF), col(0)),
            pl.BlockSpec((None, 1, TF), col(nf)),
            pl.BlockSpec((None, TF, D), lambda i, f: (layer, f, 0)),
        ],
        out_specs=out_specs,
        out_shape=out_shape,
        scratch_shapes=[pltpu.VMEM((TM_FFN, D), BF16), pltpu.VMEM((TM_FFN, D), F32),
                        pltpu.VMEM((slot_rows, TF), F32), pltpu.VMEM((slot_rows, TF), F32),
                        pltpu.VMEM((D, TF), BF16), pltpu.VMEM((D, TF), BF16), pltpu.VMEM((TF, D), BF16)],
        compiler_params=_params("arbitrary", "arbitrary"),
        name=f"ffn_l{layer}",
    )(*h_ops, mods.reshape(N_LAYERS, N_COND, 1, 6 * D), gain.reshape(1, D), w_up, w_up, b3, b3,
      conv_k, conv_k, c3, c3, w_down)
    return tuple(result) if split_out else result


S5_SEQ = 8
S5_CB = 4
CTX_SETS = CTX_B // S5_SEQ
CTX_CHUNKS = CTX_S // S5_L
LAT_CHUNKS = LAT_S // S5_L
S5_CTX_STEPS = CTX_SETS * CTX_CHUNKS // S5_CB
S5_LAT_STEPS = LAT_CHUNKS // S5_CB
S5_STEPS = S5_CTX_STEPS + S5_LAT_STEPS
S5_ROWS = (N_CTX + N_LAT) // S5_L
S5_CTX_ROWS = N_CTX // S5_L


def _s5_tile_specs():
    per = CTX_CHUNKS // S5_CB
    blk = (S5_SEQ, S5_CB, S5_L, D)

    def ctx_idx(n):
        m = jnp.minimum(n, S5_CTX_STEPS - 1)
        return (m // per, m % per, 0, 0)

    return blk, ctx_idx, (lambda n: (jnp.maximum(n - S5_CTX_STEPS, 0)))


def _s5_mod8(mods, layer):
    m = mods[layer]
    return jnp.stack([jnp.broadcast_to(m[0:1], (S5_SEQ, 6 * D)), m[LAT_ROW0:LAT_ROW0 + LAT_B]]).reshape(
        2, S5_SEQ, 1, 6 * D)


def _s5_in_body(hc_ref, hl_ref, mod_ref, g_ref, u_ref):
    is_lat = pl.program_id(0) >= S5_CTX_STEPS
    mod = mod_ref[...]
    gain = g_ref[...]
    for c in range(S5_CB):
        x = jnp.where(is_lat, hl_ref[:, c], hc_ref[:, c])
        u_ref[c] = _norm_mod(x, gain, mod[:, :, 0:D], mod[:, :, D:2 * D]).astype(BF16)


def _s5_in(h, mod8, gain):
    blk, ctx_idx, lat_idx = _s5_tile_specs()
    return pl.pallas_call(
        _s5_in_body,
        grid=(S5_STEPS,),
        in_specs=[pl.BlockSpec(blk, ctx_idx),
                  pl.BlockSpec(blk, lambda n: (N_CTX // N_LAT, lat_idx(n), 0, 0)),
                  pl.BlockSpec((None, S5_SEQ, 1, 6 * D), lambda n: (jnp.where(n >= S5_CTX_STEPS, 1, 0), 0, 0, 0)),
                  pl.BlockSpec((1, D), lambda n: (0, 0))],
        out_specs=pl.BlockSpec((None, S5_CB, S5_SEQ, S5_L, D), lambda n: (n, 0, 0, 0, 0)),
        out_shape=jax.ShapeDtypeStruct((S5_STEPS, S5_CB, S5_SEQ, S5_L, D), BF16),
        compiler_params=_params("arbitrary"),
        name="s5_in",
    )(h.reshape(N_TOK // CTX_S, CTX_CHUNKS, S5_L, D), h.reshape(N_TOK // LAT_S, LAT_CHUNKS, S5_L, D),
      mod8, gain.reshape(1, D))


def _s5_weights_body(lamc_re_ref, lamc_im_ref, lamr_re_ref, lamr_im_ref, ldt_ref,
                     bt_re_ref, bt_im_ref, btr_re_ref, btr_im_ref, ct_re_ref, ct_im_ref,
                     w_ref, a_ref):
    blk = (lax.broadcasted_iota(jnp.int32, (1, 256), 1) // S5_L).astype(F32)
    lane256 = lax.broadcasted_iota(jnp.int32, (S5_C, 256), 1)
    hi = lax.Precision.HIGHEST
    krow = []
    st_rows = {}
    w_rows = {}
    for d in range(2):
        dt = jnp.exp(ldt_ref[d])
        lr = lamc_re_ref[d]
        li = lamc_im_ref[d]

        ang = (li * dt) * blk
        ph_r, ph_i = jnp.cos(ang), jnp.sin(ang)
        mag = jnp.exp((lr * dt) * blk)
        asc_r, asc_i = mag * ph_r, mag * ph_i
        top_r = ph_r[:, S5_L * (S5_L - 1):S5_L * (S5_L - 1) + 1]
        top_i = ph_i[:, S5_L * (S5_L - 1):S5_L * (S5_L - 1) + 1]
        mag = jnp.exp((lr * dt) * (15.0 - blk))
        dsc_r = mag * (top_r * ph_r + top_i * ph_i)
        dsc_i = mag * (top_i * ph_r - top_r * ph_i)
        ar = asc_r[:, S5_L:S5_L + 1]
        ai = asc_i[:, S5_L:S5_L + 1]
        den = lr * lr + li * li
        n_re = ar - 1.0
        f_re = (n_re * lr + ai * li) / den
        f_im = (ai * lr - n_re * li) / den
        btr = jnp.tile(bt_re_ref[d], (1, S5_L))
        bti = jnp.tile(bt_im_ref[d], (1, S5_L))
        bbr = f_re * btr - f_im * bti
        bbi = f_re * bti + f_im * btr
        e0r, e0i = (asc_r, asc_i) if d == 0 else (dsc_r, dsc_i)
        e1r = e0r * ar - e0i * ai
        e1i = e0r * ai + e0i * ar
        pr, pi = (dsc_r, dsc_i) if d == 0 else (asc_r, asc_i)
        st_rows[("re", d)] = pr * bbr - pi * bbi
        st_rows[("im", d)] = pr * bbi + pi * bbr
        ctr = jnp.tile(ct_re_ref[d], (1, S5_L))
        cti = jnp.tile(ct_im_ref[d], (1, S5_L))
        k_re = ctr * e0r - cti * e0i
        k_imneg = -(ctr * e0i + cti * e0r)
        w_rows[("re", d)] = ctr * e1r - cti * e1i
        w_rows[("im", d)] = -(ctr * e1i + cti * e1r)
        lrr = lamr_re_ref[d:d + 1, :]
        lir = lamr_im_ref[d:d + 1, :]
        magr = jnp.exp(lrr * dt)
        arr = magr * jnp.cos(lir * dt)
        air = magr * jnp.sin(lir * dt)
        denr = lrr * lrr + lir * lir
        nr = arr - 1.0
        fr = (nr * lrr + air * lir) / denr
        fi = (air * lrr - nr * lir) / denr
        bbr_row = fr * btr_re_ref[d] - fi * btr_im_ref[d]
        bbi_row = fr * btr_im_ref[d] + fi * btr_re_ref[d]
        krow.append(jnp.dot(bbr_row, k_re, precision=hi, preferred_element_type=F32)
                    + jnp.dot(bbi_row, k_imneg, precision=hi, preferred_element_type=F32))
        mag16 = jnp.exp(lrr * dt * 16.0)
        a_ref[2 * d:2 * d + 1, :] = mag16 * jnp.cos(lir * dt * 16.0)
        a_ref[2 * d + 1:2 * d + 2, :] = mag16 * jnp.sin(lir * dt * 16.0)
    a_ref[4:8, :] = jnp.zeros((4, S5_P), F32)

    t_rows = []
    for s in range(S5_L):
        fwd = krow[0] if s == 0 else jnp.where(lane256 >= S5_C * s, pltpu.roll(krow[0], S5_C * s, 1), 0.0)
        bwd = krow[1] if s == S5_L - 1 else jnp.where(lane256 < S5_C * (s + 1),
                                                      pltpu.roll(krow[1], S5_C * (s + 1), 1), 0.0)
        t_rows.append(fwd + bwd)
    order = [("re", 0), ("re", 1), ("im", 0), ("im", 1)]
    w_ref[0:256, :] = jnp.concatenate(t_rows, axis=0).T.astype(BF16)
    w_ref[256:512, :] = jnp.concatenate([st_rows[o] for o in order], axis=0).astype(BF16)
    w_ref[512:768, :] = jnp.concatenate([w_rows[o] for o in order], axis=0).T.astype(BF16)


def _s5_weights(lam_re, lam_im, log_dt, b_re, b_im, c_re, c_im):
    col = lambda x: x.transpose(1, 0, 2).reshape(S5_G, 2, S5_P, 1)
    row = lambda x: x.transpose(1, 0, 2)
    ldt = log_dt.transpose(1, 0).reshape(S5_G, 2, 1, 1)
    b_colf = lambda x: x.transpose(1, 0, 2, 3)
    b_rowf = lambda x: x.transpose(1, 0, 3, 2)
    c_colf = lambda x: x.transpose(1, 0, 3, 2)
    g4 = lambda *tail: pl.BlockSpec((None,) + tail, lambda g: (g,) + (0,) * len(tail))
    return pl.pallas_call(
        _s5_weights_body,
        grid=(S5_G,),
        in_specs=[g4(2, S5_P, 1), g4(2, S5_P, 1), g4(2, S5_P), g4(2, S5_P), g4(2, 1, 1),
                  g4(2, S5_P, S5_C), g4(2, S5_P, S5_C), g4(2, S5_C, S5_P), g4(2, S5_C, S5_P),
                  g4(2, S5_P, S5_C), g4(2, S5_P, S5_C)],
        out_specs=[g4(768, 256), g4(8, S5_P)],
        out_shape=[jax.ShapeDtypeStruct((S5_G, 768, 256), BF16),
                   jax.ShapeDtypeStruct((S5_G, 8, S5_P), F32)],
        compiler_params=_params("arbitrary"),
        name="s5_weights",
    )(col(lam_re), col(lam_im), row(lam_re), row(lam_im), ldt,
      b_colf(b_re), b_colf(b_im), b_rowf(b_re), b_rowf(b_im), c_colf(c_re), c_colf(c_im))


def _s5_scan(d_scr, r0, a_re, a_im, init_re, init_im, hin_scr, n_chunks):
    lane = lax.broadcasted_iota(jnp.int32, (S5_SEQ, LANES), 1)
    fwd = lane < S5_P
    hr, hi = init_re, init_im
    for c in range(n_chunks):
        cf = slice(r0 + c * S5_SEQ, r0 + (c + 1) * S5_SEQ)
        cb = slice(r0 + (n_chunks - 1 - c) * S5_SEQ, r0 + (n_chunks - c) * S5_SEQ)
        hin_scr[cf, 0:S5_P] = hr[:, 0:S5_P]
        hin_scr[cb, S5_P:LANES] = hr[:, S5_P:LANES]
        hin_scr[cf, LANES:LANES + S5_P] = hi[:, 0:S5_P]
        hin_scr[cb, LANES + S5_P:2 * LANES] = hi[:, S5_P:LANES]
        dr = jnp.where(fwd, d_scr[cf, 0:LANES], d_scr[cb, 0:LANES])
        di = jnp.where(fwd, d_scr[cf, LANES:2 * LANES], d_scr[cb, LANES:2 * LANES])
        hr, hi = hr * a_re - hi * a_im + dr, hr * a_im + hi * a_re + di
    return hr, hi


def _s5_core_body(ut_ref, w_ref, a_ref, s0_ref, yt_ref, fin_ref, ts_scr, d_scr, hin_scr):
    a_re = a_ref[0:1, :]
    a_im = a_ref[1:2, :]
    rhs = ut_ref[...].reshape(S5_L * S5_C, S5_ROWS)
    ts_scr[...] = jnp.dot(w_ref[0:512, :], rhs, preferred_element_type=F32)
    d_scr[...] = ts_scr[256:512, :].T
    zero = jnp.zeros((S5_SEQ, LANES), F32)
    for hb in range(CTX_SETS):
        fr, fi = _s5_scan(d_scr, hb * CTX_CHUNKS * S5_SEQ, a_re, a_im, zero, zero, hin_scr, CTX_CHUNKS)
        fin_ref[S5_SEQ * hb:S5_SEQ * (hb + 1), 0:LANES] = fr
        fin_ref[S5_SEQ * hb:S5_SEQ * (hb + 1), LANES:2 * LANES] = fi
    _s5_scan(d_scr, S5_CTX_ROWS, a_re, a_im, s0_ref[:, 0:LANES], s0_ref[:, LANES:2 * LANES], hin_scr, LAT_CHUNKS)
    y = ts_scr[0:256, :] + lax.dot_general(w_ref[512:768, :], hin_scr[...].astype(BF16),
                                           (((1,), (1,)), ((), ())), preferred_element_type=F32)
    yt_ref[...] = y.reshape(S5_L, S5_C, S5_ROWS)


def _s5_core(ut, wall, avec, s0):
    g3 = lambda a, b: pl.BlockSpec((None, a, b), lambda g: (g, 0, 0))
    tok = pl.BlockSpec((S5_L, S5_C, S5_ROWS), lambda g: (0, g, 0))
    return pl.pallas_call(
        _s5_core_body,
        grid=(S5_G,),
        in_specs=[tok, g3(768, 256), g3(8, LANES), g3(LAT_B, 256)],
        out_specs=[tok, g3(CTX_B, 256)],
        out_shape=[jax.ShapeDtypeStruct((S5_L, D, S5_ROWS), F32),
                   jax.ShapeDtypeStruct((S5_G, CTX_B, 256), F32)],
        scratch_shapes=[pltpu.VMEM((512, S5_ROWS), F32), pltpu.VMEM((S5_ROWS, 256), F32),
                        pltpu.VMEM((S5_ROWS, 256), F32)],
        compiler_params=_params("arbitrary"),
        name="s5_core",
    )(ut, wall, avec, s0)


def _gelu_tanh(x):
    return 0.5 * x * (1.0 + jnp.tanh(math.sqrt(2.0 / math.pi) * (x + 0.044715 * (x * x * x))))


def _s5_out_body(hc_ref, hl_ref, y_ref, mod_ref, g_ref, dskip_ref, w_ref, oc_ref, ol_ref, w_scr):
    n = pl.program_id(0)
    is_lat = n >= S5_CTX_STEPS

    @pl.when(n == 0)
    def _():
        w_scr[...] = w_ref[...].astype(BF16)

    mod = mod_ref[...]
    gain = g_ref[...]
    dskip = dskip_ref[...]
    hs, gs = [], []
    for c in range(S5_CB):
        h = jnp.where(is_lat, hl_ref[:, c], hc_ref[:, c])
        u = _norm_mod(h, gain, mod[:, :, 0:D], mod[:, :, D:2 * D])
        y = u * dskip + y_ref[c]
        hs.append(h)
        gs.append(_gelu_tanh(y).astype(BF16).reshape(S5_SEQ * S5_L, D))
    hh = jnp.dot(jnp.concatenate(gs, axis=0), w_scr[...], preferred_element_type=F32)
    rows = S5_SEQ * S5_L
    for c in range(S5_CB):
        blk = hh[rows * c:rows * (c + 1)]
        mix = (blk[:, 0:D] * jax.nn.sigmoid(blk[:, D:2 * D])).reshape(S5_SEQ, S5_L, D)
        out = hs[c] + mod[:, :, 2 * D:3 * D] * mix

        @pl.when(is_lat)
        def _():
            ol_ref[:, c] = out

        @pl.when(jnp.logical_not(is_lat))
        def _():
            oc_ref[:, c] = out


def _s5_out(h, y, mod8, gain, d_skip, w_glu):
    blk, ctx_idx, lat_idx = _s5_tile_specs()
    h_ctx, h_lat = pl.pallas_call(
        _s5_out_body,
        grid=(S5_STEPS,),
        in_specs=[pl.BlockSpec(blk, ctx_idx),
                  pl.BlockSpec(blk, lambda n: (N_CTX // N_LAT, lat_idx(n), 0, 0)),
                  pl.BlockSpec((None, S5_CB, S5_SEQ, S5_L, D), lambda n: (n, 0, 0, 0, 0)),
                  pl.BlockSpec((None, S5_SEQ, 1, 6 * D), lambda n: (jnp.where(n >= S5_CTX_STEPS, 1, 0), 0, 0, 0)),
                  pl.BlockSpec((1, D), lambda n: (0, 0)),
                  pl.BlockSpec((1, D), lambda n: (0, 0)),
                  pl.BlockSpec((D, 2 * D), lambda n: (0, 0))],
        out_specs=[pl.BlockSpec(blk, ctx_idx),
                   pl.BlockSpec(blk, lambda n: (0, lat_idx(n), 0, 0))],
        out_shape=[jax.ShapeDtypeStruct((CTX_B, CTX_CHUNKS, S5_L, D), F32),
                   jax.ShapeDtypeStruct((LAT_B, LAT_CHUNKS, S5_L, D), F32)],
        scratch_shapes=[pltpu.VMEM((D, 2 * D), BF16)],
        compiler_params=_params("arbitrary"),
        name="s5_out",
    )(h.reshape(N_TOK // CTX_S, CTX_CHUNKS, S5_L, D), h.reshape(N_TOK // LAT_S, LAT_CHUNKS, S5_L, D),
      y, mod8, gain.reshape(1, D), d_skip.reshape(1, D), w_glu)
    return h_ctx.reshape(N_CTX, D), h_lat.reshape(N_LAT, D)


def _s5_mixer(h, mods, layer, gain, state, lam_re, lam_im, log_dt, b_re, b_im, c_re, c_im, d_skip, w_glu):
    mod8 = _s5_mod8(mods, layer)
    u = _s5_in(h, mod8, gain)
    ut = u.reshape(S5_ROWS, S5_L, D).transpose(1, 2, 0)
    wall, avec = _s5_weights(lam_re, lam_im, log_dt, b_re, b_im, c_re, c_im)
    a2 = jnp.stack([jnp.concatenate([avec[:, 0], avec[:, 2]], axis=-1),
                    jnp.concatenate([avec[:, 1], avec[:, 3]], axis=-1)], axis=1)
    a2 = jnp.pad(a2, ((0, 0), (0, 6), (0, 0)))
    s0 = state.transpose(3, 0, 2, 1, 4).reshape(S5_G, LAT_B, 4 * S5_P)
    yt, fin = _s5_core(ut, wall, a2, s0)
    y = yt.transpose(2, 0, 1).reshape(S5_STEPS, S5_CB, S5_SEQ, S5_L, D)
    new_state = fin.reshape(S5_G, CTX_B, 2, 2, S5_P).transpose(1, 3, 2, 0, 4)
    return _s5_out(h, y, mod8, gain, d_skip, w_glu), new_state


def kernel(x_prompt, x_sample, cache_l0_k, cache_l0_v, state_l1, cache_l2_k, cache_l2_v, cache_l3_k, cache_l3_v, c, c_ctx, norm1, norm2, w_mod, b_mod, w_up, b_up, conv_k, conv_b, w_down, l0_w_qkv, l0_q_norm, l0_k_norm, l0_sink, l0_w_o, l1_lam_re, l1_lam_im, l1_log_dt, l1_b_re, l1_b_im, l1_c_re, l1_c_im, l1_d_skip, l1_w_glu, l2_w_qkv, l2_q_norm, l2_k_norm, l2_w_o, l3_w_qkv, l3_q_norm, l3_k_norm, l3_sink, l3_w_o):
    h = (x_prompt.reshape(N_CTX, D), x_sample.reshape(N_LAT, D))
    cond = jnp.concatenate([c_ctx[None, :], jnp.zeros((LAT_ROW0 - 1, D), F32), c], axis=0)
    mods = _modulation(cond, w_mod, b_mod)

    attn_layers = {
        0: (l0_w_qkv, l0_q_norm, l0_k_norm, l0_sink, l0_w_o, cache_l0_k, cache_l0_v, 16, 4, 64),
        2: (l2_w_qkv, l2_q_norm, l2_k_norm, None, l2_w_o, cache_l2_k, cache_l2_v, 8, 4, 128),
        3: (l3_w_qkv, l3_q_norm, l3_k_norm, l3_sink, l3_w_o, cache_l3_k, cache_l3_v, 16, 4, 64),
    }
    new_kv = {}
    new_state = None
    for layer in range(N_LAYERS):
        if layer in attn_layers:
            w_qkv, q_norm, k_norm, sink, w_o, ck, cv, n_heads, n_kv, dh = attn_layers[layer]
            q, k_ctx, k_lat, v_ctx, v_lat = _qkv_proj(h, mods, layer, norm1[layer], w_qkv, q_norm, k_norm,
                                                      n_heads, n_kv, dh)
            new_kv[layer] = (k_ctx.reshape(CTX_B, CTX_S, n_kv, dh), v_ctx.reshape(CTX_B, CTX_S, n_kv, dh))
            o_ctx, o_lat = _attention(q, k_ctx, k_lat, v_ctx, v_lat, ck, cv, sink, layer)
            h = _out_proj(o_ctx, o_lat, h, mods, layer, w_o)
        else:
            h, new_state = _s5_mixer(h, mods, layer, norm1[layer], state_l1, l1_lam_re, l1_lam_im, l1_log_dt,
                                     l1_b_re, l1_b_im, l1_c_re, l1_c_im, l1_d_skip, l1_w_glu)
        h = _ffn(h, mods, layer, norm2[layer], w_up, b_up, conv_k, conv_b, w_down,
                 split_out=layer == N_LAYERS - 1)

    y_prompt = h[0].reshape(CTX_B, CTX_S, D)
    y_sample = h[1].reshape(LAT_B, LAT_S, D)
    return (y_prompt, y_sample, new_kv[0][0], new_kv[0][1], new_state,
            new_kv[2][0], new_kv[2][1], new_kv[3][0], new_kv[3][1])
```

```python
import functools
import math

import jax
import jax.numpy as jnp
import numpy as np
from jax import lax
from jax.experimental import pallas as pl
from jax.experimental.pallas import tpu as pltpu

F32 = jnp.float32
BF16 = jnp.bfloat16

D = 1024
N_LAYERS = 4
CTX_B, CTX_S = 32, 256
LAT_B, LAT_S = 8, 1024
PAST = 512
N_CTX = CTX_B * CTX_S
N_LAT = LAT_B * LAT_S
N_TOK = N_CTX + N_LAT
GRID_W = 64
WINDOW = 128
ROPE_THETA = 10000.0
EPS = 1e-6
D_FF = 2816
N_COND = 16
LAT_ROW0 = 8

S5_G = 64
S5_C = 16
S5_P = 64
S5_L = 16

VMEM_LIMIT = 56 * 1024 * 1024
LANES = 128
NEG_BIG = -1e30
LOG2E = math.log2(math.e)

TM = 512
TM_FFN = 1024
TF = 256
TQ = 256
ATT_RB = 32
ATT_KEYS = 1024
STAGE_RB = 128
QKV_RB = 64


def _params(*sem):
    return pltpu.CompilerParams(dimension_semantics=sem, vmem_limit_bytes=VMEM_LIMIT)


def _cond_row(i, tm):
    n_ctx_tiles = N_CTX // tm
    return jnp.where(i < n_ctx_tiles, 0, LAT_ROW0 + (i - n_ctx_tiles) // (LAT_S // tm))


def _norm_mod(x, gain, shift, scale):
    ms = jnp.mean(x * x, axis=-1, keepdims=True)
    return (x * lax.rsqrt(ms + EPS) * gain) * (1.0 + scale) + shift


def _silu(x):
    return x * jax.nn.sigmoid(x)


def _mod_body(cond_ref, w_ref, b_ref, o_ref):
    s = _silu(cond_ref[...]).astype(BF16)
    o_ref[...] = jnp.dot(s, w_ref[...].astype(BF16), preferred_element_type=F32) + b_ref[...]


def _modulation(cond, w_mod, b_mod):
    tn = 1536
    return pl.pallas_call(
        _mod_body,
        grid=(N_LAYERS, 6 * D // tn),
        in_specs=[
            pl.BlockSpec((N_COND, D), lambda l, n: (0, 0)),
            pl.BlockSpec((None, D, tn), lambda l, n: (l, 0, n)),
            pl.BlockSpec((None, 1, tn), lambda l, n: (l, 0, n)),
        ],
        out_specs=pl.BlockSpec((None, N_COND, tn), lambda l, n: (l, 0, n)),
        out_shape=jax.ShapeDtypeStruct((N_LAYERS, N_COND, 6 * D), F32),
        compiler_params=_params("arbitrary", "arbitrary"),
        name="modulation",
    )(cond, w_mod, b_mod.reshape(N_LAYERS, 1, 6 * D))


def _mod_spec(layer, tm):
    return pl.BlockSpec((None, None, 1, 6 * D), lambda i, *_: (layer, _cond_row(i, tm), 0, 0))


def _token_pair(h, tm):
    n_ctx_tiles = N_CTX // tm
    h_ctx, h_lat, lat_off = (h[0], h[1], 0) if isinstance(h, tuple) else (h, h, n_ctx_tiles)
    specs = [pl.BlockSpec((tm, D), lambda i, *_: (jnp.minimum(i, n_ctx_tiles - 1), 0)),
             pl.BlockSpec((tm, D), lambda i, *_: (jnp.maximum(i - n_ctx_tiles, 0) + lat_off, 0))]
    return (h_ctx, h_lat), specs


def _split_specs(tm, width):
    n_ctx_tiles = N_CTX // tm
    return [pl.BlockSpec((tm, width), lambda i, *_: (jnp.minimum(i, n_ctx_tiles - 1), 0)),
            pl.BlockSpec((tm, width), lambda i, *_: (jnp.maximum(i - n_ctx_tiles, 0), 0))]


def _qkv_body(hc_ref, hl_ref, mod_ref, g_ref, w_ref, qn_ref, kn_ref, bd_ref, cos_ref, sin_ref,
              q_ref, kc_ref, kl_ref, vc_ref, vl_ref, w_scr, xn_scr, qkv_scr, sq_scr, ss_scr, *, nq, nk, dh):
    is_lat = pl.program_id(0) >= N_CTX // TM

    @pl.when(pl.program_id(0) == 0)
    def _():
        w_scr[...] = w_ref[...].astype(BF16)

    mod = mod_ref[...]
    row_blocks = [slice(r, r + QKV_RB) for r in range(0, TM, QKV_RB)]
    for rb in row_blocks:
        h = jnp.where(is_lat, hl_ref[rb, :], hc_ref[rb, :])
        xn_scr[rb, :] = _norm_mod(h, g_ref[...], mod[:, 0:D], mod[:, D:2 * D]).astype(BF16)
    qkv_scr[...] = jnp.dot(xn_scr[...], w_scr[...], preferred_element_type=F32)

    bd = bd_ref[...]
    quarter = dh // 4
    lane = lax.broadcasted_iota(jnp.int32, (QKV_RB, LANES), 1)
    first = (lane % (2 * quarter)) < quarter
    inv_dh = 1.0 / dh

    def head_norm_rope(c0, gain, out_ref, o0):
        for rb in row_blocks:
            z = qkv_scr[rb, c0:c0 + 256]
            sq_scr[rb, :] = (z * z).astype(BF16)
        ss_scr[...] = jnp.dot(sq_scr[...], bd, preferred_element_type=F32)
        for rb in row_blocks:
            zn = qkv_scr[rb, c0:c0 + 256] * lax.rsqrt(ss_scr[rb, :] * inv_dh + EPS)
            for j in range(2):
                zz = zn[:, LANES * j:LANES * (j + 1)] * gain
                partner = jnp.where(first, pltpu.roll(zz, LANES - quarter, 1), pltpu.roll(zz, quarter, 1))
                zz = zz * cos_ref[rb, :] + partner * sin_ref[rb, :]
                out_ref[rb, o0 + LANES * j:o0 + LANES * (j + 1)] = zz.astype(out_ref.dtype)

    qgain = qn_ref[...] * (dh ** -0.5 * LOG2E)
    kgain = kn_ref[...]

    def finish(k_ref, v_ref):
        for c in range(nq // 256):
            head_norm_rope(256 * c, qgain, q_ref, 256 * c)
        for c in range(nk // 256):
            head_norm_rope(nq + 256 * c, kgain, k_ref, 256 * c)
        for rb in row_blocks:
            v_ref[rb, :] = qkv_scr[rb, nq + nk:]

    pl.when(is_lat)(functools.partial(finish, kl_ref, vl_ref))
    pl.when(jnp.logical_not(is_lat))(functools.partial(finish, kc_ref, vc_ref))


def _rope_tables(dh):
    half, quarter = dh // 2, dh // 4
    freqs = 1.0 / (ROPE_THETA ** (np.arange(quarter, dtype=np.float32) / quarter))
    pos = np.arange(LAT_S)
    row = (pos // GRID_W).astype(np.float32)
    col = (pos % GRID_W).astype(np.float32)
    ang_r = (row[:, None] * freqs[None, :]).astype(np.float32)
    ang_c = (col[:, None] * freqs[None, :]).astype(np.float32)
    cos = np.concatenate([np.cos(ang_r), np.cos(ang_r), np.cos(ang_c), np.cos(ang_c)], axis=1)
    sin = np.concatenate([-np.sin(ang_r), np.sin(ang_r), -np.sin(ang_c), np.sin(ang_c)], axis=1)
    reps = LANES // dh
    cos = np.tile(cos.astype(np.float32), (1, reps))
    sin = np.tile(sin.astype(np.float32), (1, reps))
    cos = np.concatenate([np.ones_like(cos), cos], axis=0)
    sin = np.concatenate([np.zeros_like(sin), sin], axis=0)
    return jnp.asarray(cos), jnp.asarray(sin)


def _block_diag_ones(dh):
    idx = np.arange(256) // dh
    return jnp.asarray((idx[:, None] == idx[None, :]).astype(np.float32), dtype=BF16)


def _qkv_proj(h, mods, layer, gain, w_qkv, q_norm, k_norm, n_heads, n_kv, dh):
    nq, nk = n_heads * dh, n_kv * dh
    nqkv = nq + 2 * nk
    cos, sin = _rope_tables(dh)
    reps = LANES // dh
    n_ctx_tiles = N_CTX // TM
    lat_tiles = LAT_S // TM

    def rope_idx(i):
        return (jnp.where(i < n_ctx_tiles, 0, lat_tiles + (i - n_ctx_tiles) % lat_tiles), 0)

    h_ops, h_specs = _token_pair(h, TM)
    kv_specs = _split_specs(TM, nk)
    kv_shapes = [jax.ShapeDtypeStruct((N_CTX, nk), F32), jax.ShapeDtypeStruct((N_LAT, nk), F32)]
    return pl.pallas_call(
        functools.partial(_qkv_body, nq=nq, nk=nk, dh=dh),
        grid=(N_TOK // TM,),
        in_specs=h_specs + [
            _mod_spec(layer, TM),
            pl.BlockSpec((1, D), lambda i: (0, 0)),
            pl.BlockSpec((D, nqkv), lambda i: (0, 0)),
            pl.BlockSpec((1, LANES), lambda i: (0, 0)),
            pl.BlockSpec((1, LANES), lambda i: (0, 0)),
            pl.BlockSpec((256, 256), lambda i: (0, 0)),
            pl.BlockSpec((TM, LANES), rope_idx),
            pl.BlockSpec((TM, LANES), rope_idx),
        ],
        out_specs=[pl.BlockSpec((TM, nq), lambda i: (i, 0))] + kv_specs + kv_specs,
        out_shape=[jax.ShapeDtypeStruct((N_TOK, nq), BF16)] + kv_shapes + kv_shapes,
        scratch_shapes=[pltpu.VMEM((D, nqkv), BF16), pltpu.VMEM((TM, D), BF16), pltpu.VMEM((TM, nqkv), F32),
                        pltpu.VMEM((TM, 256), BF16),
                        pltpu.VMEM((TM, 256), F32)],
        compiler_params=_params("arbitrary"),
        name=f"qkv_l{layer}",
    )(*h_ops, mods.reshape(N_LAYERS, N_COND, 1, 6 * D), gain.reshape(1, D), w_qkv,
      jnp.tile(q_norm, reps).reshape(1, LANES), jnp.tile(k_norm, reps).reshape(1, LANES),
      _block_diag_ones(dh), cos, sin)


def _attend(q, segs, sink_of, tq, scr):
    m_rows = q.shape[0]
    if not scr:
        assert all(bias is None for _, _, bias in segs)
        scores = [lax.dot_general(q, k, (((1,), (1,)), ((), ())), preferred_element_type=F32) for k, _, _ in segs]
        m = functools.reduce(jnp.maximum, [jnp.max(s, axis=-1, keepdims=True) for s in scores])
        head = lax.broadcasted_iota(jnp.int32, (m_rows, 1), 0) // tq
        sink = None
        if sink_of(0) is not None:
            sink = functools.reduce(lambda acc, r: jnp.where(head >= r, sink_of(r * tq), acc),
                                    range(1, m_rows // tq), jnp.full((m_rows, 1), sink_of(0), F32))
            m = jnp.maximum(m, sink)
        den = None if sink is None else jnp.exp2(sink - m)
        acc = None
        for s, (_, v, _) in zip(scores, segs):
            p = jnp.exp2(s - m)
            ps = jnp.sum(p, axis=-1, keepdims=True)
            pv = jnp.dot(p.astype(BF16), v, preferred_element_type=F32)
            den = ps if den is None else den + ps
            acc = pv if acc is None else acc + pv
        return acc / den
    s_scr, p_scr, r_scr = scr
    cols = []
    col = 0
    for k, _, _ in segs:
        t = k.shape[0]
        s_scr[0:m_rows, col:col + t] = lax.dot_general(q, k, (((1,), (1,)), ((), ())), preferred_element_type=F32)
        cols.append((col, t))
        col += t
    rows = ATT_RB * max(1, ATT_KEYS // col)
    for r0 in range(0, m_rows, rows):
        rb = slice(r0, r0 + rows)
        parts = []
        for (c0, t), (_, _, bias) in zip(cols, segs):
            s = s_scr[rb, c0:c0 + t]
            if bias is not None:
                s = s + bias[r0 % tq:r0 % tq + rows, :]
            parts.append(s)
        m = functools.reduce(jnp.maximum, [jnp.max(s, axis=-1, keepdims=True) for s in parts])
        sink = sink_of(r0)
        if sink is not None:
            m = jnp.maximum(m, sink)
        den = None if sink is None else jnp.exp2(sink - m)
        for (c0, t), s in zip(cols, parts):
            p = jnp.exp2(s - m)
            ps = jnp.sum(p, axis=-1, keepdims=True)
            den = ps if den is None else den + ps
            p_scr[rb, c0:c0 + t] = p.astype(BF16)
        r_scr[rb, :] = jnp.broadcast_to(den, (rows, LANES))
    acc = None
    for (c0, t), (_, v, _) in zip(cols, segs):
        pv = jnp.dot(p_scr[0:m_rows, c0:c0 + t], v, preferred_element_type=F32)
        acc = pv if acc is None else acc + pv
    return acc / r_scr[0:m_rows, :]


def _dup_halves(x, kv):
    lane = lax.broadcasted_iota(jnp.int32, x.shape, 1)
    r = pltpu.roll(x, 64, 1)
    lo = lane < 64
    return (jnp.where(lo, x, r) if kv % 2 == 0 else jnp.where(lo, r, x)).astype(BF16)


def _dup_segment(k, v, kv):
    c = LANES * (kv // 2)
    return _dup_halves(k[:, c:c + LANES], kv), _dup_halves(v[:, c:c + LANES], kv)


def _stage_rows(dst_ref, rows, fn):
    for kv in range(4):
        for r in range(0, rows, STAGE_RB):
            dst_ref[kv, r:r + STAGE_RB, :] = fn(slice(r, r + STAGE_RB), kv)


def _attend_heads64(q_ref, o_ref, sink_ref, segs_of, biases, tq, scr):
    lane = lax.broadcasted_iota(jnp.int32, (tq, LANES), 1)
    lo = lane < 64
    for kv in range(4):
        segs = [(k, v, b) for (k, v), b in zip(segs_of(kv), biases)]
        parts = []
        for pair in range(2):
            j = 2 * kv + pair
            qp = q_ref[:, LANES * j:LANES * (j + 1)]
            zero = jnp.zeros_like(qp)
            parts += [jnp.where(lo, qp, zero), jnp.where(lo, zero, qp)]
        sinks = [sink_ref[4 * kv + r] * LOG2E for r in range(4)]
        out = _attend(jnp.concatenate(parts, axis=0), segs, lambda r0: sinks[r0 // tq], tq, scr)
        for pair in range(2):
            j = 2 * kv + pair
            a = out[(2 * pair) * tq:(2 * pair + 1) * tq]
            b = out[(2 * pair + 1) * tq:(2 * pair + 2) * tq]
            o_ref[:, LANES * j:LANES * (j + 1)] = jnp.where(lo, a, b).astype(BF16)


def _attend_heads128(q_ref, o_ref, segs_of, tq, scr):
    for kv in range(4):
        segs = [(k, v, None) for k, v in segs_of(kv)]
        q = jnp.concatenate([q_ref[:, LANES * (2 * kv):LANES * (2 * kv + 1)],
                             q_ref[:, LANES * (2 * kv + 1):LANES * (2 * kv + 2)]], axis=0)
        out = _attend(q, segs, lambda r0: None, tq, scr)
        o_ref[:, LANES * (2 * kv):LANES * (2 * kv + 1)] = out[:tq].astype(BF16)
        o_ref[:, LANES * (2 * kv + 1):LANES * (2 * kv + 2)] = out[tq:].astype(BF16)


def _ctx_attn_a_body(sink_ref, q_ref, k_ref, v_ref, o_ref, *scr):
    _attend_heads64(q_ref, o_ref, sink_ref, lambda kv: [_dup_segment(k_ref[...], v_ref[...], kv)], [None],
                    CTX_S, scr)


def _cols128(ref, kv):
    return ref[:, LANES * kv:LANES * (kv + 1)].astype(BF16)


def _ctx_attn_c_body(q_ref, k_ref, v_ref, o_ref, *scr):
    _attend_heads128(q_ref, o_ref, lambda kv: [(_cols128(k_ref, kv), _cols128(v_ref, kv))], CTX_S, scr)


def _lat_attn_a_body(sink_ref, q_ref, k_ref, v_ref, ck_ref, cv_ref, o_ref, ck_scr, cv_scr, k_scr, v_scr,
                     bias_scr, *scr):
    qi = pl.program_id(1)

    @pl.when(qi == 0)
    def _():
        for src, dst, rows in ((ck_ref, ck_scr, PAST), (cv_ref, cv_scr, PAST),
                               (k_ref, k_scr, LAT_S), (v_ref, v_scr, LAT_S)):
            _stage_rows(dst, rows, lambda rb, kv, src=src: _dup_halves(
                src[rb, LANES * (kv // 2):LANES * (kv // 2 + 1)], kv))

    band = TQ + 2 * WINDOW
    ws = pl.multiple_of(jnp.clip(qi * TQ - WINDOW, 0, LAT_S - band), WINDOW)
    qpos = qi * TQ + lax.broadcasted_iota(jnp.int32, (TQ, band), 0)
    kpos = ws + lax.broadcasted_iota(jnp.int32, (TQ, band), 1)
    bias_scr[...] = jnp.where(jnp.abs(qpos - kpos) <= WINDOW, 0.0, NEG_BIG)
    segs_of = lambda kv: [(ck_scr[kv], cv_scr[kv]), (k_scr[kv, pl.ds(ws, band), :], v_scr[kv, pl.ds(ws, band), :])]
    _attend_heads64(q_ref, o_ref, sink_ref, segs_of, [None, bias_scr], TQ, scr)


def _lat_attn_c_body(q_ref, k_ref, v_ref, ck_ref, cv_ref, o_ref, ck_scr, cv_scr, k_scr, v_scr, *scr):
    @pl.when(pl.program_id(1) == 0)
    def _():
        for src, dst, rows in ((ck_ref, ck_scr, PAST), (cv_ref, cv_scr, PAST),
                               (k_ref, k_scr, LAT_S), (v_ref, v_scr, LAT_S)):
            _stage_rows(dst, rows, lambda rb, kv, src=src: src[rb, LANES * kv:LANES * (kv + 1)].astype(BF16))

    _attend_heads128(q_ref, o_ref, lambda kv: [(ck_scr[kv], cv_scr[kv]), (k_scr[kv], v_scr[kv])], TQ, scr)


_SMEM_SPEC = pl.BlockSpec(memory_space=pltpu.SMEM)


def _attention(q, k_ctx, k_lat, v_ctx, v_lat, cache_k, cache_v, sink, layer):
    nk = k_ctx.shape[1]
    ck = cache_k.reshape(LAT_B, PAST, nk)
    cv = cache_v.reshape(LAT_B, PAST, nk)
    ctx_specs = [
        pl.BlockSpec((CTX_S, D), lambda b: (b, 0)),
        pl.BlockSpec((CTX_S, nk), lambda b: (b, 0)),
        pl.BlockSpec((CTX_S, nk), lambda b: (b, 0)),
    ]
    qb = LAT_S // TQ
    lat_specs = [
        pl.BlockSpec((TQ, D), lambda b, i: (N_CTX // TQ + b * qb + i, 0)),
        pl.BlockSpec((LAT_S, nk), lambda b, i: (b, 0)),
        pl.BlockSpec((LAT_S, nk), lambda b, i: (b, 0)),
        pl.BlockSpec((None, PAST, nk), lambda b, i: (b, 0, 0)),
        pl.BlockSpec((None, PAST, nk), lambda b, i: (b, 0, 0)),
    ]
    heads_per_kv = D // nk
    band = TQ + 2 * WINDOW

    def softmax_scratch(m_rows, keys):
        return [pltpu.VMEM((m_rows, keys), F32), pltpu.VMEM((m_rows, keys), BF16), pltpu.VMEM((m_rows, LANES), F32)]

    lat_keys = PAST + (band if sink is not None else LAT_S)
    staged = [pltpu.VMEM((4, PAST, LANES), BF16), pltpu.VMEM((4, PAST, LANES), BF16),
              pltpu.VMEM((4, LAT_S, LANES), BF16), pltpu.VMEM((4, LAT_S, LANES), BF16)]
    if sink is not None:
        staged.append(pltpu.VMEM((TQ, band), F32))
    ctx_out = dict(out_specs=pl.BlockSpec((CTX_S, D), lambda b: (b, 0)),
                   out_shape=jax.ShapeDtypeStruct((N_CTX, D), BF16),
                   grid=(CTX_B,), compiler_params=_params("arbitrary"))
    lat_out = dict(out_specs=pl.BlockSpec((TQ, D), lambda b, i: (b * qb + i, 0)),
                   out_shape=jax.ShapeDtypeStruct((N_LAT, D), BF16),
                   scratch_shapes=staged + softmax_scratch(heads_per_kv * TQ, lat_keys),
                   grid=(LAT_B, qb), compiler_params=_params("arbitrary", "arbitrary"))
    if sink is not None:
        o_ctx = pl.pallas_call(_ctx_attn_a_body, in_specs=[_SMEM_SPEC] + ctx_specs,
                               name=f"attn_ctx_l{layer}", **ctx_out)(sink, q, k_ctx, v_ctx)
        o_lat = pl.pallas_call(_lat_attn_a_body, in_specs=[_SMEM_SPEC] + lat_specs,
                               name=f"attn_lat_l{layer}", **lat_out)(sink, q, k_lat, v_lat, ck, cv)
    else:
        o_ctx = pl.pallas_call(_ctx_attn_c_body, in_specs=ctx_specs,
                               name=f"attn_ctx_l{layer}", **ctx_out)(q, k_ctx, v_ctx)
        o_lat = pl.pallas_call(_lat_attn_c_body, in_specs=lat_specs,
                               name=f"attn_lat_l{layer}", **lat_out)(q, k_lat, v_lat, ck, cv)
    return o_ctx, o_lat


FFN_CHUNK = CTX_S
FFN_SLOT = FFN_CHUNK + 16
FFN_NCHUNK = TM_FFN // FFN_CHUNK


def _ffn_body(*refs, n_h, n_out, mixer):
    h_refs, refs = refs[:n_h], refs[n_h:]
    if mixer:
        (oc_ref, ol_ref, wo_ref), refs = refs[:3], refs[3:]
    (mod_ref, g_ref, wg_ref, wv_ref, bg_ref, bv_ref, kg_ref, kv_ref, cg_ref, cv_ref, wd_ref) = refs[:11]
    out_refs, refs = refs[11:11 + n_out], refs[11 + n_out:]
    xn_scr, acc_scr, ug_scr, uv_scr, wg_scr, wv_scr, wd_scr = refs
    i = pl.program_id(0)
    f = pl.program_id(1)
    is_lat = i >= N_CTX // TM_FFN
    out_cases = [(None, out_refs[0])] if n_out == 1 else [(jnp.logical_not(is_lat), out_refs[0]),
                                                           (is_lat, out_refs[1])]

    def guarded(cond, extra, fn):
        pl.when(extra if cond is None else jnp.logical_and(extra, cond))(fn)

    def load_h():
        if n_h == 1:
            return h_refs[0][...]
        return jnp.where(is_lat, h_refs[1][...], h_refs[0][...])

    def start(out_ref):
        mod = mod_ref[...]
        x = load_h()
        if mixer:
            heads = jnp.where(is_lat, ol_ref[...], oc_ref[...])
            x = x + mod[:, 2 * D:3 * D] * jnp.dot(heads, wo_ref[...], preferred_element_type=F32)
            out_ref[...] = x
        xn_scr[...] = _norm_mod(x, g_ref[...], mod[:, 3 * D:4 * D], mod[:, 4 * D:5 * D]).astype(BF16)
        acc_scr[...] = jnp.zeros_like(acc_scr)

    for cond, out_ref in out_cases:
        guarded(cond, f == 0, functools.partial(start, out_ref))

    wg_scr[...] = wg_ref[...].astype(BF16)
    wv_scr[...] = wv_ref[...].astype(BF16)
    wd_scr[...] = wd_ref[...].astype(BF16)
    base = [8 + FFN_SLOT * k for k in range(FFN_NCHUNK)]

    for k in range(FFN_NCHUNK):
        xk = xn_scr[FFN_CHUNK * k:FFN_CHUNK * (k + 1), :]
        ug_scr[base[k]:base[k] + FFN_CHUNK, :] = jnp.dot(xk, wg_scr[...], preferred_element_type=F32)
        uv_scr[base[k]:base[k] + FFN_CHUNK, :] = jnp.dot(xk, wv_scr[...], preferred_element_type=F32)

    for scr, b_ref in ((ug_scr, bg_ref), (uv_scr, bv_ref)):
        pad = -b_ref[...]
        tops = [pad] + [jnp.where(is_lat, scr[base[k - 1] + FFN_CHUNK - 1:base[k - 1] + FFN_CHUNK, :], pad)
                        for k in range(1, FFN_NCHUNK)]
        bots = [jnp.where(is_lat, scr[base[k + 1]:base[k + 1] + 1, :], pad)
                for k in range(FFN_NCHUNK - 1)] + [pad]
        for k in range(FFN_NCHUNK):
            scr[base[k] - 1:base[k], :] = tops[k]
            scr[base[k] + FFN_CHUNK:base[k] + FFN_CHUNK + 1, :] = bots[k]

    def conv(scr, b_ref, k_ref, c_ref, k):
        kk = k_ref[...]
        const = c_ref[...] + b_ref[...] * (kk[0:1] + kk[1:2] + kk[2:3])
        lo = base[k]
        return (const + kk[0:1] * scr[lo - 1:lo - 1 + FFN_CHUNK, :] + kk[1:2] * scr[lo:lo + FFN_CHUNK, :]
                + kk[2:3] * scr[lo + 1:lo + 1 + FFN_CHUNK, :])

    for k in range(FFN_NCHUNK):
        gate = conv(ug_scr, bg_ref, kg_ref, cg_ref, k)
        val = conv(uv_scr, bv_ref, kv_ref, cv_ref, k)
        a = (_silu(gate) * val).astype(BF16)
        rows = slice(FFN_CHUNK * k, FFN_CHUNK * (k + 1))
        acc_scr[rows, :] += jnp.dot(a, wd_scr[...], preferred_element_type=F32)

    def write(out_ref):
        x = out_ref[...] if mixer else load_h()
        out_ref[...] = x + mod_ref[:, 5 * D:6 * D] * acc_scr[...]

    for cond, out_ref in out_cases:
        guarded(cond, f == pl.num_programs(1) - 1, functools.partial(write, out_ref))


def _ffn(h, mods, layer, gain, w_up, b_up, conv_k, conv_b, w_down, split_out=False, mixer=None):
    nf = D_FF // TF
    b3 = b_up.reshape(N_LAYERS, 1, 2 * D_FF)
    c3 = conv_b.reshape(N_LAYERS, 1, 2 * D_FF)
    col = lambda off: (lambda i, f: (layer, 0, off + f))
    if isinstance(h, tuple):
        h_ops, h_specs = _token_pair(h, TM_FFN)
    else:
        h_ops, h_specs = (h,), [pl.BlockSpec((TM_FFN, D), lambda i, f: (i, 0))]
    if mixer is not None:
        o_ops, o_specs = _token_pair(mixer[:2], TM_FFN)
        h_ops = tuple(h_ops) + tuple(o_ops) + (mixer[2].astype(BF16),)
        h_specs = h_specs + o_specs + [pl.BlockSpec((D, D), lambda i, f: (0, 0))]
    if split_out:
        out_specs = _split_specs(TM_FFN, D)
        out_shape = [jax.ShapeDtypeStruct((N_CTX, D), F32), jax.ShapeDtypeStruct((N_LAT, D), F32)]
    else:
        out_specs = pl.BlockSpec((TM_FFN, D), lambda i, f: (i, 0))
        out_shape = jax.ShapeDtypeStruct((N_TOK, D), F32)
    slot_rows = 8 + FFN_SLOT * FFN_NCHUNK
    result = pl.pallas_call(
        functools.partial(_ffn_body, n_h=len(h_ops) - (3 if mixer is not None else 0),
                          n_out=2 if split_out else 1, mixer=mixer is not None),
        grid=(N_TOK // TM_FFN, nf),
        in_specs=h_specs + [
            _mod_spec(layer, TM_FFN),
            pl.BlockSpec((1, D), lambda i, f: (0, 0)),
            pl.BlockSpec((None, D, TF), col(0)),
            pl.BlockSpec((None, D, TF), col(nf)),
            pl.BlockSpec((None, 1, TF), col(0)),
            pl.BlockSpec((None, 1, TF), col(nf)),
            pl.BlockSpec((None, 3, TF), col(0)),
            pl.BlockSpec((None, 3, TF), col(nf)),
            pl.BlockSpec((None, 1, TF), col(0)),
            pl.BlockSpec((None, 1, TF), col(nf)),
            pl.BlockSpec((None, TF, D), lambda i, f: (layer, f, 0)),
        ],
        out_specs=out_specs,
        out_shape=out_shape,
        scratch_shapes=[pltpu.VMEM((TM_FFN, D), BF16), pltpu.VMEM((TM_FFN, D), F32),
                        pltpu.VMEM((slot_rows, TF), F32), pltpu.VMEM((slot_rows, TF), F32),
                        pltpu.VMEM((D, TF), BF16), pltpu.VMEM((D, TF), BF16), pltpu.VMEM((TF, D), BF16)],
        compiler_params=_params("arbitrary", "arbitrary"),
        name=f"ffn_l{layer}",
    )(*h_ops, mods.reshape(N_LAYERS, N_COND, 1, 6 * D), gain.reshape(1, D), w_up, w_up, b3, b3,
      conv_k, conv_k, c3, c3, w_down)
    return tuple(result) if split_out else result


S5_SEQ = 8
S5_CB = 4
CTX_SETS = CTX_B // S5_SEQ
CTX_CHUNKS = CTX_S // S5_L
LAT_CHUNKS = LAT_S // S5_L
S5_CTX_STEPS = CTX_SETS * CTX_CHUNKS // S5_CB
S5_LAT_STEPS = LAT_CHUNKS // S5_CB
S5_STEPS = S5_CTX_STEPS + S5_LAT_STEPS
S5_ROWS = (N_CTX + N_LAT) // S5_L
S5_CTX_ROWS = N_CTX // S5_L


def _s5_tile_specs():
    per = CTX_CHUNKS // S5_CB
    blk = (S5_SEQ, S5_CB, S5_L, D)

    def ctx_idx(n):
        m = jnp.minimum(n, S5_CTX_STEPS - 1)
        return (m // per, m % per, 0, 0)

    return blk, ctx_idx, (lambda n: (jnp.maximum(n - S5_CTX_STEPS, 0)))


def _s5_mod8(mods, layer):
    m = mods[layer]
    return jnp.stack([jnp.broadcast_to(m[0:1], (S5_SEQ, 6 * D)), m[LAT_ROW0:LAT_ROW0 + LAT_B]]).reshape(
        2, S5_SEQ, 1, 6 * D)


def _s5_in_body(hc_ref, hl_ref, mod_ref, g_ref, u_ref):
    is_lat = pl.program_id(0) >= S5_CTX_STEPS
    mod = mod_ref[...]
    gain = g_ref[...]
    for c in range(S5_CB):
        x = jnp.where(is_lat, hl_ref[:, c], hc_ref[:, c])
        u_ref[c] = _norm_mod(x, gain, mod[:, :, 0:D], mod[:, :, D:2 * D]).astype(BF16)


def _s5_in(h, mod8, gain):
    blk, ctx_idx, lat_idx = _s5_tile_specs()
    return pl.pallas_call(
        _s5_in_body,
        grid=(S5_STEPS,),
        in_specs=[pl.BlockSpec(blk, ctx_idx),
                  pl.BlockSpec(blk, lambda n: (N_CTX // N_LAT, lat_idx(n), 0, 0)),
                  pl.BlockSpec((None, S5_SEQ, 1, 6 * D), lambda n: (jnp.where(n >= S5_CTX_STEPS, 1, 0), 0, 0, 0)),
                  pl.BlockSpec((1, D), lambda n: (0, 0))],
        out_specs=pl.BlockSpec((None, S5_CB, S5_SEQ, S5_L, D), lambda n: (n, 0, 0, 0, 0)),
        out_shape=jax.ShapeDtypeStruct((S5_STEPS, S5_CB, S5_SEQ, S5_L, D), BF16),
        compiler_params=_params("arbitrary"),
        name="s5_in",
    )(h.reshape(N_TOK // CTX_S, CTX_CHUNKS, S5_L, D), h.reshape(N_TOK // LAT_S, LAT_CHUNKS, S5_L, D),
      mod8, gain.reshape(1, D))


def _s5_weights_body(lamc_re_ref, lamc_im_ref, lamr_re_ref, lamr_im_ref, ldt_ref,
                     bt_re_ref, bt_im_ref, ct_re_ref, ct_im_ref, w_ref, a_ref):
    blk = (lax.broadcasted_iota(jnp.int32, (1, 256), 1) // S5_L).astype(F32)
    lane256 = lax.broadcasted_iota(jnp.int32, (S5_C, 256), 1)
    hi = lax.Precision.HIGHEST
    krow = []
    st_rows = {}
    w_rows = {}
    for d in range(2):
        dt = jnp.exp(ldt_ref[d])
        lr = lamc_re_ref[d]
        li = lamc_im_ref[d]

        ang = (li * dt) * blk
        ph_r, ph_i = jnp.cos(ang), jnp.sin(ang)
        mag = jnp.exp((lr * dt) * blk)
        asc_r, asc_i = mag * ph_r, mag * ph_i
        top_r = ph_r[:, S5_L * (S5_L - 1):S5_L * (S5_L - 1) + 1]
        top_i = ph_i[:, S5_L * (S5_L - 1):S5_L * (S5_L - 1) + 1]
        mag = jnp.exp((lr * dt) * (15.0 - blk))
        dsc_r = mag * (top_r * ph_r + top_i * ph_i)
        dsc_i = mag * (top_i * ph_r - top_r * ph_i)
        ar = asc_r[:, S5_L:S5_L + 1]
        ai = asc_i[:, S5_L:S5_L + 1]
        den = lr * lr + li * li
        n_re = ar - 1.0
        f_re = (n_re * lr + ai * li) / den
        f_im = (ai * lr - n_re * li) / den
        btr = jnp.tile(bt_re_ref[d], (1, S5_L))
        bti = jnp.tile(bt_im_ref[d], (1, S5_L))
        bbr = f_re * btr - f_im * bti
        bbi = f_re * bti + f_im * btr
        e0r, e0i = (asc_r, asc_i) if d == 0 else (dsc_r, dsc_i)
        e1r = e0r * ar - e0i * ai
        e1i = e0r * ai + e0i * ar
        pr, pi = (dsc_r, dsc_i) if d == 0 else (asc_r, asc_i)
        st_rows[("re", d)] = pr * bbr - pi * bbi
        st_rows[("im", d)] = pr * bbi + pi * bbr
        ctr = jnp.tile(ct_re_ref[d].T, (1, S5_L))
        cti = jnp.tile(ct_im_ref[d].T, (1, S5_L))
        k_re = ctr * e0r - cti * e0i
        k_imneg = -(ctr * e0i + cti * e0r)
        w_rows[("re", d)] = ctr * e1r - cti * e1i
        w_rows[("im", d)] = -(ctr * e1i + cti * e1r)
        lrr = lamr_re_ref[d:d + 1, :]
        lir = lamr_im_ref[d:d + 1, :]
        magr = jnp.exp(lrr * dt)
        arr = magr * jnp.cos(lir * dt)
        air = magr * jnp.sin(lir * dt)
        denr = lrr * lrr + lir * lir
        nr = arr - 1.0
        fr = (nr * lrr + air * lir) / denr
        fi = (air * lrr - nr * lir) / denr
        b_re_t, b_im_t = bt_re_ref[d].T, bt_im_ref[d].T
        bbr_row = fr * b_re_t - fi * b_im_t
        bbi_row = fr * b_im_t + fi * b_re_t
        krow.append(jnp.dot(bbr_row, k_re, precision=hi, preferred_element_type=F32)
                    + jnp.dot(bbi_row, k_imneg, precision=hi, preferred_element_type=F32))
        mag16 = jnp.exp(lrr * dt * 16.0)
        a_ref[2 * d:2 * d + 1, :] = mag16 * jnp.cos(lir * dt * 16.0)
        a_ref[2 * d + 1:2 * d + 2, :] = mag16 * jnp.sin(lir * dt * 16.0)
    a_ref[4:8, :] = jnp.zeros((4, S5_P), F32)

    t_rows = []
    for s in range(S5_L):
        fwd = krow[0] if s == 0 else jnp.where(lane256 >= S5_C * s, pltpu.roll(krow[0], S5_C * s, 1), 0.0)
        bwd = krow[1] if s == S5_L - 1 else jnp.where(lane256 < S5_C * (s + 1),
                                                      pltpu.roll(krow[1], S5_C * (s + 1), 1), 0.0)
        t_rows.append(fwd + bwd)
    order = [("re", 0), ("re", 1), ("im", 0), ("im", 1)]
    w_ref[0:256, :] = jnp.concatenate(t_rows, axis=0).T.astype(BF16)
    w_ref[256:512, :] = jnp.concatenate([st_rows[o] for o in order], axis=0).astype(BF16)
    w_ref[512:768, :] = jnp.concatenate([w_rows[o] for o in order], axis=0).T.astype(BF16)


def _s5_weights(lam_re, lam_im, log_dt, b_re, b_im, c_re, c_im):
    col = lambda x: x.transpose(1, 0, 2).reshape(S5_G, 2, S5_P, 1)
    row = lambda x: x.transpose(1, 0, 2)
    ldt = log_dt.transpose(1, 0).reshape(S5_G, 2, 1, 1)
    per_group = lambda x: x.transpose(1, 0, 2, 3)
    g4 = lambda *tail: pl.BlockSpec((None,) + tail, lambda g: (g,) + (0,) * len(tail))
    return pl.pallas_call(
        _s5_weights_body,
        grid=(S5_G,),
        in_specs=[g4(2, S5_P, 1), g4(2, S5_P, 1), g4(2, S5_P), g4(2, S5_P), g4(2, 1, 1),
                  g4(2, S5_P, S5_C), g4(2, S5_P, S5_C), g4(2, S5_C, S5_P), g4(2, S5_C, S5_P)],
        out_specs=[g4(768, 256), g4(8, S5_P)],
        out_shape=[jax.ShapeDtypeStruct((S5_G, 768, 256), BF16),
                   jax.ShapeDtypeStruct((S5_G, 8, S5_P), F32)],
        compiler_params=_params("arbitrary"),
        name="s5_weights",
    )(col(lam_re), col(lam_im), row(lam_re), row(lam_im), ldt,
      per_group(b_re), per_group(b_im), per_group(c_re), per_group(c_im))


def _s5_scan(d_scr, r0, a_re, a_im, init_re, init_im, hin_scr, n_chunks):
    lane = lax.broadcasted_iota(jnp.int32, (S5_SEQ, LANES), 1)
    fwd = lane < S5_P
    hr, hi = init_re, init_im
    for c in range(n_chunks):
        cf = slice(r0 + c * S5_SEQ, r0 + (c + 1) * S5_SEQ)
        cb = slice(r0 + (n_chunks - 1 - c) * S5_SEQ, r0 + (n_chunks - c) * S5_SEQ)
        hin_scr[cf, 0:S5_P] = hr[:, 0:S5_P]
        hin_scr[cb, S5_P:LANES] = hr[:, S5_P:LANES]
        hin_scr[cf, LANES:LANES + S5_P] = hi[:, 0:S5_P]
        hin_scr[cb, LANES + S5_P:2 * LANES] = hi[:, S5_P:LANES]
        dr = jnp.where(fwd, d_scr[cf, 0:LANES], d_scr[cb, 0:LANES])
        di = jnp.where(fwd, d_scr[cf, LANES:2 * LANES], d_scr[cb, LANES:2 * LANES])
        hr, hi = hr * a_re - hi * a_im + dr, hr * a_im + hi * a_re + di
    return hr, hi


def _s5_core_body(ut_ref, w_ref, a_ref, s0_ref, yt_ref, fin_ref, ts_scr, d_scr, hin_scr):
    a_re = a_ref[0:1, :]
    a_im = a_ref[1:2, :]
    rhs = ut_ref[...].reshape(S5_L * S5_C, S5_ROWS)
    ts_scr[...] = jnp.dot(w_ref[0:512, :], rhs, preferred_element_type=F32)
    d_scr[...] = ts_scr[256:512, :].T
    zero = jnp.zeros((S5_SEQ, LANES), F32)
    for hb in range(CTX_SETS):
        fr, fi = _s5_scan(d_scr, hb * CTX_CHUNKS * S5_SEQ, a_re, a_im, zero, zero, hin_scr, CTX_CHUNKS)
        fin_ref[S5_SEQ * hb:S5_SEQ * (hb + 1), 0:LANES] = fr
        fin_ref[S5_SEQ * hb:S5_SEQ * (hb + 1), LANES:2 * LANES] = fi
    _s5_scan(d_scr, S5_CTX_ROWS, a_re, a_im, s0_ref[:, 0:LANES], s0_ref[:, LANES:2 * LANES], hin_scr, LAT_CHUNKS)
    y = ts_scr[0:256, :] + lax.dot_general(w_ref[512:768, :], hin_scr[...].astype(BF16),
                                           (((1,), (1,)), ((), ())), preferred_element_type=F32)
    yt_ref[...] = y.reshape(S5_L, S5_C, S5_ROWS)


def _s5_core(ut, wall, avec, s0):
    g3 = lambda a, b: pl.BlockSpec((None, a, b), lambda g: (g, 0, 0))
    tok = pl.BlockSpec((S5_L, S5_C, S5_ROWS), lambda g: (0, g, 0))
    return pl.pallas_call(
        _s5_core_body,
        grid=(S5_G,),
        in_specs=[tok, g3(768, 256), g3(8, LANES), g3(LAT_B, 256)],
        out_specs=[tok, g3(CTX_B, 256)],
        out_shape=[jax.ShapeDtypeStruct((S5_L, D, S5_ROWS), F32),
                   jax.ShapeDtypeStruct((S5_G, CTX_B, 256), F32)],
        scratch_shapes=[pltpu.VMEM((512, S5_ROWS), F32), pltpu.VMEM((S5_ROWS, 256), F32),
                        pltpu.VMEM((S5_ROWS, 256), F32)],
        compiler_params=_params("arbitrary"),
        name="s5_core",
    )(ut, wall, avec, s0)


def _gelu_tanh(x):
    return 0.5 * x * (1.0 + jnp.tanh(math.sqrt(2.0 / math.pi) * (x + 0.044715 * (x * x * x))))


def _s5_out_body(hc_ref, hl_ref, y_ref, mod_ref, g_ref, dskip_ref, w_ref, oc_ref, ol_ref, w_scr):
    n = pl.program_id(0)
    is_lat = n >= S5_CTX_STEPS

    @pl.when(n == 0)
    def _():
        w_scr[...] = w_ref[...].astype(BF16)

    mod = mod_ref[...]
    gain = g_ref[...]
    dskip = dskip_ref[...]
    hs, gs = [], []
    for c in range(S5_CB):
        h = jnp.where(is_lat, hl_ref[:, c], hc_ref[:, c])
        u = _norm_mod(h, gain, mod[:, :, 0:D], mod[:, :, D:2 * D])
        y = u * dskip + y_ref[c]
        hs.append(h)
        gs.append(_gelu_tanh(y).astype(BF16).reshape(S5_SEQ * S5_L, D))
    hh = jnp.dot(jnp.concatenate(gs, axis=0), w_scr[...], preferred_element_type=F32)
    rows = S5_SEQ * S5_L
    for c in range(S5_CB):
        blk = hh[rows * c:rows * (c + 1)]
        mix = (blk[:, 0:D] * jax.nn.sigmoid(blk[:, D:2 * D])).reshape(S5_SEQ, S5_L, D)
        out = hs[c] + mod[:, :, 2 * D:3 * D] * mix

        @pl.when(is_lat)
        def _():
            ol_ref[:, c] = out

        @pl.when(jnp.logical_not(is_lat))
        def _():
            oc_ref[:, c] = out


def _s5_out(h, y, mod8, gain, d_skip, w_glu):
    blk, ctx_idx, lat_idx = _s5_tile_specs()
    h_ctx, h_lat = pl.pallas_call(
        _s5_out_body,
        grid=(S5_STEPS,),
        in_specs=[pl.BlockSpec(blk, ctx_idx),
                  pl.BlockSpec(blk, lambda n: (N_CTX // N_LAT, lat_idx(n), 0, 0)),
                  pl.BlockSpec((None, S5_CB, S5_SEQ, S5_L, D), lambda n: (n, 0, 0, 0, 0)),
                  pl.BlockSpec((None, S5_SEQ, 1, 6 * D), lambda n: (jnp.where(n >= S5_CTX_STEPS, 1, 0), 0, 0, 0)),
                  pl.BlockSpec((1, D), lambda n: (0, 0)),
                  pl.BlockSpec((1, D), lambda n: (0, 0)),
                  pl.BlockSpec((D, 2 * D), lambda n: (0, 0))],
        out_specs=[pl.BlockSpec(blk, ctx_idx),
                   pl.BlockSpec(blk, lambda n: (0, lat_idx(n), 0, 0))],
        out_shape=[jax.ShapeDtypeStruct((CTX_B, CTX_CHUNKS, S5_L, D), F32),
                   jax.ShapeDtypeStruct((LAT_B, LAT_CHUNKS, S5_L, D), F32)],
        scratch_shapes=[pltpu.VMEM((D, 2 * D), BF16)],
        compiler_params=_params("arbitrary"),
        name="s5_out",
    )(h.reshape(N_TOK // CTX_S, CTX_CHUNKS, S5_L, D), h.reshape(N_TOK // LAT_S, LAT_CHUNKS, S5_L, D),
      y, mod8, gain.reshape(1, D), d_skip.reshape(1, D), w_glu)
    return h_ctx.reshape(N_CTX, D), h_lat.reshape(N_LAT, D)


def _s5_mixer(h, mods, layer, gain, state, lam_re, lam_im, log_dt, b_re, b_im, c_re, c_im, d_skip, w_glu):
    mod8 = _s5_mod8(mods, layer)
    u = _s5_in(h, mod8, gain)
    ut = u.reshape(S5_ROWS, S5_L, D).transpose(1, 2, 0)
    wall, avec = _s5_weights(lam_re, lam_im, log_dt, b_re, b_im, c_re, c_im)
    a2 = jnp.stack([jnp.concatenate([avec[:, 0], avec[:, 2]], axis=-1),
                    jnp.concatenate([avec[:, 1], avec[:, 3]], axis=-1)], axis=1)
    a2 = jnp.pad(a2, ((0, 0), (0, 6), (0, 0)))
    s0 = state.transpose(3, 0, 2, 1, 4).reshape(S5_G, LAT_B, 4 * S5_P)
    yt, fin = _s5_core(ut, wall, a2, s0)
    y = yt.transpose(2, 0, 1).reshape(S5_STEPS, S5_CB, S5_SEQ, S5_L, D)
    new_state = fin.reshape(S5_G, CTX_B, 2, 2, S5_P).transpose(1, 3, 2, 0, 4)
    return _s5_out(h, y, mod8, gain, d_skip, w_glu), new_state


def kernel(x_prompt, x_sample, cache_l0_k, cache_l0_v, state_l1, cache_l2_k, cache_l2_v, cache_l3_k, cache_l3_v, c, c_ctx, norm1, norm2, w_mod, b_mod, w_up, b_up, conv_k, conv_b, w_down, l0_w_qkv, l0_q_norm, l0_k_norm, l0_sink, l0_w_o, l1_lam_re, l1_lam_im, l1_log_dt, l1_b_re, l1_b_im, l1_c_re, l1_c_im, l1_d_skip, l1_w_glu, l2_w_qkv, l2_q_norm, l2_k_norm, l2_w_o, l3_w_qkv, l3_q_norm, l3_k_norm, l3_sink, l3_w_o):
    h = (x_prompt.reshape(N_CTX, D), x_sample.reshape(N_LAT, D))
    cond = jnp.concatenate([c_ctx[None, :], jnp.zeros((LAT_ROW0 - 1, D), F32), c], axis=0)
    mods = _modulation(cond, w_mod, b_mod)

    attn_layers = {
        0: (l0_w_qkv, l0_q_norm, l0_k_norm, l0_sink, l0_w_o, cache_l0_k, cache_l0_v, 16, 4, 64),
        2: (l2_w_qkv, l2_q_norm, l2_k_norm, None, l2_w_o, cache_l2_k, cache_l2_v, 8, 4, 128),
        3: (l3_w_qkv, l3_q_norm, l3_k_norm, l3_sink, l3_w_o, cache_l3_k, cache_l3_v, 16, 4, 64),
    }
    new_kv = {}
    new_state = None
    for layer in range(N_LAYERS):
        if layer in attn_layers:
            w_qkv, q_norm, k_norm, sink, w_o, ck, cv, n_heads, n_kv, dh = attn_layers[layer]
            q, k_ctx, k_lat, v_ctx, v_lat = _qkv_proj(h, mods, layer, norm1[layer], w_qkv, q_norm, k_norm,
                                                      n_heads, n_kv, dh)
            new_kv[layer] = (k_ctx.reshape(CTX_B, CTX_S, n_kv, dh), v_ctx.reshape(CTX_B, CTX_S, n_kv, dh))
            mixer = _attention(q, k_ctx, k_lat, v_ctx, v_lat, ck, cv, sink, layer) + (w_o,)
        else:
            mixer = None
            h, new_state = _s5_mixer(h, mods, layer, norm1[layer], state_l1, l1_lam_re, l1_lam_im, l1_log_dt,
                                     l1_b_re, l1_b_im, l1_c_re, l1_c_im, l1_d_skip, l1_w_glu)
        h = _ffn(h, mods, layer, norm2[layer], w_up, b_up, conv_k, conv_b, w_down,
                 split_out=layer == N_LAYERS - 1, mixer=mixer)

    y_prompt = h[0].reshape(CTX_B, CTX_S, D)
    y_sample = h[1].reshape(LAT_B, LAT_S, D)
    return (y_prompt, y_sample, new_kv[0][0], new_kv[0][1], new_state,
            new_kv[2][0], new_kv[2][1], new_kv[3][0], new_kv[3][1])
```

```python
import functools
import math

import jax
import jax.numpy as jnp
import numpy as np
from jax import lax
from jax.experimental import pallas as pl
from jax.experimental.pallas import tpu as pltpu

F32 = jnp.float32
BF16 = jnp.bfloat16

D = 1024
N_LAYERS = 4
CTX_B, CTX_S = 32, 256
LAT_B, LAT_S = 8, 1024
PAST = 512
N_CTX = CTX_B * CTX_S
N_LAT = LAT_B * LAT_S
N_TOK = N_CTX + N_LAT
GRID_W = 64
WINDOW = 128
ROPE_THETA = 10000.0
EPS = 1e-6
D_FF = 2816
N_COND = 16
LAT_ROW0 = 8

S5_G = 64
S5_C = 16
S5_P = 64
S5_L = 16

VMEM_LIMIT = 56 * 1024 * 1024
LANES = 128
NEG_BIG = -1e30
LOG2E = math.log2(math.e)

TM = 512
TM_FFN = 1024
TF = 256
TQ = 256
ATT_RB = 32
ATT_KEYS = 1024
STAGE_RB = 128
QKV_RB = 64


def _params(*sem):
    return pltpu.CompilerParams(dimension_semantics=sem, vmem_limit_bytes=VMEM_LIMIT)


def _cond_row(i, tm):
    n_ctx_tiles = N_CTX // tm
    return jnp.where(i < n_ctx_tiles, 0, LAT_ROW0 + (i - n_ctx_tiles) // (LAT_S // tm))


def _norm_mod(x, gain, shift, scale):
    ms = jnp.mean(x * x, axis=-1, keepdims=True)
    return (x * lax.rsqrt(ms + EPS) * gain) * (1.0 + scale) + shift


def _silu(x):
    return x * jax.nn.sigmoid(x)


def _mod_body(cond_ref, w_ref, b_ref, o_ref):
    s = _silu(cond_ref[...]).astype(BF16)
    o_ref[...] = jnp.dot(s, w_ref[...].astype(BF16), preferred_element_type=F32) + b_ref[...]


def _modulation(cond, w_mod, b_mod):
    tn = 1536
    return pl.pallas_call(
        _mod_body,
        grid=(N_LAYERS, 6 * D // tn),
        in_specs=[
            pl.BlockSpec((N_COND, D), lambda l, n: (0, 0)),
            pl.BlockSpec((None, D, tn), lambda l, n: (l, 0, n)),
            pl.BlockSpec((None, 1, tn), lambda l, n: (l, 0, n)),
        ],
        out_specs=pl.BlockSpec((None, N_COND, tn), lambda l, n: (l, 0, n)),
        out_shape=jax.ShapeDtypeStruct((N_LAYERS, N_COND, 6 * D), F32),
        compiler_params=_params("arbitrary", "arbitrary"),
        name="modulation",
    )(cond, w_mod, b_mod.reshape(N_LAYERS, 1, 6 * D))


def _mod_spec(layer, tm):
    return pl.BlockSpec((None, None, 1, 6 * D), lambda i, *_: (layer, _cond_row(i, tm), 0, 0))


def _token_pair(h, tm):
    n_ctx_tiles = N_CTX // tm
    h_ctx, h_lat, lat_off = (h[0], h[1], 0) if isinstance(h, tuple) else (h, h, n_ctx_tiles)
    specs = [pl.BlockSpec((tm, D), lambda i, *_: (jnp.minimum(i, n_ctx_tiles - 1), 0)),
             pl.BlockSpec((tm, D), lambda i, *_: (jnp.maximum(i - n_ctx_tiles, 0) + lat_off, 0))]
    return (h_ctx, h_lat), specs


def _split_specs(tm, width):
    n_ctx_tiles = N_CTX // tm
    return [pl.BlockSpec((tm, width), lambda i, *_: (jnp.minimum(i, n_ctx_tiles - 1), 0)),
            pl.BlockSpec((tm, width), lambda i, *_: (jnp.maximum(i - n_ctx_tiles, 0), 0))]


def _qkv_body(hc_ref, hl_ref, mod_ref, g_ref, w_ref, qn_ref, kn_ref, bd_ref, cos_ref, sin_ref,
              q_ref, kc_ref, kl_ref, vc_ref, vl_ref, w_scr, xn_scr, qkv_scr, sq_scr, ss_scr, *, nq, nk, dh):
    is_lat = pl.program_id(0) >= N_CTX // TM

    @pl.when(pl.program_id(0) == 0)
    def _():
        w_scr[...] = w_ref[...].astype(BF16)

    mod = mod_ref[...]
    row_blocks = [slice(r, r + QKV_RB) for r in range(0, TM, QKV_RB)]
    for rb in row_blocks:
        h = jnp.where(is_lat, hl_ref[rb, :], hc_ref[rb, :])
        xn_scr[rb, :] = _norm_mod(h, g_ref[...], mod[:, 0:D], mod[:, D:2 * D]).astype(BF16)
    qkv_scr[...] = jnp.dot(xn_scr[...], w_scr[...], preferred_element_type=F32)

    bd = bd_ref[...]
    quarter = dh // 4
    lane = lax.broadcasted_iota(jnp.int32, (QKV_RB, LANES), 1)
    first = (lane % (2 * quarter)) < quarter
    inv_dh = 1.0 / dh

    def head_norm_rope(c0, gain, out_ref, o0):
        for rb in row_blocks:
            z = qkv_scr[rb, c0:c0 + 256]
            sq_scr[rb, :] = (z * z).astype(BF16)
        ss_scr[...] = jnp.dot(sq_scr[...], bd, preferred_element_type=F32)
        for rb in row_blocks:
            zn = qkv_scr[rb, c0:c0 + 256] * lax.rsqrt(ss_scr[rb, :] * inv_dh + EPS)
            for j in range(2):
                zz = zn[:, LANES * j:LANES * (j + 1)] * gain
                partner = jnp.where(first, pltpu.roll(zz, LANES - quarter, 1), pltpu.roll(zz, quarter, 1))
                zz = zz * cos_ref[rb, :] + partner * sin_ref[rb, :]
                out_ref[rb, o0 + LANES * j:o0 + LANES * (j + 1)] = zz.astype(out_ref.dtype)

    qgain = qn_ref[...] * (dh ** -0.5 * LOG2E)
    kgain = kn_ref[...]

    def finish(k_ref, v_ref):
        for c in range(nq // 256):
            head_norm_rope(256 * c, qgain, q_ref, 256 * c)
        for c in range(nk // 256):
            head_norm_rope(nq + 256 * c, kgain, k_ref, 256 * c)
        for rb in row_blocks:
            v_ref[rb, :] = qkv_scr[rb, nq + nk:]

    pl.when(is_lat)(functools.partial(finish, kl_ref, vl_ref))
    pl.when(jnp.logical_not(is_lat))(functools.partial(finish, kc_ref, vc_ref))


def _rope_tables(dh):
    half, quarter = dh // 2, dh // 4
    freqs = 1.0 / (ROPE_THETA ** (np.arange(quarter, dtype=np.float32) / quarter))
    pos = np.arange(LAT_S)
    row = (pos // GRID_W).astype(np.float32)
    col = (pos % GRID_W).astype(np.float32)
    ang_r = (row[:, None] * freqs[None, :]).astype(np.float32)
    ang_c = (col[:, None] * freqs[None, :]).astype(np.float32)
    cos = np.concatenate([np.cos(ang_r), np.cos(ang_r), np.cos(ang_c), np.cos(ang_c)], axis=1)
    sin = np.concatenate([-np.sin(ang_r), np.sin(ang_r), -np.sin(ang_c), np.sin(ang_c)], axis=1)
    reps = LANES // dh
    cos = np.tile(cos.astype(np.float32), (1, reps))
    sin = np.tile(sin.astype(np.float32), (1, reps))
    cos = np.concatenate([np.ones_like(cos), cos], axis=0)
    sin = np.concatenate([np.zeros_like(sin), sin], axis=0)
    return jnp.asarray(cos), jnp.asarray(sin)


def _block_diag_ones(dh):
    idx = np.arange(256) // dh
    return jnp.asarray((idx[:, None] == idx[None, :]).astype(np.float32), dtype=BF16)


def _qkv_proj(h, mods, layer, gain, w_qkv, q_norm, k_norm, n_heads, n_kv, dh):
    nq, nk = n_heads * dh, n_kv * dh
    nqkv = nq + 2 * nk
    cos, sin = _rope_tables(dh)
    reps = LANES // dh
    n_ctx_tiles = N_CTX // TM
    lat_tiles = LAT_S // TM

    def rope_idx(i):
        return (jnp.where(i < n_ctx_tiles, 0, lat_tiles + (i - n_ctx_tiles) % lat_tiles), 0)

    h_ops, h_specs = _token_pair(h, TM)
    kv_specs = _split_specs(TM, nk)
    kv_shapes = [jax.ShapeDtypeStruct((N_CTX, nk), F32), jax.ShapeDtypeStruct((N_LAT, nk), F32)]
    return pl.pallas_call(
        functools.partial(_qkv_body, nq=nq, nk=nk, dh=dh),
        grid=(N_TOK // TM,),
        in_specs=h_specs + [
            _mod_spec(layer, TM),
            pl.BlockSpec((1, D), lambda i: (0, 0)),
            pl.BlockSpec((D, nqkv), lambda i: (0, 0)),
            pl.BlockSpec((1, LANES), lambda i: (0, 0)),
            pl.BlockSpec((1, LANES), lambda i: (0, 0)),
            pl.BlockSpec((256, 256), lambda i: (0, 0)),
            pl.BlockSpec((TM, LANES), rope_idx),
            pl.BlockSpec((TM, LANES), rope_idx),
        ],
        out_specs=[pl.BlockSpec((TM, nq), lambda i: (i, 0))] + kv_specs + kv_specs,
        out_shape=[jax.ShapeDtypeStruct((N_TOK, nq), BF16)] + kv_shapes + kv_shapes,
        scratch_shapes=[pltpu.VMEM((D, nqkv), BF16), pltpu.VMEM((TM, D), BF16), pltpu.VMEM((TM, nqkv), F32),
                        pltpu.VMEM((TM, 256), BF16),
                        pltpu.VMEM((TM, 256), F32)],
        compiler_params=_params("arbitrary"),
        name=f"qkv_l{layer}",
    )(*h_ops, mods.reshape(N_LAYERS, N_COND, 1, 6 * D), gain.reshape(1, D), w_qkv,
      jnp.tile(q_norm, reps).reshape(1, LANES), jnp.tile(k_norm, reps).reshape(1, LANES),
      _block_diag_ones(dh), cos, sin)


def _attend(q, segs, sink_of, tq, scr):
    m_rows = q.shape[0]
    if not scr:
        assert all(bias is None for _, _, bias in segs)
        scores = [lax.dot_general(q, k, (((1,), (1,)), ((), ())), preferred_element_type=F32) for k, _, _ in segs]
        m = functools.reduce(jnp.maximum, [jnp.max(s, axis=-1, keepdims=True) for s in scores])
        head = lax.broadcasted_iota(jnp.int32, (m_rows, 1), 0) // tq
        sink = None
        if sink_of(0) is not None:
            sink = functools.reduce(lambda acc, r: jnp.where(head >= r, sink_of(r * tq), acc),
                                    range(1, m_rows // tq), jnp.full((m_rows, 1), sink_of(0), F32))
            m = jnp.maximum(m, sink)
        den = None if sink is None else jnp.exp2(sink - m)
        acc = None
        for s, (_, v, _) in zip(scores, segs):
            p = jnp.exp2(s - m)
            ps = jnp.sum(p, axis=-1, keepdims=True)
            pv = jnp.dot(p.astype(BF16), v, preferred_element_type=F32)
            den = ps if den is None else den + ps
            acc = pv if acc is None else acc + pv
        return acc / den
    s_scr, p_scr, r_scr = scr
    cols = []
    col = 0
    for k, _, _ in segs:
        t = k.shape[0]
        s_scr[0:m_rows, col:col + t] = lax.dot_general(q, k, (((1,), (1,)), ((), ())), preferred_element_type=F32)
        cols.append((col, t))
        col += t
    rows = ATT_RB * max(1, ATT_KEYS // col)
    for r0 in range(0, m_rows, rows):
        rb = slice(r0, r0 + rows)
        parts = []
        for (c0, t), (_, _, bias) in zip(cols, segs):
            s = s_scr[rb, c0:c0 + t]
            if bias is not None:
                s = s + bias[r0 % tq:r0 % tq + rows, :]
            parts.append(s)
        m = functools.reduce(jnp.maximum, [jnp.max(s, axis=-1, keepdims=True) for s in parts])
        sink = sink_of(r0)
        if sink is not None:
            m = jnp.maximum(m, sink)
        den = None if sink is None else jnp.exp2(sink - m)
        for (c0, t), s in zip(cols, parts):
            p = jnp.exp2(s - m)
            ps = jnp.sum(p, axis=-1, keepdims=True)
            den = ps if den is None else den + ps
            p_scr[rb, c0:c0 + t] = p.astype(BF16)
        r_scr[rb, :] = jnp.broadcast_to(den, (rows, LANES))
    acc = None
    for (c0, t), (_, v, _) in zip(cols, segs):
        pv = jnp.dot(p_scr[0:m_rows, c0:c0 + t], v, preferred_element_type=F32)
        acc = pv if acc is None else acc + pv
    return acc / r_scr[0:m_rows, :]


def _dup_halves(x, kv):
    lane = lax.broadcasted_iota(jnp.int32, x.shape, 1)
    r = pltpu.roll(x, 64, 1)
    lo = lane < 64
    return (jnp.where(lo, x, r) if kv % 2 == 0 else jnp.where(lo, r, x)).astype(BF16)


def _dup_segment(k, v, kv):
    c = LANES * (kv // 2)
    return _dup_halves(k[:, c:c + LANES], kv), _dup_halves(v[:, c:c + LANES], kv)


def _stage_rows(dst_ref, rows, fn):
    for kv in range(4):
        for r in range(0, rows, STAGE_RB):
            dst_ref[kv, r:r + STAGE_RB, :] = fn(slice(r, r + STAGE_RB), kv)


def _attend_heads64(q_ref, o_ref, sink_ref, segs_of, biases, tq, scr):
    lane = lax.broadcasted_iota(jnp.int32, (tq, LANES), 1)
    lo = lane < 64
    for kv in range(4):
        segs = [(k, v, b) for (k, v), b in zip(segs_of(kv), biases)]
        parts = []
        for pair in range(2):
            j = 2 * kv + pair
            qp = q_ref[:, LANES * j:LANES * (j + 1)]
            zero = jnp.zeros_like(qp)
            parts += [jnp.where(lo, qp, zero), jnp.where(lo, zero, qp)]
        sinks = [sink_ref[4 * kv + r] * LOG2E for r in range(4)]
        out = _attend(jnp.concatenate(parts, axis=0), segs, lambda r0: sinks[r0 // tq], tq, scr)
        for pair in range(2):
            j = 2 * kv + pair
            a = out[(2 * pair) * tq:(2 * pair + 1) * tq]
            b = out[(2 * pair + 1) * tq:(2 * pair + 2) * tq]
            o_ref[:, LANES * j:LANES * (j + 1)] = jnp.where(lo, a, b).astype(BF16)


def _attend_heads128(q_ref, o_ref, segs_of, tq, scr):
    for kv in range(4):
        segs = [(k, v, None) for k, v in segs_of(kv)]
        q = jnp.concatenate([q_ref[:, LANES * (2 * kv):LANES * (2 * kv + 1)],
                             q_ref[:, LANES * (2 * kv + 1):LANES * (2 * kv + 2)]], axis=0)
        out = _attend(q, segs, lambda r0: None, tq, scr)
        o_ref[:, LANES * (2 * kv):LANES * (2 * kv + 1)] = out[:tq].astype(BF16)
        o_ref[:, LANES * (2 * kv + 1):LANES * (2 * kv + 2)] = out[tq:].astype(BF16)


def _ctx_attn_a_body(sink_ref, q_ref, k_ref, v_ref, o_ref, *scr):
    _attend_heads64(q_ref, o_ref, sink_ref, lambda kv: [_dup_segment(k_ref[...], v_ref[...], kv)], [None],
                    CTX_S, scr)


def _cols128(ref, kv):
    return ref[:, LANES * kv:LANES * (kv + 1)].astype(BF16)


def _ctx_attn_c_body(q_ref, k_ref, v_ref, o_ref, *scr):
    _attend_heads128(q_ref, o_ref, lambda kv: [(_cols128(k_ref, kv), _cols128(v_ref, kv))], CTX_S, scr)


def _lat_attn_a_body(sink_ref, q_ref, k_ref, v_ref, ck_ref, cv_ref, o_ref, ck_scr, cv_scr, k_scr, v_scr,
                     bias_scr, *scr):
    qi = pl.program_id(1)

    @pl.when(qi == 0)
    def _():
        for src, dst, rows in ((ck_ref, ck_scr, PAST), (cv_ref, cv_scr, PAST),
                               (k_ref, k_scr, LAT_S), (v_ref, v_scr, LAT_S)):
            _stage_rows(dst, rows, lambda rb, kv, src=src: _dup_halves(
                src[rb, LANES * (kv // 2):LANES * (kv // 2 + 1)], kv))

    band = TQ + 2 * WINDOW
    ws = pl.multiple_of(jnp.clip(qi * TQ - WINDOW, 0, LAT_S - band), WINDOW)
    qpos = qi * TQ + lax.broadcasted_iota(jnp.int32, (TQ, band), 0)
    kpos = ws + lax.broadcasted_iota(jnp.int32, (TQ, band), 1)
    bias_scr[...] = jnp.where(jnp.abs(qpos - kpos) <= WINDOW, 0.0, NEG_BIG)
    segs_of = lambda kv: [(ck_scr[kv], cv_scr[kv]), (k_scr[kv, pl.ds(ws, band), :], v_scr[kv, pl.ds(ws, band), :])]
    _attend_heads64(q_ref, o_ref, sink_ref, segs_of, [None, bias_scr], TQ, scr)


def _lat_attn_c_body(q_ref, k_ref, v_ref, ck_ref, cv_ref, o_ref, ck_scr, cv_scr, k_scr, v_scr, *scr):
    @pl.when(pl.program_id(1) == 0)
    def _():
        for src, dst, rows in ((ck_ref, ck_scr, PAST), (cv_ref, cv_scr, PAST),
                               (k_ref, k_scr, LAT_S), (v_ref, v_scr, LAT_S)):
            _stage_rows(dst, rows, lambda rb, kv, src=src: src[rb, LANES * kv:LANES * (kv + 1)].astype(BF16))

    _attend_heads128(q_ref, o_ref, lambda kv: [(ck_scr[kv], cv_scr[kv]), (k_scr[kv], v_scr[kv])], TQ, scr)


_SMEM_SPEC = pl.BlockSpec(memory_space=pltpu.SMEM)


def _attention(q, k_ctx, k_lat, v_ctx, v_lat, cache_k, cache_v, sink, layer):
    nk = k_ctx.shape[1]
    ck = cache_k.reshape(LAT_B, PAST, nk)
    cv = cache_v.reshape(LAT_B, PAST, nk)
    ctx_specs = [
        pl.BlockSpec((CTX_S, D), lambda b: (b, 0)),
        pl.BlockSpec((CTX_S, nk), lambda b: (b, 0)),
        pl.BlockSpec((CTX_S, nk), lambda b: (b, 0)),
    ]
    qb = LAT_S // TQ
    lat_specs = [
        pl.BlockSpec((TQ, D), lambda b, i: (N_CTX // TQ + b * qb + i, 0)),
        pl.BlockSpec((LAT_S, nk), lambda b, i: (b, 0)),
        pl.BlockSpec((LAT_S, nk), lambda b, i: (b, 0)),
        pl.BlockSpec((None, PAST, nk), lambda b, i: (b, 0, 0)),
        pl.BlockSpec((None, PAST, nk), lambda b, i: (b, 0, 0)),
    ]
    heads_per_kv = D // nk
    band = TQ + 2 * WINDOW

    def softmax_scratch(m_rows, keys):
        return [pltpu.VMEM((m_rows, keys), F32), pltpu.VMEM((m_rows, keys), BF16), pltpu.VMEM((m_rows, LANES), F32)]

    lat_keys = PAST + (band if sink is not None else LAT_S)
    staged = [pltpu.VMEM((4, PAST, LANES), BF16), pltpu.VMEM((4, PAST, LANES), BF16),
              pltpu.VMEM((4, LAT_S, LANES), BF16), pltpu.VMEM((4, LAT_S, LANES), BF16)]
    if sink is not None:
        staged.append(pltpu.VMEM((TQ, band), F32))
    ctx_out = dict(out_specs=pl.BlockSpec((CTX_S, D), lambda b: (b, 0)),
                   out_shape=jax.ShapeDtypeStruct((N_CTX, D), BF16),
                   grid=(CTX_B,), compiler_params=_params("arbitrary"))
    lat_out = dict(out_specs=pl.BlockSpec((TQ, D), lambda b, i: (b * qb + i, 0)),
                   out_shape=jax.ShapeDtypeStruct((N_LAT, D), BF16),
                   scratch_shapes=staged + softmax_scratch(heads_per_kv * TQ, lat_keys),
                   grid=(LAT_B, qb), compiler_params=_params("arbitrary", "arbitrary"))
    if sink is not None:
        o_ctx = pl.pallas_call(_ctx_attn_a_body, in_specs=[_SMEM_SPEC] + ctx_specs,
                               name=f"attn_ctx_l{layer}", **ctx_out)(sink, q, k_ctx, v_ctx)
        o_lat = pl.pallas_call(_lat_attn_a_body, in_specs=[_SMEM_SPEC] + lat_specs,
                               name=f"attn_lat_l{layer}", **lat_out)(sink, q, k_lat, v_lat, ck, cv)
    else:
        o_ctx = pl.pallas_call(_ctx_attn_c_body, in_specs=ctx_specs,
                               name=f"attn_ctx_l{layer}", **ctx_out)(q, k_ctx, v_ctx)
        o_lat = pl.pallas_call(_lat_attn_c_body, in_specs=lat_specs,
                               name=f"attn_lat_l{layer}", **lat_out)(q, k_lat, v_lat, ck, cv)
    return o_ctx, o_lat


FFN_CHUNK = CTX_S
FFN_SLOT = FFN_CHUNK + 16
FFN_NCHUNK = TM_FFN // FFN_CHUNK


def _ffn_body(*refs, n_h, n_out, mixer):
    h_refs, refs = refs[:n_h], refs[n_h:]
    if mixer:
        (oc_ref, ol_ref, wo_ref), refs = refs[:3], refs[3:]
    (mod_ref, g_ref, wg_ref, wv_ref, bg_ref, bv_ref, kg_ref, kv_ref, cg_ref, cv_ref, wd_ref) = refs[:11]
    out_refs, refs = refs[11:11 + n_out], refs[11 + n_out:]
    xn_scr, acc_scr, ug_scr, uv_scr, wg_scr, wv_scr, wd_scr = refs
    i = pl.program_id(0)
    f = pl.program_id(1)
    is_lat = i >= N_CTX // TM_FFN
    out_cases = [(None, out_refs[0])] if n_out == 1 else [(jnp.logical_not(is_lat), out_refs[0]),
                                                           (is_lat, out_refs[1])]

    def guarded(cond, extra, fn):
        pl.when(extra if cond is None else jnp.logical_and(extra, cond))(fn)

    def load_h():
        if n_h == 1:
            return h_refs[0][...]
        return jnp.where(is_lat, h_refs[1][...], h_refs[0][...])

    def start(out_ref):
        mod = mod_ref[...]
        x = load_h()
        if mixer:
            heads = jnp.where(is_lat, ol_ref[...], oc_ref[...])
            x = x + mod[:, 2 * D:3 * D] * jnp.dot(heads, wo_ref[...], preferred_element_type=F32)
            out_ref[...] = x
        xn_scr[...] = _norm_mod(x, g_ref[...], mod[:, 3 * D:4 * D], mod[:, 4 * D:5 * D]).astype(BF16)
        acc_scr[...] = jnp.zeros_like(acc_scr)

    for cond, out_ref in out_cases:
        guarded(cond, f == 0, functools.partial(start, out_ref))

    wg_scr[...] = wg_ref[...].astype(BF16)
    wv_scr[...] = wv_ref[...].astype(BF16)
    wd_scr[...] = wd_ref[...].astype(BF16)
    base = [8 + FFN_SLOT * k for k in range(FFN_NCHUNK)]

    for k in range(FFN_NCHUNK):
        xk = xn_scr[FFN_CHUNK * k:FFN_CHUNK * (k + 1), :]
        ug_scr[base[k]:base[k] + FFN_CHUNK, :] = jnp.dot(xk, wg_scr[...], preferred_element_type=F32)
        uv_scr[base[k]:base[k] + FFN_CHUNK, :] = jnp.dot(xk, wv_scr[...], preferred_element_type=F32)

    for scr, b_ref in ((ug_scr, bg_ref), (uv_scr, bv_ref)):
        pad = -b_ref[...]
        tops = [pad] + [jnp.where(is_lat, scr[base[k - 1] + FFN_CHUNK - 1:base[k - 1] + FFN_CHUNK, :], pad)
                        for k in range(1, FFN_NCHUNK)]
        bots = [jnp.where(is_lat, scr[base[k + 1]:base[k + 1] + 1, :], pad)
                for k in range(FFN_NCHUNK - 1)] + [pad]
        for k in range(FFN_NCHUNK):
            scr[base[k] - 1:base[k], :] = tops[k]
            scr[base[k] + FFN_CHUNK:base[k] + FFN_CHUNK + 1, :] = bots[k]

    def conv(scr, b_ref, k_ref, c_ref, k):
        kk = k_ref[...]
        const = c_ref[...] + b_ref[...] * (kk[0:1] + kk[1:2] + kk[2:3])
        lo = base[k]
        return (const + kk[0:1] * scr[lo - 1:lo - 1 + FFN_CHUNK, :] + kk[1:2] * scr[lo:lo + FFN_CHUNK, :]
                + kk[2:3] * scr[lo + 1:lo + 1 + FFN_CHUNK, :])

    for k in range(FFN_NCHUNK):
        gate = conv(ug_scr, bg_ref, kg_ref, cg_ref, k)
        val = conv(uv_scr, bv_ref, kv_ref, cv_ref, k)
        a = (_silu(gate) * val).astype(BF16)
        rows = slice(FFN_CHUNK * k, FFN_CHUNK * (k + 1))
        acc_scr[rows, :] += jnp.dot(a, wd_scr[...], preferred_element_type=F32)

    def write(out_ref):
        x = out_ref[...] if mixer else load_h()
        out_ref[...] = x + mod_ref[:, 5 * D:6 * D] * acc_scr[...]

    for cond, out_ref in out_cases:
        guarded(cond, f == pl.num_programs(1) - 1, functools.partial(write, out_ref))


def _ffn(h, mods, layer, gain, w_up, b_up, conv_k, conv_b, w_down, split_out=False, mixer=None):
    nf = D_FF // TF
    b3 = b_up.reshape(N_LAYERS, 1, 2 * D_FF)
    c3 = conv_b.reshape(N_LAYERS, 1, 2 * D_FF)
    col = lambda off: (lambda i, f: (layer, 0, off + f))
    if isinstance(h, tuple):
        h_ops, h_specs = _token_pair(h, TM_FFN)
    else:
        h_ops, h_specs = (h,), [pl.BlockSpec((TM_FFN, D), lambda i, f: (i, 0))]
    if mixer is not None:
        o_ops, o_specs = _token_pair(mixer[:2], TM_FFN)
        h_ops = tuple(h_ops) + tuple(o_ops) + (mixer[2].astype(BF16),)
        h_specs = h_specs + o_specs + [pl.BlockSpec((D, D), lambda i, f: (0, 0))]
    if split_out:
        out_specs = _split_specs(TM_FFN, D)
        out_shape = [jax.ShapeDtypeStruct((N_CTX, D), F32), jax.ShapeDtypeStruct((N_LAT, D), F32)]
    else:
        out_specs = pl.BlockSpec((TM_FFN, D), lambda i, f: (i, 0))
        out_shape = jax.ShapeDtypeStruct((N_TOK, D), F32)
    slot_rows = 8 + FFN_SLOT * FFN_NCHUNK
    result = pl.pallas_call(
        functools.partial(_ffn_body, n_h=len(h_ops) - (3 if mixer is not None else 0),
                          n_out=2 if split_out else 1, mixer=mixer is not None),
        grid=(N_TOK // TM_FFN, nf),
        in_specs=h_specs + [
            _mod_spec(layer, TM_FFN),
            pl.BlockSpec((1, D), lambda i, f: (0, 0)),
            pl.BlockSpec((None, D, TF), col(0)),
            pl.BlockSpec((None, D, TF), col(nf)),
            pl.BlockSpec((None, 1, TF), col(0)),
            pl.BlockSpec((None, 1, TF), col(nf)),
            pl.BlockSpec((None, 3, TF), col(0)),
            pl.BlockSpec((None, 3, TF), col(nf)),
            pl.BlockSpec((None, 1, TF), col(0)),
            pl.BlockSpec((None, 1, TF), col(nf)),
            pl.BlockSpec((None, TF, D), lambda i, f: (layer, f, 0)),
        ],
        out_specs=out_specs,
        out_shape=out_shape,
        scratch_shapes=[pltpu.VMEM((TM_FFN, D), BF16), pltpu.VMEM((TM_FFN, D), F32),
                        pltpu.VMEM((slot_rows, TF), F32), pltpu.VMEM((slot_rows, TF), F32),
                        pltpu.VMEM((D, TF), BF16), pltpu.VMEM((D, TF), BF16), pltpu.VMEM((TF, D), BF16)],
        compiler_params=_params("arbitrary", "arbitrary"),
        name=f"ffn_l{layer}",
    )(*h_ops, mods.reshape(N_LAYERS, N_COND, 1, 6 * D), gain.reshape(1, D), w_up, w_up, b3, b3,
      conv_k, conv_k, c3, c3, w_down)
    return tuple(result) if split_out else result


S5_SEQ = 8
S5_CB = 4
CTX_SETS = CTX_B // S5_SEQ
CTX_CHUNKS = CTX_S // S5_L
LAT_CHUNKS = LAT_S // S5_L
S5_CTX_STEPS = CTX_SETS * CTX_CHUNKS // S5_CB
S5_LAT_STEPS = LAT_CHUNKS // S5_CB
S5_STEPS = S5_CTX_STEPS + S5_LAT_STEPS
S5_ROWS = (N_CTX + N_LAT) // S5_L
S5_CTX_ROWS = N_CTX // S5_L


def _s5_tile_specs():
    per = CTX_CHUNKS // S5_CB
    blk = (S5_SEQ, S5_CB, S5_L, D)

    def ctx_idx(n):
        m = jnp.minimum(n, S5_CTX_STEPS - 1)
        return (m // per, m % per, 0, 0)

    return blk, ctx_idx, (lambda n: (jnp.maximum(n - S5_CTX_STEPS, 0)))


def _s5_mod8(mods, layer):
    m = mods[layer]
    return jnp.stack([jnp.broadcast_to(m[0:1], (S5_SEQ, 6 * D)), m[LAT_ROW0:LAT_ROW0 + LAT_B]]).reshape(
        2, S5_SEQ, 1, 6 * D)


def _s5_in_body(hc_ref, hl_ref, mod_ref, g_ref, u_ref):
    is_lat = pl.program_id(0) >= S5_CTX_STEPS
    mod = mod_ref[...]
    gain = g_ref[...]
    for c in range(S5_CB):
        x = jnp.where(is_lat, hl_ref[:, c], hc_ref[:, c])
        u_ref[c] = _norm_mod(x, gain, mod[:, :, 0:D], mod[:, :, D:2 * D]).astype(BF16)


def _s5_in(h, mod8, gain):
    blk, ctx_idx, lat_idx = _s5_tile_specs()
    return pl.pallas_call(
        _s5_in_body,
        grid=(S5_STEPS,),
        in_specs=[pl.BlockSpec(blk, ctx_idx),
                  pl.BlockSpec(blk, lambda n: (N_CTX // N_LAT, lat_idx(n), 0, 0)),
                  pl.BlockSpec((None, S5_SEQ, 1, 6 * D), lambda n: (jnp.where(n >= S5_CTX_STEPS, 1, 0), 0, 0, 0)),
                  pl.BlockSpec((1, D), lambda n: (0, 0))],
        out_specs=pl.BlockSpec((None, S5_CB, S5_SEQ, S5_L, D), lambda n: (n, 0, 0, 0, 0)),
        out_shape=jax.ShapeDtypeStruct((S5_STEPS, S5_CB, S5_SEQ, S5_L, D), BF16),
        compiler_params=_params("arbitrary"),
        name="s5_in",
    )(h.reshape(N_TOK // CTX_S, CTX_CHUNKS, S5_L, D), h.reshape(N_TOK // LAT_S, LAT_CHUNKS, S5_L, D),
      mod8, gain.reshape(1, D))


def _s5_weights_body(lamc_re_ref, lamc_im_ref, lamr_re_ref, lamr_im_ref, ldt_ref,
                     bt_re_ref, bt_im_ref, ct_re_ref, ct_im_ref, w_ref, a_ref):
    blk = (lax.broadcasted_iota(jnp.int32, (1, 256), 1) // S5_L).astype(F32)
    lane256 = lax.broadcasted_iota(jnp.int32, (S5_C, 256), 1)
    hi = lax.Precision.HIGHEST
    krow = []
    st_rows = {}
    w_rows = {}
    for d in range(2):
        dt = jnp.exp(ldt_ref[d])
        lr = lamc_re_ref[d]
        li = lamc_im_ref[d]

        ang = (li * dt) * blk
        ph_r, ph_i = jnp.cos(ang), jnp.sin(ang)
        mag = jnp.exp((lr * dt) * blk)
        asc_r, asc_i = mag * ph_r, mag * ph_i
        top_r = ph_r[:, S5_L * (S5_L - 1):S5_L * (S5_L - 1) + 1]
        top_i = ph_i[:, S5_L * (S5_L - 1):S5_L * (S5_L - 1) + 1]
        mag = jnp.exp((lr * dt) * (15.0 - blk))
        dsc_r = mag * (top_r * ph_r + top_i * ph_i)
        dsc_i = mag * (top_i * ph_r - top_r * ph_i)
        ar = asc_r[:, S5_L:S5_L + 1]
        ai = asc_i[:, S5_L:S5_L + 1]
        den = lr * lr + li * li
        n_re = ar - 1.0
        f_re = (n_re * lr + ai * li) / den
        f_im = (ai * lr - n_re * li) / den
        btr = jnp.tile(bt_re_ref[d], (1, S5_L))
        bti = jnp.tile(bt_im_ref[d], (1, S5_L))
        bbr = f_re * btr - f_im * bti
        bbi = f_re * bti + f_im * btr
        e0r, e0i = (asc_r, asc_i) if d == 0 else (dsc_r, dsc_i)
        e1r = e0r * ar - e0i * ai
        e1i = e0r * ai + e0i * ar
        pr, pi = (dsc_r, dsc_i) if d == 0 else (asc_r, asc_i)
        st_rows[("re", d)] = pr * bbr - pi * bbi
        st_rows[("im", d)] = pr * bbi + pi * bbr
        ctr = jnp.tile(ct_re_ref[d].T, (1, S5_L))
        cti = jnp.tile(ct_im_ref[d].T, (1, S5_L))
        k_re = ctr * e0r - cti * e0i
        k_imneg = -(ctr * e0i + cti * e0r)
        w_rows[("re", d)] = ctr * e1r - cti * e1i
        w_rows[("im", d)] = -(ctr * e1i + cti * e1r)
        lrr = lamr_re_ref[d:d + 1, :]
        lir = lamr_im_ref[d:d + 1, :]
        magr = jnp.exp(lrr * dt)
        arr = magr * jnp.cos(lir * dt)
        air = magr * jnp.sin(lir * dt)
        denr = lrr * lrr + lir * lir
        nr = arr - 1.0
        fr = (nr * lrr + air * lir) / denr
        fi = (air * lrr - nr * lir) / denr
        b_re_t, b_im_t = bt_re_ref[d].T, bt_im_ref[d].T
        bbr_row = fr * b_re_t - fi * b_im_t
        bbi_row = fr * b_im_t + fi * b_re_t
        krow.append(jnp.dot(bbr_row, k_re, precision=hi, preferred_element_type=F32)
                    + jnp.dot(bbi_row, k_imneg, precision=hi, preferred_element_type=F32))
        mag16 = jnp.exp(lrr * dt * 16.0)
        a_ref[2 * d:2 * d + 1, :] = mag16 * jnp.cos(lir * dt * 16.0)
        a_ref[2 * d + 1:2 * d + 2, :] = mag16 * jnp.sin(lir * dt * 16.0)
    a_ref[4:8, :] = jnp.zeros((4, S5_P), F32)

    t_rows = []
    for s in range(S5_L):
        fwd = krow[0] if s == 0 else jnp.where(lane256 >= S5_C * s, pltpu.roll(krow[0], S5_C * s, 1), 0.0)
        bwd = krow[1] if s == S5_L - 1 else jnp.where(lane256 < S5_C * (s + 1),
                                                      pltpu.roll(krow[1], S5_C * (s + 1), 1), 0.0)
        t_rows.append(fwd + bwd)
    order = [("re", 0), ("re", 1), ("im", 0), ("im", 1)]
    w_ref[0:256, :] = jnp.concatenate(t_rows, axis=0).T.astype(BF16)
    w_ref[256:512, :] = jnp.concatenate([st_rows[o] for o in order], axis=0).astype(BF16)
    w_ref[512:768, :] = jnp.concatenate([w_rows[o] for o in order], axis=0).T.astype(BF16)


def _s5_weights(lam_re, lam_im, log_dt, b_re, b_im, c_re, c_im):
    col = lambda x: x.transpose(1, 0, 2).reshape(S5_G, 2, S5_P, 1)
    row = lambda x: x.transpose(1, 0, 2)
    ldt = log_dt.transpose(1, 0).reshape(S5_G, 2, 1, 1)
    per_group = lambda x: x.transpose(1, 0, 2, 3)
    g4 = lambda *tail: pl.BlockSpec((None,) + tail, lambda g: (g,) + (0,) * len(tail))
    return pl.pallas_call(
        _s5_weights_body,
        grid=(S5_G,),
        in_specs=[g4(2, S5_P, 1), g4(2, S5_P, 1), g4(2, S5_P), g4(2, S5_P), g4(2, 1, 1),
                  g4(2, S5_P, S5_C), g4(2, S5_P, S5_C), g4(2, S5_C, S5_P), g4(2, S5_C, S5_P)],
        out_specs=[g4(768, 256), g4(8, S5_P)],
        out_shape=[jax.ShapeDtypeStruct((S5_G, 768, 256), BF16),
                   jax.ShapeDtypeStruct((S5_G, 8, S5_P), F32)],
        compiler_params=_params("arbitrary"),
        name="s5_weights",
    )(col(lam_re), col(lam_im), row(lam_re), row(lam_im), ldt,
      per_group(b_re), per_group(b_im), per_group(c_re), per_group(c_im))


def _s5_scan(d_scr, r0, a_re, a_im, init_re, init_im, hin_scr, n_chunks):
    lane = lax.broadcasted_iota(jnp.int32, (S5_SEQ, LANES), 1)
    fwd = lane < S5_P
    hr, hi = init_re, init_im
    for c in range(n_chunks):
        cf = slice(r0 + c * S5_SEQ, r0 + (c + 1) * S5_SEQ)
        cb = slice(r0 + (n_chunks - 1 - c) * S5_SEQ, r0 + (n_chunks - c) * S5_SEQ)
        hin_scr[cf, 0:S5_P] = hr[:, 0:S5_P]
        hin_scr[cb, S5_P:LANES] = hr[:, S5_P:LANES]
        hin_scr[cf, LANES:LANES + S5_P] = hi[:, 0:S5_P]
        hin_scr[cb, LANES + S5_P:2 * LANES] = hi[:, S5_P:LANES]
        dr = jnp.where(fwd, d_scr[cf, 0:LANES], d_scr[cb, 0:LANES])
        di = jnp.where(fwd, d_scr[cf, LANES:2 * LANES], d_scr[cb, LANES:2 * LANES])
        hr, hi = hr * a_re - hi * a_im + dr, hr * a_im + hi * a_re + di
    return hr, hi


def _s5_core_body(ut_ref, w_ref, a_ref, s0_ref, yt_ref, fin_ref, ts_scr, d_scr, hin_scr):
    a_re = a_ref[0:1, :]
    a_im = a_ref[1:2, :]
    rhs = ut_ref[...].reshape(S5_L * S5_C, S5_ROWS)
    ts_scr[...] = jnp.dot(w_ref[0:512, :], rhs, preferred_element_type=F32)
    d_scr[...] = ts_scr[256:512, :].T
    zero = jnp.zeros((S5_SEQ, LANES), F32)
    for hb in range(CTX_SETS):
        fr, fi = _s5_scan(d_scr, hb * CTX_CHUNKS * S5_SEQ, a_re, a_im, zero, zero, hin_scr, CTX_CHUNKS)
        fin_ref[S5_SEQ * hb:S5_SEQ * (hb + 1), 0:LANES] = fr
        fin_ref[S5_SEQ * hb:S5_SEQ * (hb + 1), LANES:2 * LANES] = fi
    _s5_scan(d_scr, S5_CTX_ROWS, a_re, a_im, s0_ref[:, 0:LANES], s0_ref[:, LANES:2 * LANES], hin_scr, LAT_CHUNKS)
    y = ts_scr[0:256, :] + lax.dot_general(w_ref[512:768, :], hin_scr[...].astype(BF16),
                                           (((1,), (1,)), ((), ())), preferred_element_type=F32)
    yt_ref[...] = y.reshape(S5_L, S5_C, S5_ROWS)


def _s5_core(ut, wall, avec, s0):
    g3 = lambda a, b: pl.BlockSpec((None, a, b), lambda g: (g, 0, 0))
    tok = pl.BlockSpec((S5_L, S5_C, S5_ROWS), lambda g: (0, g, 0))
    return pl.pallas_call(
        _s5_core_body,
        grid=(S5_G,),
        in_specs=[tok, g3(768, 256), g3(8, LANES), g3(LAT_B, 256)],
        out_specs=[tok, g3(CTX_B, 256)],
        out_shape=[jax.ShapeDtypeStruct((S5_L, D, S5_ROWS), F32),
                   jax.ShapeDtypeStruct((S5_G, CTX_B, 256), F32)],
        scratch_shapes=[pltpu.VMEM((512, S5_ROWS), F32), pltpu.VMEM((S5_ROWS, 256), F32),
                        pltpu.VMEM((S5_ROWS, 256), F32)],
        compiler_params=_params("arbitrary"),
        name="s5_core",
    )(ut, wall, avec, s0)


def _gelu_tanh(x):
    return 0.5 * x * (1.0 + jnp.tanh(math.sqrt(2.0 / math.pi) * (x + 0.044715 * (x * x * x))))


def _s5_out_body(hc_ref, hl_ref, y_ref, mod_ref, g_ref, dskip_ref, w_ref, oc_ref, ol_ref, w_scr, out_scr):
    n = pl.program_id(0)
    is_lat = n >= S5_CTX_STEPS

    @pl.when(n == 0)
    def _():
        w_scr[...] = w_ref[...].astype(BF16)

    mod = mod_ref[...]
    gain = g_ref[...]
    dskip = dskip_ref[...]
    rows = S5_SEQ * S5_L
    for half in range(0, S5_CB, 2):
        hs, gs = [], []
        for c in (half, half + 1):
            h = jnp.where(is_lat, hl_ref[:, c], hc_ref[:, c])
            u = _norm_mod(h, gain, mod[:, :, 0:D], mod[:, :, D:2 * D])
            y = u * dskip + y_ref[c]
            hs.append(h)
            gs.append(_gelu_tanh(y).astype(BF16).reshape(rows, D))
        hh = jnp.dot(jnp.concatenate(gs, axis=0), w_scr[...], preferred_element_type=F32)
        for j, c in enumerate((half, half + 1)):
            blk = hh[rows * j:rows * (j + 1)]
            mix = (blk[:, 0:D] * jax.nn.sigmoid(blk[:, D:2 * D])).reshape(S5_SEQ, S5_L, D)
            out_scr[c] = hs[j] + mod[:, :, 2 * D:3 * D] * mix

    def emit(out_ref):
        for c in range(S5_CB):
            out_ref[:, c] = out_scr[c]

    pl.when(is_lat)(functools.partial(emit, ol_ref))
    pl.when(jnp.logical_not(is_lat))(functools.partial(emit, oc_ref))


def _s5_out(h, y, mod8, gain, d_skip, w_glu):
    blk, ctx_idx, lat_idx = _s5_tile_specs()
    h_ctx, h_lat = pl.pallas_call(
        _s5_out_body,
        grid=(S5_STEPS,),
        in_specs=[pl.BlockSpec(blk, ctx_idx),
                  pl.BlockSpec(blk, lambda n: (N_CTX // N_LAT, lat_idx(n), 0, 0)),
                  pl.BlockSpec((None, S5_CB, S5_SEQ, S5_L, D), lambda n: (n, 0, 0, 0, 0)),
                  pl.BlockSpec((None, S5_SEQ, 1, 6 * D), lambda n: (jnp.where(n >= S5_CTX_STEPS, 1, 0), 0, 0, 0)),
                  pl.BlockSpec((1, D), lambda n: (0, 0)),
                  pl.BlockSpec((1, D), lambda n: (0, 0)),
                  pl.BlockSpec((D, 2 * D), lambda n: (0, 0))],
        out_specs=[pl.BlockSpec(blk, ctx_idx),
                   pl.BlockSpec(blk, lambda n: (0, lat_idx(n), 0, 0))],
        out_shape=[jax.ShapeDtypeStruct((CTX_B, CTX_CHUNKS, S5_L, D), F32),
                   jax.ShapeDtypeStruct((LAT_B, LAT_CHUNKS, S5_L, D), F32)],
        scratch_shapes=[pltpu.VMEM((D, 2 * D), BF16), pltpu.VMEM((S5_CB, S5_SEQ, S5_L, D), F32)],
        compiler_params=_params("arbitrary"),
        name="s5_out",
    )(h.reshape(N_TOK // CTX_S, CTX_CHUNKS, S5_L, D), h.reshape(N_TOK // LAT_S, LAT_CHUNKS, S5_L, D),
      y, mod8, gain.reshape(1, D), d_skip.reshape(1, D), w_glu)
    return h_ctx.reshape(N_CTX, D), h_lat.reshape(N_LAT, D)


def _s5_mixer(h, mods, layer, gain, state, lam_re, lam_im, log_dt, b_re, b_im, c_re, c_im, d_skip, w_glu):
    mod8 = _s5_mod8(mods, layer)
    u = _s5_in(h, mod8, gain)
    ut = u.reshape(S5_ROWS, S5_L, D).transpose(1, 2, 0)
    wall, avec = _s5_weights(lam_re, lam_im, log_dt, b_re, b_im, c_re, c_im)
    a2 = jnp.stack([jnp.concatenate([avec[:, 0], avec[:, 2]], axis=-1),
                    jnp.concatenate([avec[:, 1], avec[:, 3]], axis=-1)], axis=1)
    a2 = jnp.pad(a2, ((0, 0), (0, 6), (0, 0)))
    s0 = state.transpose(3, 0, 2, 1, 4).reshape(S5_G, LAT_B, 4 * S5_P)
    yt, fin = _s5_core(ut, wall, a2, s0)
    y = yt.transpose(2, 0, 1).reshape(S5_STEPS, S5_CB, S5_SEQ, S5_L, D)
    new_state = fin.reshape(S5_G, CTX_B, 2, 2, S5_P).transpose(1, 3, 2, 0, 4)
    return _s5_out(h, y, mod8, gain, d_skip, w_glu), new_state


def kernel(x_prompt, x_sample, cache_l0_k, cache_l0_v, state_l1, cache_l2_k, cache_l2_v, cache_l3_k, cache_l3_v, c, c_ctx, norm1, norm2, w_mod, b_mod, w_up, b_up, conv_k, conv_b, w_down, l0_w_qkv, l0_q_norm, l0_k_norm, l0_sink, l0_w_o, l1_lam_re, l1_lam_im, l1_log_dt, l1_b_re, l1_b_im, l1_c_re, l1_c_im, l1_d_skip, l1_w_glu, l2_w_qkv, l2_q_norm, l2_k_norm, l2_w_o, l3_w_qkv, l3_q_norm, l3_k_norm, l3_sink, l3_w_o):
    h = (x_prompt.reshape(N_CTX, D), x_sample.reshape(N_LAT, D))
    cond = jnp.concatenate([c_ctx[None, :], jnp.zeros((LAT_ROW0 - 1, D), F32), c], axis=0)
    mods = _modulation(cond, w_mod, b_mod)

    attn_layers = {
        0: (l0_w_qkv, l0_q_norm, l0_k_norm, l0_sink, l0_w_o, cache_l0_k, cache_l0_v, 16, 4, 64),
        2: (l2_w_qkv, l2_q_norm, l2_k_norm, None, l2_w_o, cache_l2_k, cache_l2_v, 8, 4, 128),
        3: (l3_w_qkv, l3_q_norm, l3_k_norm, l3_sink, l3_w_o, cache_l3_k, cache_l3_v, 16, 4, 64),
    }
    new_kv = {}
    new_state = None
    for layer in range(N_LAYERS):
        if layer in attn_layers:
            w_qkv, q_norm, k_norm, sink, w_o, ck, cv, n_heads, n_kv, dh = attn_layers[layer]
            q, k_ctx, k_lat, v_ctx, v_lat = _qkv_proj(h, mods, layer, norm1[layer], w_qkv, q_norm, k_norm,
                                                      n_heads, n_kv, dh)
            new_kv[layer] = (k_ctx.reshape(CTX_B, CTX_S, n_kv, dh), v_ctx.reshape(CTX_B, CTX_S, n_kv, dh))
            mixer = _attention(q, k_ctx, k_lat, v_ctx, v_lat, ck, cv, sink, layer) + (w_o,)
        else:
            mixer = None
            h, new_state = _s5_mixer(h, mods, layer, norm1[layer], state_l1, l1_lam_re, l1_lam_im, l1_log_dt,
                                     l1_b_re, l1_b_im, l1_c_re, l1_c_im, l1_d_skip, l1_w_glu)
        h = _ffn(h, mods, layer, norm2[layer], w_up, b_up, conv_k, conv_b, w_down,
                 split_out=layer == N_LAYERS - 1, mixer=mixer)

    y_prompt = h[0].reshape(CTX_B, CTX_S, D)
    y_sample = h[1].reshape(LAT_B, LAT_S, D)
    return (y_prompt, y_sample, new_kv[0][0], new_kv[0][1], new_state,
            new_kv[2][0], new_kv[2][1], new_kv[3][0], new_kv[3][1])
```

```python
import functools
import math

import jax
import jax.numpy as jnp
import numpy as np
from jax import lax
from jax.experimental import pallas as pl
from jax.experimental.pallas import tpu as pltpu

F32 = jnp.float32
BF16 = jnp.bfloat16

D = 1024
N_LAYERS = 4
CTX_B, CTX_S = 32, 256
LAT_B, LAT_S = 8, 1024
PAST = 512
N_CTX = CTX_B * CTX_S
N_LAT = LAT_B * LAT_S
N_TOK = N_CTX + N_LAT
GRID_W = 64
WINDOW = 128
ROPE_THETA = 10000.0
EPS = 1e-6
D_FF = 2816
N_COND = 16
LAT_ROW0 = 8

S5_G = 64
S5_C = 16
S5_P = 64
S5_L = 16
S5_GB = 2

VMEM_LIMIT = 56 * 1024 * 1024
LANES = 128
NEG_BIG = -1e30
LOG2E = math.log2(math.e)

TM = 512
TM_FFN = 1024
TF = 256
TQ = 256
ATT_RB = 32
ATT_KEYS = 1024
STAGE_RB = 128
QKV_RB = 64


def _params(*sem):
    return pltpu.CompilerParams(dimension_semantics=sem, vmem_limit_bytes=VMEM_LIMIT)


def _cond_row(i, tm):
    n_ctx_tiles = N_CTX // tm
    return jnp.where(i < n_ctx_tiles, 0, LAT_ROW0 + (i - n_ctx_tiles) // (LAT_S // tm))


def _norm_mod(x, gain, shift, scale):
    ms = jnp.mean(x * x, axis=-1, keepdims=True)
    return (x * lax.rsqrt(ms + EPS) * gain) * (1.0 + scale) + shift


def _silu(x):
    return x * jax.nn.sigmoid(x)


def _mod_body(cond_ref, w_ref, b_ref, o_ref):
    s = _silu(cond_ref[...]).astype(BF16)
    o_ref[...] = jnp.dot(s, w_ref[...].astype(BF16), preferred_element_type=F32) + b_ref[...]


def _modulation(cond, w_mod, b_mod):
    tn = 1536
    return pl.pallas_call(
        _mod_body,
        grid=(N_LAYERS, 6 * D // tn),
        in_specs=[
            pl.BlockSpec((N_COND, D), lambda l, n: (0, 0)),
            pl.BlockSpec((None, D, tn), lambda l, n: (l, 0, n)),
            pl.BlockSpec((None, 1, tn), lambda l, n: (l, 0, n)),
        ],
        out_specs=pl.BlockSpec((None, N_COND, tn), lambda l, n: (l, 0, n)),
        out_shape=jax.ShapeDtypeStruct((N_LAYERS, N_COND, 6 * D), F32),
        compiler_params=_params("arbitrary", "arbitrary"),
        name="modulation",
    )(cond, w_mod, b_mod.reshape(N_LAYERS, 1, 6 * D))


def _mod_spec(layer, tm):
    return pl.BlockSpec((None, None, 1, 6 * D), lambda i, *_: (layer, _cond_row(i, tm), 0, 0))


def _token_pair(h, tm):
    n_ctx_tiles = N_CTX // tm
    h_ctx, h_lat, lat_off = (h[0], h[1], 0) if isinstance(h, tuple) else (h, h, n_ctx_tiles)
    specs = [pl.BlockSpec((tm, D), lambda i, *_: (jnp.minimum(i, n_ctx_tiles - 1), 0)),
             pl.BlockSpec((tm, D), lambda i, *_: (jnp.maximum(i - n_ctx_tiles, 0) + lat_off, 0))]
    return (h_ctx, h_lat), specs


def _split_specs(tm, width):
    n_ctx_tiles = N_CTX // tm
    return [pl.BlockSpec((tm, width), lambda i, *_: (jnp.minimum(i, n_ctx_tiles - 1), 0)),
            pl.BlockSpec((tm, width), lambda i, *_: (jnp.maximum(i - n_ctx_tiles, 0), 0))]


def _qkv_body(hc_ref, hl_ref, mod_ref, g_ref, w_ref, qn_ref, kn_ref, bd_ref, cos_ref, sin_ref,
              q_ref, kc_ref, kl_ref, vc_ref, vl_ref, w_scr, xn_scr, qkv_scr, sq_scr, ss_scr, *, nq, nk, dh):
    is_lat = pl.program_id(0) >= N_CTX // TM

    @pl.when(pl.program_id(0) == 0)
    def _():
        w_scr[...] = w_ref[...].astype(BF16)

    mod = mod_ref[...]
    row_blocks = [slice(r, r + QKV_RB) for r in range(0, TM, QKV_RB)]
    for rb in row_blocks:
        h = jnp.where(is_lat, hl_ref[rb, :], hc_ref[rb, :])
        xn_scr[rb, :] = _norm_mod(h, g_ref[...], mod[:, 0:D], mod[:, D:2 * D]).astype(BF16)
    qkv_scr[...] = jnp.dot(xn_scr[...], w_scr[...], preferred_element_type=F32)

    bd = bd_ref[...]
    quarter = dh // 4
    lane = lax.broadcasted_iota(jnp.int32, (QKV_RB, LANES), 1)
    first = (lane % (2 * quarter)) < quarter
    inv_dh = 1.0 / dh

    def head_norm_rope(c0, gain, out_ref, o0):
        for rb in row_blocks:
            z = qkv_scr[rb, c0:c0 + 256]
            sq_scr[rb, :] = (z * z).astype(BF16)
        ss_scr[...] = jnp.dot(sq_scr[...], bd, preferred_element_type=F32)
        for rb in row_blocks:
            zn = qkv_scr[rb, c0:c0 + 256] * lax.rsqrt(ss_scr[rb, :] * inv_dh + EPS)
            for j in range(2):
                zz = zn[:, LANES * j:LANES * (j + 1)] * gain
                partner = jnp.where(first, pltpu.roll(zz, LANES - quarter, 1), pltpu.roll(zz, quarter, 1))
                zz = zz * cos_ref[rb, :] + partner * sin_ref[rb, :]
                out_ref[rb, o0 + LANES * j:o0 + LANES * (j + 1)] = zz.astype(out_ref.dtype)

    qgain = qn_ref[...] * (dh ** -0.5 * LOG2E)
    kgain = kn_ref[...]

    def finish(k_ref, v_ref):
        for c in range(nq // 256):
            head_norm_rope(256 * c, qgain, q_ref, 256 * c)
        for c in range(nk // 256):
            head_norm_rope(nq + 256 * c, kgain, k_ref, 256 * c)
        for rb in row_blocks:
            v_ref[rb, :] = qkv_scr[rb, nq + nk:]

    pl.when(is_lat)(functools.partial(finish, kl_ref, vl_ref))
    pl.when(jnp.logical_not(is_lat))(functools.partial(finish, kc_ref, vc_ref))


def _rope_tables(dh):
    half, quarter = dh // 2, dh // 4
    freqs = 1.0 / (ROPE_THETA ** (np.arange(quarter, dtype=np.float32) / quarter))
    pos = np.arange(LAT_S)
    row = (pos // GRID_W).astype(np.float32)
    col = (pos % GRID_W).astype(np.float32)
    ang_r = (row[:, None] * freqs[None, :]).astype(np.float32)
    ang_c = (col[:, None] * freqs[None, :]).astype(np.float32)
    cos = np.concatenate([np.cos(ang_r), np.cos(ang_r), np.cos(ang_c), np.cos(ang_c)], axis=1)
    sin = np.concatenate([-np.sin(ang_r), np.sin(ang_r), -np.sin(ang_c), np.sin(ang_c)], axis=1)
    reps = LANES // dh
    cos = np.tile(cos.astype(np.float32), (1, reps))
    sin = np.tile(sin.astype(np.float32), (1, reps))
    cos = np.concatenate([np.ones_like(cos), cos], axis=0)
    sin = np.concatenate([np.zeros_like(sin), sin], axis=0)
    return jnp.asarray(cos), jnp.asarray(sin)


def _block_diag_ones(dh):
    idx = np.arange(256) // dh
    return jnp.asarray((idx[:, None] == idx[None, :]).astype(np.float32), dtype=BF16)


def _qkv_proj(h, mods, layer, gain, w_qkv, q_norm, k_norm, n_heads, n_kv, dh):
    nq, nk = n_heads * dh, n_kv * dh
    nqkv = nq + 2 * nk
    cos, sin = _rope_tables(dh)
    reps = LANES // dh
    n_ctx_tiles = N_CTX // TM
    lat_tiles = LAT_S // TM

    def rope_idx(i):
        return (jnp.where(i < n_ctx_tiles, 0, lat_tiles + (i - n_ctx_tiles) % lat_tiles), 0)

    h_ops, h_specs = _token_pair(h, TM)
    kv_specs = _split_specs(TM, nk)
    kv_shapes = [jax.ShapeDtypeStruct((N_CTX, nk), F32), jax.ShapeDtypeStruct((N_LAT, nk), F32)]
    return pl.pallas_call(
        functools.partial(_qkv_body, nq=nq, nk=nk, dh=dh),
        grid=(N_TOK // TM,),
        in_specs=h_specs + [
            _mod_spec(layer, TM),
            pl.BlockSpec((1, D), lambda i: (0, 0)),
            pl.BlockSpec((D, nqkv), lambda i: (0, 0)),
            pl.BlockSpec((1, LANES), lambda i: (0, 0)),
            pl.BlockSpec((1, LANES), lambda i: (0, 0)),
            pl.BlockSpec((256, 256), lambda i: (0, 0)),
            pl.BlockSpec((TM, LANES), rope_idx),
            pl.BlockSpec((TM, LANES), rope_idx),
        ],
        out_specs=[pl.BlockSpec((TM, nq), lambda i: (i, 0))] + kv_specs + kv_specs,
        out_shape=[jax.ShapeDtypeStruct((N_TOK, nq), BF16)] + kv_shapes + kv_shapes,
        scratch_shapes=[pltpu.VMEM((D, nqkv), BF16), pltpu.VMEM((TM, D), BF16), pltpu.VMEM((TM, nqkv), F32),
                        pltpu.VMEM((TM, 256), BF16),
                        pltpu.VMEM((TM, 256), F32)],
        compiler_params=_params("arbitrary"),
        name=f"qkv_l{layer}",
    )(*h_ops, mods.reshape(N_LAYERS, N_COND, 1, 6 * D), gain.reshape(1, D), w_qkv,
      jnp.tile(q_norm, reps).reshape(1, LANES), jnp.tile(k_norm, reps).reshape(1, LANES),
      _block_diag_ones(dh), cos, sin)


def _attend(q, segs, sink_of, tq, scr):
    m_rows = q.shape[0]
    if not scr:
        assert all(bias is None for _, _, bias in segs)
        scores = [lax.dot_general(q, k, (((1,), (1,)), ((), ())), preferred_element_type=F32) for k, _, _ in segs]
        m = functools.reduce(jnp.maximum, [jnp.max(s, axis=-1, keepdims=True) for s in scores])
        head = lax.broadcasted_iota(jnp.int32, (m_rows, 1), 0) // tq
        sink = None
        if sink_of(0) is not None:
            sink = functools.reduce(lambda acc, r: jnp.where(head >= r, sink_of(r * tq), acc),
                                    range(1, m_rows // tq), jnp.full((m_rows, 1), sink_of(0), F32))
            m = jnp.maximum(m, sink)
        den = None if sink is None else jnp.exp2(sink - m)
        acc = None
        for s, (_, v, _) in zip(scores, segs):
            p = jnp.exp2(s - m)
            ps = jnp.sum(p, axis=-1, keepdims=True)
            pv = jnp.dot(p.astype(BF16), v, preferred_element_type=F32)
            den = ps if den is None else den + ps
            acc = pv if acc is None else acc + pv
        return acc / den
    s_scr, p_scr, r_scr = scr
    cols = []
    col = 0
    for k, _, _ in segs:
        t = k.shape[0]
        s_scr[0:m_rows, col:col + t] = lax.dot_general(q, k, (((1,), (1,)), ((), ())), preferred_element_type=F32)
        cols.append((col, t))
        col += t
    rows = ATT_RB * max(1, ATT_KEYS // col)
    for r0 in range(0, m_rows, rows):
        rb = slice(r0, r0 + rows)
        parts = []
        for (c0, t), (_, _, bias) in zip(cols, segs):
            s = s_scr[rb, c0:c0 + t]
            if bias is not None:
                s = s + bias[r0 % tq:r0 % tq + rows, :]
            parts.append(s)
        m = functools.reduce(jnp.maximum, [jnp.max(s, axis=-1, keepdims=True) for s in parts])
        sink = sink_of(r0)
        if sink is not None:
            m = jnp.maximum(m, sink)
        den = None if sink is None else jnp.exp2(sink - m)
        for (c0, t), s in zip(cols, parts):
            p = jnp.exp2(s - m)
            ps = jnp.sum(p, axis=-1, keepdims=True)
            den = ps if den is None else den + ps
            p_scr[rb, c0:c0 + t] = p.astype(BF16)
        r_scr[rb, :] = jnp.broadcast_to(den, (rows, LANES))
    acc = None
    for (c0, t), (_, v, _) in zip(cols, segs):
        pv = jnp.dot(p_scr[0:m_rows, c0:c0 + t], v, preferred_element_type=F32)
        acc = pv if acc is None else acc + pv
    return acc / r_scr[0:m_rows, :]


def _dup_halves(x, kv):
    lane = lax.broadcasted_iota(jnp.int32, x.shape, 1)
    r = pltpu.roll(x, 64, 1)
    lo = lane < 64
    return (jnp.where(lo, x, r) if kv % 2 == 0 else jnp.where(lo, r, x)).astype(BF16)


def _dup_segment(k, v, kv):
    c = LANES * (kv // 2)
    return _dup_halves(k[:, c:c + LANES], kv), _dup_halves(v[:, c:c + LANES], kv)


def _stage_rows(dst_ref, rows, fn):
    for kv in range(4):
        for r in range(0, rows, STAGE_RB):
            dst_ref[kv, r:r + STAGE_RB, :] = fn(slice(r, r + STAGE_RB), kv)


def _attend_heads64(q_ref, o_ref, sink_ref, segs_of, biases, tq, scr):
    lane = lax.broadcasted_iota(jnp.int32, (tq, LANES), 1)
    lo = lane < 64
    for kv in range(4):
        segs = [(k, v, b) for (k, v), b in zip(segs_of(kv), biases)]
        parts = []
        for pair in range(2):
            j = 2 * kv + pair
            qp = q_ref[:, LANES * j:LANES * (j + 1)]
            zero = jnp.zeros_like(qp)
            parts += [jnp.where(lo, qp, zero), jnp.where(lo, zero, qp)]
        sinks = [sink_ref[4 * kv + r] * LOG2E for r in range(4)]
        out = _attend(jnp.concatenate(parts, axis=0), segs, lambda r0: sinks[r0 // tq], tq, scr)
        for pair in range(2):
            j = 2 * kv + pair
            a = out[(2 * pair) * tq:(2 * pair + 1) * tq]
            b = out[(2 * pair + 1) * tq:(2 * pair + 2) * tq]
            o_ref[:, LANES * j:LANES * (j + 1)] = jnp.where(lo, a, b).astype(BF16)


def _attend_heads128(q_ref, o_ref, segs_of, tq, scr):
    for kv in range(4):
        segs = [(k, v, None) for k, v in segs_of(kv)]
        q = jnp.concatenate([q_ref[:, LANES * (2 * kv):LANES * (2 * kv + 1)],
                             q_ref[:, LANES * (2 * kv + 1):LANES * (2 * kv + 2)]], axis=0)
        out = _attend(q, segs, lambda r0: None, tq, scr)
        o_ref[:, LANES * (2 * kv):LANES * (2 * kv + 1)] = out[:tq].astype(BF16)
        o_ref[:, LANES * (2 * kv + 1):LANES * (2 * kv + 2)] = out[tq:].astype(BF16)


def _ctx_attn_a_body(sink_ref, q_ref, k_ref, v_ref, o_ref, *scr):
    _attend_heads64(q_ref, o_ref, sink_ref, lambda kv: [_dup_segment(k_ref[...], v_ref[...], kv)], [None],
                    CTX_S, scr)


def _cols128(ref, kv):
    return ref[:, LANES * kv:LANES * (kv + 1)].astype(BF16)


def _ctx_attn_c_body(q_ref, k_ref, v_ref, o_ref, *scr):
    _attend_heads128(q_ref, o_ref, lambda kv: [(_cols128(k_ref, kv), _cols128(v_ref, kv))], CTX_S, scr)


def _lat_attn_a_body(sink_ref, q_ref, k_ref, v_ref, ck_ref, cv_ref, o_ref, ck_scr, cv_scr, k_scr, v_scr,
                     bias_scr, *scr):
    qi = pl.program_id(1)

    @pl.when(qi == 0)
    def _():
        for src, dst, rows in ((ck_ref, ck_scr, PAST), (cv_ref, cv_scr, PAST),
                               (k_ref, k_scr, LAT_S), (v_ref, v_scr, LAT_S)):
            _stage_rows(dst, rows, lambda rb, kv, src=src: _dup_halves(
                src[rb, LANES * (kv // 2):LANES * (kv // 2 + 1)], kv))

    band = TQ + 2 * WINDOW
    ws = pl.multiple_of(jnp.clip(qi * TQ - WINDOW, 0, LAT_S - band), WINDOW)
    qpos = qi * TQ + lax.broadcasted_iota(jnp.int32, (TQ, band), 0)
    kpos = ws + lax.broadcasted_iota(jnp.int32, (TQ, band), 1)
    bias_scr[...] = jnp.where(jnp.abs(qpos - kpos) <= WINDOW, 0.0, NEG_BIG)
    segs_of = lambda kv: [(ck_scr[kv], cv_scr[kv]), (k_scr[kv, pl.ds(ws, band), :], v_scr[kv, pl.ds(ws, band), :])]
    _attend_heads64(q_ref, o_ref, sink_ref, segs_of, [None, bias_scr], TQ, scr)


def _lat_attn_c_body(q_ref, k_ref, v_ref, ck_ref, cv_ref, o_ref, ck_scr, cv_scr, k_scr, v_scr, *scr):
    @pl.when(pl.program_id(1) == 0)
    def _():
        for src, dst, rows in ((ck_ref, ck_scr, PAST), (cv_ref, cv_scr, PAST),
                               (k_ref, k_scr, LAT_S), (v_ref, v_scr, LAT_S)):
            _stage_rows(dst, rows, lambda rb, kv, src=src: src[rb, LANES * kv:LANES * (kv + 1)].astype(BF16))

    _attend_heads128(q_ref, o_ref, lambda kv: [(ck_scr[kv], cv_scr[kv]), (k_scr[kv], v_scr[kv])], TQ, scr)


_SMEM_SPEC = pl.BlockSpec(memory_space=pltpu.SMEM)


def _attention(q, k_ctx, k_lat, v_ctx, v_lat, cache_k, cache_v, sink, layer):
    nk = k_ctx.shape[1]
    ck = cache_k.reshape(LAT_B, PAST, nk)
    cv = cache_v.reshape(LAT_B, PAST, nk)
    ctx_specs = [
        pl.BlockSpec((CTX_S, D), lambda b: (b, 0)),
        pl.BlockSpec((CTX_S, nk), lambda b: (b, 0)),
        pl.BlockSpec((CTX_S, nk), lambda b: (b, 0)),
    ]
    qb = LAT_S // TQ
    lat_specs = [
        pl.BlockSpec((TQ, D), lambda b, i: (N_CTX // TQ + b * qb + i, 0)),
        pl.BlockSpec((LAT_S, nk), lambda b, i: (b, 0)),
        pl.BlockSpec((LAT_S, nk), lambda b, i: (b, 0)),
        pl.BlockSpec((None, PAST, nk), lambda b, i: (b, 0, 0)),
        pl.BlockSpec((None, PAST, nk), lambda b, i: (b, 0, 0)),
    ]
    heads_per_kv = D // nk
    band = TQ + 2 * WINDOW

    def softmax_scratch(m_rows, keys):
        return [pltpu.VMEM((m_rows, keys), F32), pltpu.VMEM((m_rows, keys), BF16), pltpu.VMEM((m_rows, LANES), F32)]

    lat_keys = PAST + (band if sink is not None else LAT_S)
    staged = [pltpu.VMEM((4, PAST, LANES), BF16), pltpu.VMEM((4, PAST, LANES), BF16),
              pltpu.VMEM((4, LAT_S, LANES), BF16), pltpu.VMEM((4, LAT_S, LANES), BF16)]
    if sink is not None:
        staged.append(pltpu.VMEM((TQ, band), F32))
    ctx_out = dict(out_specs=pl.BlockSpec((CTX_S, D), lambda b: (b, 0)),
                   out_shape=jax.ShapeDtypeStruct((N_CTX, D), BF16),
                   grid=(CTX_B,), compiler_params=_params("arbitrary"))
    lat_out = dict(out_specs=pl.BlockSpec((TQ, D), lambda b, i: (b * qb + i, 0)),
                   out_shape=jax.ShapeDtypeStruct((N_LAT, D), BF16),
                   scratch_shapes=staged + softmax_scratch(heads_per_kv * TQ, lat_keys),
                   grid=(LAT_B, qb), compiler_params=_params("arbitrary", "arbitrary"))
    if sink is not None:
        o_ctx = pl.pallas_call(_ctx_attn_a_body, in_specs=[_SMEM_SPEC] + ctx_specs,
                               name=f"attn_ctx_l{layer}", **ctx_out)(sink, q, k_ctx, v_ctx)
        o_lat = pl.pallas_call(_lat_attn_a_body, in_specs=[_SMEM_SPEC] + lat_specs,
                               name=f"attn_lat_l{layer}", **lat_out)(sink, q, k_lat, v_lat, ck, cv)
    else:
        o_ctx = pl.pallas_call(_ctx_attn_c_body, in_specs=ctx_specs,
                               name=f"attn_ctx_l{layer}", **ctx_out)(q, k_ctx, v_ctx)
        o_lat = pl.pallas_call(_lat_attn_c_body, in_specs=lat_specs,
                               name=f"attn_lat_l{layer}", **lat_out)(q, k_lat, v_lat, ck, cv)
    return o_ctx, o_lat


FFN_CHUNK = CTX_S
FFN_SLOT = FFN_CHUNK + 16
FFN_NCHUNK = TM_FFN // FFN_CHUNK


def _ffn_body(*refs, n_h, n_out, mixer):
    h_refs, refs = refs[:n_h], refs[n_h:]
    if mixer:
        (oc_ref, ol_ref, wo_ref), refs = refs[:3], refs[3:]
    (mod_ref, g_ref, wg_ref, wv_ref, bg_ref, bv_ref, kg_ref, kv_ref, cg_ref, cv_ref, wd_ref) = refs[:11]
    out_refs, refs = refs[11:11 + n_out], refs[11 + n_out:]
    xn_scr, acc_scr, ug_scr, uv_scr, wg_scr, wv_scr, wd_scr = refs
    i = pl.program_id(0)
    f = pl.program_id(1)
    is_lat = i >= N_CTX // TM_FFN
    out_cases = [(None, out_refs[0])] if n_out == 1 else [(jnp.logical_not(is_lat), out_refs[0]),
                                                           (is_lat, out_refs[1])]

    def guarded(cond, extra, fn):
        pl.when(extra if cond is None else jnp.logical_and(extra, cond))(fn)

    def load_h():
        if n_h == 1:
            return h_refs[0][...]
        return jnp.where(is_lat, h_refs[1][...], h_refs[0][...])

    def start(out_ref):
        mod = mod_ref[...]
        x = load_h()
        if mixer:
            heads = jnp.where(is_lat, ol_ref[...], oc_ref[...])
            x = x + mod[:, 2 * D:3 * D] * jnp.dot(heads, wo_ref[...], preferred_element_type=F32)
            out_ref[...] = x
        xn_scr[...] = _norm_mod(x, g_ref[...], mod[:, 3 * D:4 * D], mod[:, 4 * D:5 * D]).astype(BF16)
        acc_scr[...] = jnp.zeros_like(acc_scr)

    for cond, out_ref in out_cases:
        guarded(cond, f == 0, functools.partial(start, out_ref))

    wg_scr[...] = wg_ref[...].astype(BF16)
    wv_scr[...] = wv_ref[...].astype(BF16)
    wd_scr[...] = wd_ref[...].astype(BF16)
    base = [8 + FFN_SLOT * k for k in range(FFN_NCHUNK)]

    for k in range(FFN_NCHUNK):
        xk = xn_scr[FFN_CHUNK * k:FFN_CHUNK * (k + 1), :]
        ug_scr[base[k]:base[k] + FFN_CHUNK, :] = jnp.dot(xk, wg_scr[...], preferred_element_type=F32)
        uv_scr[base[k]:base[k] + FFN_CHUNK, :] = jnp.dot(xk, wv_scr[...], preferred_element_type=F32)

    for scr, b_ref in ((ug_scr, bg_ref), (uv_scr, bv_ref)):
        pad = -b_ref[...]
        tops = [pad] + [jnp.where(is_lat, scr[base[k - 1] + FFN_CHUNK - 1:base[k - 1] + FFN_CHUNK, :], pad)
                        for k in range(1, FFN_NCHUNK)]
        bots = [jnp.where(is_lat, scr[base[k + 1]:base[k + 1] + 1, :], pad)
                for k in range(FFN_NCHUNK - 1)] + [pad]
        for k in range(FFN_NCHUNK):
            scr[base[k] - 1:base[k], :] = tops[k]
            scr[base[k] + FFN_CHUNK:base[k] + FFN_CHUNK + 1, :] = bots[k]

    def conv(scr, b_ref, k_ref, c_ref, k):
        kk = k_ref[...]
        const = c_ref[...] + b_ref[...] * (kk[0:1] + kk[1:2] + kk[2:3])
        lo = base[k]
        return (const + kk[0:1] * scr[lo - 1:lo - 1 + FFN_CHUNK, :] + kk[1:2] * scr[lo:lo + FFN_CHUNK, :]
                + kk[2:3] * scr[lo + 1:lo + 1 + FFN_CHUNK, :])

    for k in range(FFN_NCHUNK):
        gate = conv(ug_scr, bg_ref, kg_ref, cg_ref, k)
        val = conv(uv_scr, bv_ref, kv_ref, cv_ref, k)
        a = (_silu(gate) * val).astype(BF16)
        rows = slice(FFN_CHUNK * k, FFN_CHUNK * (k + 1))
        acc_scr[rows, :] += jnp.dot(a, wd_scr[...], preferred_element_type=F32)

    def write(out_ref):
        x = out_ref[...] if mixer else load_h()
        out_ref[...] = x + mod_ref[:, 5 * D:6 * D] * acc_scr[...]

    for cond, out_ref in out_cases:
        guarded(cond, f == pl.num_programs(1) - 1, functools.partial(write, out_ref))


def _ffn(h, mods, layer, gain, w_up, b_up, conv_k, conv_b, w_down, split_out=False, mixer=None):
    nf = D_FF // TF
    b3 = b_up.reshape(N_LAYERS, 1, 2 * D_FF)
    c3 = conv_b.reshape(N_LAYERS, 1, 2 * D_FF)
    col = lambda off: (lambda i, f: (layer, 0, off + f))
    if isinstance(h, tuple):
        h_ops, h_specs = _token_pair(h, TM_FFN)
    else:
        h_ops, h_specs = (h,), [pl.BlockSpec((TM_FFN, D), lambda i, f: (i, 0))]
    if mixer is not None:
        o_ops, o_specs = _token_pair(mixer[:2], TM_FFN)
        h_ops = tuple(h_ops) + tuple(o_ops) + (mixer[2].astype(BF16),)
        h_specs = h_specs + o_specs + [pl.BlockSpec((D, D), lambda i, f: (0, 0))]
    if split_out:
        out_specs = _split_specs(TM_FFN, D)
        out_shape = [jax.ShapeDtypeStruct((N_CTX, D), F32), jax.ShapeDtypeStruct((N_LAT, D), F32)]
    else:
        out_specs = pl.BlockSpec((TM_FFN, D), lambda i, f: (i, 0))
        out_shape = jax.ShapeDtypeStruct((N_TOK, D), F32)
    slot_rows = 8 + FFN_SLOT * FFN_NCHUNK
    result = pl.pallas_call(
        functools.partial(_ffn_body, n_h=len(h_ops) - (3 if mixer is not None else 0),
                          n_out=2 if split_out else 1, mixer=mixer is not None),
        grid=(N_TOK // TM_FFN, nf),
        in_specs=h_specs + [
            _mod_spec(layer, TM_FFN),
            pl.BlockSpec((1, D), lambda i, f: (0, 0)),
            pl.BlockSpec((None, D, TF), col(0)),
            pl.BlockSpec((None, D, TF), col(nf)),
            pl.BlockSpec((None, 1, TF), col(0)),
            pl.BlockSpec((None, 1, TF), col(nf)),
            pl.BlockSpec((None, 3, TF), col(0)),
            pl.BlockSpec((None, 3, TF), col(nf)),
            pl.BlockSpec((None, 1, TF), col(0)),
            pl.BlockSpec((None, 1, TF), col(nf)),
            pl.BlockSpec((None, TF, D), lambda i, f: (layer, f, 0)),
        ],
        out_specs=out_specs,
        out_shape=out_shape,
        scratch_shapes=[pltpu.VMEM((TM_FFN, D), BF16), pltpu.VMEM((TM_FFN, D), F32),
                        pltpu.VMEM((slot_rows, TF), F32), pltpu.VMEM((slot_rows, TF), F32),
                        pltpu.VMEM((D, TF), BF16), pltpu.VMEM((D, TF), BF16), pltpu.VMEM((TF, D), BF16)],
        compiler_params=_params("arbitrary", "arbitrary"),
        name=f"ffn_l{layer}",
    )(*h_ops, mods.reshape(N_LAYERS, N_COND, 1, 6 * D), gain.reshape(1, D), w_up, w_up, b3, b3,
      conv_k, conv_k, c3, c3, w_down)
    return tuple(result) if split_out else result


S5_SEQ = 8
S5_CB = 4
CTX_SETS = CTX_B // S5_SEQ
CTX_CHUNKS = CTX_S // S5_L
LAT_CHUNKS = LAT_S // S5_L
S5_CTX_STEPS = CTX_SETS * CTX_CHUNKS // S5_CB
S5_LAT_STEPS = LAT_CHUNKS // S5_CB
S5_STEPS = S5_CTX_STEPS + S5_LAT_STEPS
S5_ROWS = (N_CTX + N_LAT) // S5_L
S5_CTX_ROWS = N_CTX // S5_L


def _s5_tile_specs():
    per = CTX_CHUNKS // S5_CB
    blk = (S5_SEQ, S5_CB, S5_L, D)

    def ctx_idx(n):
        m = jnp.minimum(n, S5_CTX_STEPS - 1)
        return (m // per, m % per, 0, 0)

    return blk, ctx_idx, (lambda n: (jnp.maximum(n - S5_CTX_STEPS, 0)))


def _s5_mod8(mods, layer):
    m = mods[layer]
    return jnp.stack([jnp.broadcast_to(m[0:1], (S5_SEQ, 6 * D)), m[LAT_ROW0:LAT_ROW0 + LAT_B]]).reshape(
        2, S5_SEQ, 1, 6 * D)


def _s5_in_body(hc_ref, hl_ref, mod_ref, g_ref, u_ref):
    is_lat = pl.program_id(0) >= S5_CTX_STEPS
    mod = mod_ref[...]
    gain = g_ref[...]
    for c in range(S5_CB):
        x = jnp.where(is_lat, hl_ref[:, c], hc_ref[:, c])
        u_ref[c] = _norm_mod(x, gain, mod[:, :, 0:D], mod[:, :, D:2 * D]).astype(BF16)


def _s5_in(h, mod8, gain):
    blk, ctx_idx, lat_idx = _s5_tile_specs()
    return pl.pallas_call(
        _s5_in_body,
        grid=(S5_STEPS,),
        in_specs=[pl.BlockSpec(blk, ctx_idx),
                  pl.BlockSpec(blk, lambda n: (N_CTX // N_LAT, lat_idx(n), 0, 0)),
                  pl.BlockSpec((None, S5_SEQ, 1, 6 * D), lambda n: (jnp.where(n >= S5_CTX_STEPS, 1, 0), 0, 0, 0)),
                  pl.BlockSpec((1, D), lambda n: (0, 0))],
        out_specs=pl.BlockSpec((None, S5_CB, S5_SEQ, S5_L, D), lambda n: (n, 0, 0, 0, 0)),
        out_shape=jax.ShapeDtypeStruct((S5_STEPS, S5_CB, S5_SEQ, S5_L, D), BF16),
        compiler_params=_params("arbitrary"),
        name="s5_in",
    )(h.reshape(N_TOK // CTX_S, CTX_CHUNKS, S5_L, D), h.reshape(N_TOK // LAT_S, LAT_CHUNKS, S5_L, D),
      mod8, gain.reshape(1, D))


def _s5_weights_body(lamc_re_ref, lamc_im_ref, lamr_re_ref, lamr_im_ref, ldt_ref,
                     bt_re_ref, bt_im_ref, ct_re_ref, ct_im_ref, w_ref, a_ref):
    blk = (lax.broadcasted_iota(jnp.int32, (1, 256), 1) // S5_L).astype(F32)
    lane256 = lax.broadcasted_iota(jnp.int32, (S5_C, 256), 1)
    hi = lax.Precision.HIGHEST
    krow = []
    st_rows = {}
    w_rows = {}
    for d in range(2):
        dt = jnp.exp(ldt_ref[d])
        lr = lamc_re_ref[d]
        li = lamc_im_ref[d]

        ang = (li * dt) * blk
        ph_r, ph_i = jnp.cos(ang), jnp.sin(ang)
        mag = jnp.exp((lr * dt) * blk)
        asc_r, asc_i = mag * ph_r, mag * ph_i
        top_r = ph_r[:, S5_L * (S5_L - 1):S5_L * (S5_L - 1) + 1]
        top_i = ph_i[:, S5_L * (S5_L - 1):S5_L * (S5_L - 1) + 1]
        mag = jnp.exp((lr * dt) * (15.0 - blk))
        dsc_r = mag * (top_r * ph_r + top_i * ph_i)
        dsc_i = mag * (top_i * ph_r - top_r * ph_i)
        ar = asc_r[:, S5_L:S5_L + 1]
        ai = asc_i[:, S5_L:S5_L + 1]
        den = lr * lr + li * li
        n_re = ar - 1.0
        f_re = (n_re * lr + ai * li) / den
        f_im = (ai * lr - n_re * li) / den
        btr = jnp.tile(bt_re_ref[d], (1, S5_L))
        bti = jnp.tile(bt_im_ref[d], (1, S5_L))
        bbr = f_re * btr - f_im * bti
        bbi = f_re * bti + f_im * btr
        e0r, e0i = (asc_r, asc_i) if d == 0 else (dsc_r, dsc_i)
        e1r = e0r * ar - e0i * ai
        e1i = e0r * ai + e0i * ar
        pr, pi = (dsc_r, dsc_i) if d == 0 else (asc_r, asc_i)
        st_rows[("re", d)] = pr * bbr - pi * bbi
        st_rows[("im", d)] = pr * bbi + pi * bbr
        ctr = jnp.tile(ct_re_ref[d].T, (1, S5_L))
        cti = jnp.tile(ct_im_ref[d].T, (1, S5_L))
        k_re = ctr * e0r - cti * e0i
        k_imneg = -(ctr * e0i + cti * e0r)
        w_rows[("re", d)] = ctr * e1r - cti * e1i
        w_rows[("im", d)] = -(ctr * e1i + cti * e1r)
        lrr = lamr_re_ref[d:d + 1, :]
        lir = lamr_im_ref[d:d + 1, :]
        magr = jnp.exp(lrr * dt)
        arr = magr * jnp.cos(lir * dt)
        air = magr * jnp.sin(lir * dt)
        denr = lrr * lrr + lir * lir
        nr = arr - 1.0
        fr = (nr * lrr + air * lir) / denr
        fi = (air * lrr - nr * lir) / denr
        b_re_t, b_im_t = bt_re_ref[d].T, bt_im_ref[d].T
        bbr_row = fr * b_re_t - fi * b_im_t
        bbi_row = fr * b_im_t + fi * b_re_t
        krow.append(jnp.dot(bbr_row, k_re, precision=hi, preferred_element_type=F32)
                    + jnp.dot(bbi_row, k_imneg, precision=hi, preferred_element_type=F32))
        mag16 = jnp.exp(lrr * dt * 16.0)
        a_ref[2 * d:2 * d + 1, :] = mag16 * jnp.cos(lir * dt * 16.0)
        a_ref[2 * d + 1:2 * d + 2, :] = mag16 * jnp.sin(lir * dt * 16.0)
    a_ref[4:8, :] = jnp.zeros((4, S5_P), F32)

    t_rows = []
    for s in range(S5_L):
        fwd = krow[0] if s == 0 else jnp.where(lane256 >= S5_C * s, pltpu.roll(krow[0], S5_C * s, 1), 0.0)
        bwd = krow[1] if s == S5_L - 1 else jnp.where(lane256 < S5_C * (s + 1),
                                                      pltpu.roll(krow[1], S5_C * (s + 1), 1), 0.0)
        t_rows.append(fwd + bwd)
    order = [("re", 0), ("re", 1), ("im", 0), ("im", 1)]
    w_ref[0:256, :] = jnp.concatenate(t_rows, axis=0).T.astype(BF16)
    w_ref[256:512, :] = jnp.concatenate([st_rows[o] for o in order], axis=0).astype(BF16)
    w_ref[512:768, :] = jnp.concatenate([w_rows[o] for o in order], axis=0).T.astype(BF16)


def _s5_weights(lam_re, lam_im, log_dt, b_re, b_im, c_re, c_im):
    col = lambda x: x.transpose(1, 0, 2).reshape(S5_G, 2, S5_P, 1)
    row = lambda x: x.transpose(1, 0, 2)
    ldt = log_dt.transpose(1, 0).reshape(S5_G, 2, 1, 1)
    per_group = lambda x: x.transpose(1, 0, 2, 3)
    g4 = lambda *tail: pl.BlockSpec((None,) + tail, lambda g: (g,) + (0,) * len(tail))
    return pl.pallas_call(
        _s5_weights_body,
        grid=(S5_G,),
        in_specs=[g4(2, S5_P, 1), g4(2, S5_P, 1), g4(2, S5_P), g4(2, S5_P), g4(2, 1, 1),
                  g4(2, S5_P, S5_C), g4(2, S5_P, S5_C), g4(2, S5_C, S5_P), g4(2, S5_C, S5_P)],
        out_specs=[g4(768, 256), g4(8, S5_P)],
        out_shape=[jax.ShapeDtypeStruct((S5_G, 768, 256), BF16),
                   jax.ShapeDtypeStruct((S5_G, 8, S5_P), F32)],
        compiler_params=_params("arbitrary"),
        name="s5_weights",
    )(col(lam_re), col(lam_im), row(lam_re), row(lam_im), ldt,
      per_group(b_re), per_group(b_im), per_group(c_re), per_group(c_im))


def _s5_scan(d_scr, r0, a_re, a_im, init_re, init_im, hin_scr, n_chunks):
    lane = lax.broadcasted_iota(jnp.int32, (S5_SEQ, LANES), 1)
    fwd = lane < S5_P
    hr, hi = init_re, init_im
    for c in range(n_chunks):
        cf = slice(r0 + c * S5_SEQ, r0 + (c + 1) * S5_SEQ)
        cb = slice(r0 + (n_chunks - 1 - c) * S5_SEQ, r0 + (n_chunks - c) * S5_SEQ)
        hin_scr[cf, 0:S5_P] = hr[:, 0:S5_P]
        hin_scr[cb, S5_P:LANES] = hr[:, S5_P:LANES]
        hin_scr[cf, LANES:LANES + S5_P] = hi[:, 0:S5_P]
        hin_scr[cb, LANES + S5_P:2 * LANES] = hi[:, S5_P:LANES]
        dr = jnp.where(fwd, d_scr[cf, 0:LANES], d_scr[cb, 0:LANES])
        di = jnp.where(fwd, d_scr[cf, LANES:2 * LANES], d_scr[cb, LANES:2 * LANES])
        hr, hi = hr * a_re - hi * a_im + dr, hr * a_im + hi * a_re + di
    return hr, hi


def _s5_core_body(ut_ref, w_ref, a_ref, s0_ref, yt_ref, fin_ref, ts_scr, d_scr, hin_scr):
    for gi in range(S5_GB):
        a_re = a_ref[gi, 0:1, :]
        a_im = a_ref[gi, 1:2, :]
        ch = slice(S5_C * gi, S5_C * (gi + 1))
        rhs = ut_ref[:, ch, :].reshape(S5_L * S5_C, S5_ROWS)
        ts_scr[gi] = jnp.dot(w_ref[gi, 0:512, :], rhs, preferred_element_type=F32)
        d_scr[gi] = ts_scr[gi, 256:512, :].T
        zero = jnp.zeros((S5_SEQ, LANES), F32)
        for hb in range(CTX_SETS):
            fr, fi = _s5_scan(d_scr.at[gi], hb * CTX_CHUNKS * S5_SEQ, a_re, a_im, zero, zero, hin_scr.at[gi],
                              CTX_CHUNKS)
            fin_ref[gi, S5_SEQ * hb:S5_SEQ * (hb + 1), 0:LANES] = fr
            fin_ref[gi, S5_SEQ * hb:S5_SEQ * (hb + 1), LANES:2 * LANES] = fi
        _s5_scan(d_scr.at[gi], S5_CTX_ROWS, a_re, a_im, s0_ref[gi, :, 0:LANES], s0_ref[gi, :, LANES:2 * LANES],
                 hin_scr.at[gi], LAT_CHUNKS)
        y = ts_scr[gi, 0:256, :] + lax.dot_general(w_ref[gi, 512:768, :], hin_scr[gi].astype(BF16),
                                                   (((1,), (1,)), ((), ())), preferred_element_type=F32)
        yt_ref[:, ch, :] = y.reshape(S5_L, S5_C, S5_ROWS)


def _s5_core(ut, wall, avec, s0):
    g3 = lambda a, b: pl.BlockSpec((S5_GB, a, b), lambda g: (g, 0, 0))
    tok = pl.BlockSpec((S5_L, S5_GB * S5_C, S5_ROWS), lambda g: (0, g, 0))
    return pl.pallas_call(
        _s5_core_body,
        grid=(S5_G // S5_GB,),
        in_specs=[tok, g3(768, 256), g3(8, LANES), g3(LAT_B, 256)],
        out_specs=[tok, g3(CTX_B, 256)],
        out_shape=[jax.ShapeDtypeStruct((S5_L, D, S5_ROWS), F32),
                   jax.ShapeDtypeStruct((S5_G, CTX_B, 256), F32)],
        scratch_shapes=[pltpu.VMEM((S5_GB, 512, S5_ROWS), F32), pltpu.VMEM((S5_GB, S5_ROWS, 256), F32),
                        pltpu.VMEM((S5_GB, S5_ROWS, 256), F32)],
        compiler_params=_params("arbitrary"),
        name="s5_core",
    )(ut, wall, avec, s0)


def _gelu_tanh(x):
    return 0.5 * x * (1.0 + jnp.tanh(math.sqrt(2.0 / math.pi) * (x + 0.044715 * (x * x * x))))


def _s5_out_body(hc_ref, hl_ref, y_ref, mod_ref, g_ref, dskip_ref, w_ref, oc_ref, ol_ref, w_scr, out_scr):
    n = pl.program_id(0)
    is_lat = n >= S5_CTX_STEPS

    @pl.when(n == 0)
    def _():
        w_scr[...] = w_ref[...].astype(BF16)

    mod = mod_ref[...]
    gain = g_ref[...]
    dskip = dskip_ref[...]
    rows = S5_SEQ * S5_L
    for half in range(0, S5_CB, 2):
        hs, gs = [], []
        for c in (half, half + 1):
            h = jnp.where(is_lat, hl_ref[:, c], hc_ref[:, c])
            u = _norm_mod(h, gain, mod[:, :, 0:D], mod[:, :, D:2 * D])
            y = u * dskip + y_ref[c]
            hs.append(h)
            gs.append(_gelu_tanh(y).astype(BF16).reshape(rows, D))
        hh = jnp.dot(jnp.concatenate(gs, axis=0), w_scr[...], preferred_element_type=F32)
        for j, c in enumerate((half, half + 1)):
            blk = hh[rows * j:rows * (j + 1)]
            mix = (blk[:, 0:D] * jax.nn.sigmoid(blk[:, D:2 * D])).reshape(S5_SEQ, S5_L, D)
            out_scr[c] = hs[j] + mod[:, :, 2 * D:3 * D] * mix

    def emit(out_ref):
        for c in range(S5_CB):
            out_ref[:, c] = out_scr[c]

    pl.when(is_lat)(functools.partial(emit, ol_ref))
    pl.when(jnp.logical_not(is_lat))(functools.partial(emit, oc_ref))


def _s5_out(h, y, mod8, gain, d_skip, w_glu):
    blk, ctx_idx, lat_idx = _s5_tile_specs()
    h_ctx, h_lat = pl.pallas_call(
        _s5_out_body,
        grid=(S5_STEPS,),
        in_specs=[pl.BlockSpec(blk, ctx_idx),
                  pl.BlockSpec(blk, lambda n: (N_CTX // N_LAT, lat_idx(n), 0, 0)),
                  pl.BlockSpec((None, S5_CB, S5_SEQ, S5_L, D), lambda n: (n, 0, 0, 0, 0)),
                  pl.BlockSpec((None, S5_SEQ, 1, 6 * D), lambda n: (jnp.where(n >= S5_CTX_STEPS, 1, 0), 0, 0, 0)),
                  pl.BlockSpec((1, D), lambda n: (0, 0)),
                  pl.BlockSpec((1, D), lambda n: (0, 0)),
                  pl.BlockSpec((D, 2 * D), lambda n: (0, 0))],
        out_specs=[pl.BlockSpec(blk, ctx_idx),
                   pl.BlockSpec(blk, lambda n: (0, lat_idx(n), 0, 0))],
        out_shape=[jax.ShapeDtypeStruct((CTX_B, CTX_CHUNKS, S5_L, D), F32),
                   jax.ShapeDtypeStruct((LAT_B, LAT_CHUNKS, S5_L, D), F32)],
        scratch_shapes=[pltpu.VMEM((D, 2 * D), BF16), pltpu.VMEM((S5_CB, S5_SEQ, S5_L, D), F32)],
        compiler_params=_params("arbitrary"),
        name="s5_out",
    )(h.reshape(N_TOK // CTX_S, CTX_CHUNKS, S5_L, D), h.reshape(N_TOK // LAT_S, LAT_CHUNKS, S5_L, D),
      y, mod8, gain.reshape(1, D), d_skip.reshape(1, D), w_glu)
    return h_ctx.reshape(N_CTX, D), h_lat.reshape(N_LAT, D)


def _s5_mixer(h, mods, layer, gain, state, lam_re, lam_im, log_dt, b_re, b_im, c_re, c_im, d_skip, w_glu):
    mod8 = _s5_mod8(mods, layer)
    u = _s5_in(h, mod8, gain)
    ut = u.reshape(S5_ROWS, S5_L, D).transpose(1, 2, 0)
    wall, avec = _s5_weights(lam_re, lam_im, log_dt, b_re, b_im, c_re, c_im)
    a2 = jnp.stack([jnp.concatenate([avec[:, 0], avec[:, 2]], axis=-1),
                    jnp.concatenate([avec[:, 1], avec[:, 3]], axis=-1)], axis=1)
    a2 = jnp.pad(a2, ((0, 0), (0, 6), (0, 0)))
    s0 = state.transpose(3, 0, 2, 1, 4).reshape(S5_G, LAT_B, 4 * S5_P)
    yt, fin = _s5_core(ut, wall, a2, s0)
    y = yt.transpose(2, 0, 1).reshape(S5_STEPS, S5_CB, S5_SEQ, S5_L, D)
    new_state = fin.reshape(S5_G, CTX_B, 2, 2, S5_P).transpose(1, 3, 2, 0, 4)
    return _s5_out(h, y, mod8, gain, d_skip, w_glu), new_state


def kernel(x_prompt, x_sample, cache_l0_k, cache_l0_v, state_l1, cache_l2_k, cache_l2_v, cache_l3_k, cache_l3_v, c, c_ctx, norm1, norm2, w_mod, b_mod, w_up, b_up, conv_k, conv_b, w_down, l0_w_qkv, l0_q_norm, l0_k_norm, l0_sink, l0_w_o, l1_lam_re, l1_lam_im, l1_log_dt, l1_b_re, l1_b_im, l1_c_re, l1_c_im, l1_d_skip, l1_w_glu, l2_w_qkv, l2_q_norm, l2_k_norm, l2_w_o, l3_w_qkv, l3_q_norm, l3_k_norm, l3_sink, l3_w_o):
    h = (x_prompt.reshape(N_CTX, D), x_sample.reshape(N_LAT, D))
    cond = jnp.concatenate([c_ctx[None, :], jnp.zeros((LAT_ROW0 - 1, D), F32), c], axis=0)
    mods = _modulation(cond, w_mod, b_mod)

    attn_layers = {
        0: (l0_w_qkv, l0_q_norm, l0_k_norm, l0_sink, l0_w_o, cache_l0_k, cache_l0_v, 16, 4, 64),
        2: (l2_w_qkv, l2_q_norm, l2_k_norm, None, l2_w_o, cache_l2_k, cache_l2_v, 8, 4, 128),
        3: (l3_w_qkv, l3_q_norm, l3_k_norm, l3_sink, l3_w_o, cache_l3_k, cache_l3_v, 16, 4, 64),
    }
    new_kv = {}
    new_state = None
    for layer in range(N_LAYERS):
        if layer in attn_layers:
            w_qkv, q_norm, k_norm, sink, w_o, ck, cv, n_heads, n_kv, dh = attn_layers[layer]
            q, k_ctx, k_lat, v_ctx, v_lat = _qkv_proj(h, mods, layer, norm1[layer], w_qkv, q_norm, k_norm,
                                                      n_heads, n_kv, dh)
            new_kv[layer] = (k_ctx.reshape(CTX_B, CTX_S, n_kv, dh), v_ctx.reshape(CTX_B, CTX_S, n_kv, dh))
            mixer = _attention(q, k_ctx, k_lat, v_ctx, v_lat, ck, cv, sink, layer) + (w_o,)
        else:
            mixer = None
            h, new_state = _s5_mixer(h, mods, layer, norm1[layer], state_l1, l1_lam_re, l1_lam_im, l1_log_dt,
                                     l1_b_re, l1_b_im, l1_c_re, l1_c_im, l1_d_skip, l1_w_glu)
        h = _ffn(h, mods, layer, norm2[layer], w_up, b_up, conv_k, conv_b, w_down,
                 split_out=layer == N_LAYERS - 1, mixer=mixer)

    y_prompt = h[0].reshape(CTX_B, CTX_S, D)
    y_sample = h[1].reshape(LAT_B, LAT_S, D)
    return (y_prompt, y_sample, new_kv[0][0], new_kv[0][1], new_state,
            new_kv[2][0], new_kv[2][1], new_kv[3][0], new_kv[3][1])
```

```python
import functools
import math

import jax
import jax.numpy as jnp
import numpy as np
from jax import lax
from jax.experimental import pallas as pl
from jax.experimental.pallas import tpu as pltpu

F32 = jnp.float32
BF16 = jnp.bfloat16

D = 1024
N_LAYERS = 4
CTX_B, CTX_S = 32, 256
LAT_B, LAT_S = 8, 1024
PAST = 512
N_CTX = CTX_B * CTX_S
N_LAT = LAT_B * LAT_S
N_TOK = N_CTX + N_LAT
GRID_W = 64
WINDOW = 128
ROPE_THETA = 10000.0
EPS = 1e-6
D_FF = 2816
N_COND = 16
LAT_ROW0 = 8

S5_G = 64
S5_C = 16
S5_P = 64
S5_L = 16
S5_GB = 2

VMEM_LIMIT = 56 * 1024 * 1024
LANES = 128
NEG_BIG = -1e30
LOG2E = math.log2(math.e)

TM = 512
TM_FFN = 1024
TF = 256
TQ = 256
ATT_RB = 32
ATT_KEYS = 1024
STAGE_RB = 128
QKV_RB = 64


def _params(*sem):
    return pltpu.CompilerParams(dimension_semantics=sem, vmem_limit_bytes=VMEM_LIMIT)


def _cond_row(i, tm):
    n_ctx_tiles = N_CTX // tm
    return jnp.where(i < n_ctx_tiles, 0, LAT_ROW0 + (i - n_ctx_tiles) // (LAT_S // tm))


def _norm_mod(x, gain, shift, scale):
    ms = jnp.mean(x * x, axis=-1, keepdims=True)
    return (x * lax.rsqrt(ms + EPS) * gain) * (1.0 + scale) + shift


def _silu(x):
    return x * jax.nn.sigmoid(x)


def _mod_body(cond_ref, w_ref, b_ref, o_ref):
    s = _silu(cond_ref[...]).astype(BF16)
    o_ref[...] = jnp.dot(s, w_ref[...].astype(BF16), preferred_element_type=F32) + b_ref[...]


def _modulation(cond, w_mod, b_mod):
    tn = 1536
    return pl.pallas_call(
        _mod_body,
        grid=(N_LAYERS, 6 * D // tn),
        in_specs=[
            pl.BlockSpec((N_COND, D), lambda l, n: (0, 0)),
            pl.BlockSpec((None, D, tn), lambda l, n: (l, 0, n)),
            pl.BlockSpec((None, 1, tn), lambda l, n: (l, 0, n)),
        ],
        out_specs=pl.BlockSpec((None, N_COND, tn), lambda l, n: (l, 0, n)),
        out_shape=jax.ShapeDtypeStruct((N_LAYERS, N_COND, 6 * D), F32),
        compiler_params=_params("arbitrary", "arbitrary"),
        name="modulation",
    )(cond, w_mod, b_mod.reshape(N_LAYERS, 1, 6 * D))


def _mod_spec(layer, tm):
    return pl.BlockSpec((None, None, 1, 6 * D), lambda i, *_: (layer, _cond_row(i, tm), 0, 0))


def _token_pair(h, tm):
    n_ctx_tiles = N_CTX // tm
    h_ctx, h_lat, lat_off = (h[0], h[1], 0) if isinstance(h, tuple) else (h, h, n_ctx_tiles)
    specs = [pl.BlockSpec((tm, D), lambda i, *_: (jnp.minimum(i, n_ctx_tiles - 1), 0)),
             pl.BlockSpec((tm, D), lambda i, *_: (jnp.maximum(i - n_ctx_tiles, 0) + lat_off, 0))]
    return (h_ctx, h_lat), specs


def _split_specs(tm, width):
    n_ctx_tiles = N_CTX // tm
    return [pl.BlockSpec((tm, width), lambda i, *_: (jnp.minimum(i, n_ctx_tiles - 1), 0)),
            pl.BlockSpec((tm, width), lambda i, *_: (jnp.maximum(i - n_ctx_tiles, 0), 0))]


def _qkv_body(hc_ref, hl_ref, mod_ref, g_ref, w_ref, qn_ref, kn_ref, bd_ref, cos_ref, sin_ref,
              q_ref, kc_ref, kl_ref, vc_ref, vl_ref, w_scr, xn_scr, qkv_scr, sq_scr, ss_scr, *, nq, nk, dh):
    is_lat = pl.program_id(0) >= N_CTX // TM

    @pl.when(pl.program_id(0) == 0)
    def _():
        w_scr[...] = w_ref[...].astype(BF16)

    mod = mod_ref[...]
    row_blocks = [slice(r, r + QKV_RB) for r in range(0, TM, QKV_RB)]
    for rb in row_blocks:
        h = jnp.where(is_lat, hl_ref[rb, :], hc_ref[rb, :])
        xn_scr[rb, :] = _norm_mod(h, g_ref[...], mod[:, 0:D], mod[:, D:2 * D]).astype(BF16)
    qkv_scr[...] = jnp.dot(xn_scr[...], w_scr[...], preferred_element_type=F32)

    bd = bd_ref[...]
    quarter = dh // 4
    lane = lax.broadcasted_iota(jnp.int32, (QKV_RB, LANES), 1)
    first = (lane % (2 * quarter)) < quarter
    inv_dh = 1.0 / dh

    def head_norm_rope(c0, gain, out_ref, o0):
        for rb in row_blocks:
            z = qkv_scr[rb, c0:c0 + 256]
            sq_scr[rb, :] = (z * z).astype(BF16)
        ss_scr[...] = jnp.dot(sq_scr[...], bd, preferred_element_type=F32)
        for rb in row_blocks:
            zn = qkv_scr[rb, c0:c0 + 256] * lax.rsqrt(ss_scr[rb, :] * inv_dh + EPS)
            for j in range(2):
                zz = zn[:, LANES * j:LANES * (j + 1)] * gain
                partner = jnp.where(first, pltpu.roll(zz, LANES - quarter, 1), pltpu.roll(zz, quarter, 1))
                zz = zz * cos_ref[rb, :] + partner * sin_ref[rb, :]
                out_ref[rb, o0 + LANES * j:o0 + LANES * (j + 1)] = zz.astype(out_ref.dtype)

    qgain = qn_ref[...] * (dh ** -0.5 * LOG2E)
    kgain = kn_ref[...]

    def finish(k_ref, v_ref):
        for c in range(nq // 256):
            head_norm_rope(256 * c, qgain, q_ref, 256 * c)
        for c in range(nk // 256):
            head_norm_rope(nq + 256 * c, kgain, k_ref, 256 * c)
        for rb in row_blocks:
            v_ref[rb, :] = qkv_scr[rb, nq + nk:]

    pl.when(is_lat)(functools.partial(finish, kl_ref, vl_ref))
    pl.when(jnp.logical_not(is_lat))(functools.partial(finish, kc_ref, vc_ref))


def _rope_tables(dh):
    half, quarter = dh // 2, dh // 4
    freqs = 1.0 / (ROPE_THETA ** (np.arange(quarter, dtype=np.float32) / quarter))
    pos = np.arange(LAT_S)
    row = (pos // GRID_W).astype(np.float32)
    col = (pos % GRID_W).astype(np.float32)
    ang_r = (row[:, None] * freqs[None, :]).astype(np.float32)
    ang_c = (col[:, None] * freqs[None, :]).astype(np.float32)
    cos = np.concatenate([np.cos(ang_r), np.cos(ang_r), np.cos(ang_c), np.cos(ang_c)], axis=1)
    sin = np.concatenate([-np.sin(ang_r), np.sin(ang_r), -np.sin(ang_c), np.sin(ang_c)], axis=1)
    reps = LANES // dh
    cos = np.tile(cos.astype(np.float32), (1, reps))
    sin = np.tile(sin.astype(np.float32), (1, reps))
    cos = np.concatenate([np.ones_like(cos), cos], axis=0)
    sin = np.concatenate([np.zeros_like(sin), sin], axis=0)
    return jnp.asarray(cos), jnp.asarray(sin)


def _block_diag_ones(dh):
    idx = np.arange(256) // dh
    return jnp.asarray((idx[:, None] == idx[None, :]).astype(np.float32), dtype=BF16)


def _qkv_proj(h, mods, layer, gain, w_qkv, q_norm, k_norm, n_heads, n_kv, dh):
    nq, nk = n_heads * dh, n_kv * dh
    nqkv = nq + 2 * nk
    cos, sin = _rope_tables(dh)
    reps = LANES // dh
    n_ctx_tiles = N_CTX // TM
    lat_tiles = LAT_S // TM

    def rope_idx(i):
        return (jnp.where(i < n_ctx_tiles, 0, lat_tiles + (i - n_ctx_tiles) % lat_tiles), 0)

    h_ops, h_specs = _token_pair(h, TM)
    kv_specs = _split_specs(TM, nk)
    kv_shapes = [jax.ShapeDtypeStruct((N_CTX, nk), F32), jax.ShapeDtypeStruct((N_LAT, nk), F32)]
    return pl.pallas_call(
        functools.partial(_qkv_body, nq=nq, nk=nk, dh=dh),
        grid=(N_TOK // TM,),
        in_specs=h_specs + [
            _mod_spec(layer, TM),
            pl.BlockSpec((1, D), lambda i: (0, 0)),
            pl.BlockSpec((D, nqkv), lambda i: (0, 0)),
            pl.BlockSpec((1, LANES), lambda i: (0, 0)),
            pl.BlockSpec((1, LANES), lambda i: (0, 0)),
            pl.BlockSpec((256, 256), lambda i: (0, 0)),
            pl.BlockSpec((TM, LANES), rope_idx),
            pl.BlockSpec((TM, LANES), rope_idx),
        ],
        out_specs=[pl.BlockSpec((TM, nq), lambda i: (i, 0))] + kv_specs + kv_specs,
        out_shape=[jax.ShapeDtypeStruct((N_TOK, nq), BF16)] + kv_shapes + kv_shapes,
        scratch_shapes=[pltpu.VMEM((D, nqkv), BF16), pltpu.VMEM((TM, D), BF16), pltpu.VMEM((TM, nqkv), F32),
                        pltpu.VMEM((TM, 256), BF16),
                        pltpu.VMEM((TM, 256), F32)],
        compiler_params=_params("arbitrary"),
        name=f"qkv_l{layer}",
    )(*h_ops, mods.reshape(N_LAYERS, N_COND, 1, 6 * D), gain.reshape(1, D), w_qkv,
      jnp.tile(q_norm, reps).reshape(1, LANES), jnp.tile(k_norm, reps).reshape(1, LANES),
      _block_diag_ones(dh), cos, sin)


def _attend(q, segs, sink_of, tq, scr):
    m_rows = q.shape[0]
    if not scr:
        assert all(bias is None for _, _, bias in segs)
        scores = [lax.dot_general(q, k, (((1,), (1,)), ((), ())), preferred_element_type=F32) for k, _, _ in segs]
        m = functools.reduce(jnp.maximum, [jnp.max(s, axis=-1, keepdims=True) for s in scores])
        head = lax.broadcasted_iota(jnp.int32, (m_rows, 1), 0) // tq
        sink = None
        if sink_of(0) is not None:
            sink = functools.reduce(lambda acc, r: jnp.where(head >= r, sink_of(r * tq), acc),
                                    range(1, m_rows // tq), jnp.full((m_rows, 1), sink_of(0), F32))
            m = jnp.maximum(m, sink)
        den = None if sink is None else jnp.exp2(sink - m)
        acc = None
        for s, (_, v, _) in zip(scores, segs):
            p = jnp.exp2(s - m)
            ps = jnp.sum(p, axis=-1, keepdims=True)
            pv = jnp.dot(p.astype(BF16), v, preferred_element_type=F32)
            den = ps if den is None else den + ps
            acc = pv if acc is None else acc + pv
        return acc / den
    s_scr, p_scr, r_scr = scr
    cols = []
    col = 0
    for k, _, _ in segs:
        t = k.shape[0]
        s_scr[0:m_rows, col:col + t] = lax.dot_general(q, k, (((1,), (1,)), ((), ())), preferred_element_type=F32)
        cols.append((col, t))
        col += t
    rows = ATT_RB * max(1, ATT_KEYS // col)
    for r0 in range(0, m_rows, rows):
        rb = slice(r0, r0 + rows)
        parts = []
        for (c0, t), (_, _, bias) in zip(cols, segs):
            s = s_scr[rb, c0:c0 + t]
            if bias is not None:
                s = s + bias[r0 % tq:r0 % tq + rows, :]
            parts.append(s)
        m = functools.reduce(jnp.maximum, [jnp.max(s, axis=-1, keepdims=True) for s in parts])
        sink = sink_of(r0)
        if sink is not None:
            m = jnp.maximum(m, sink)
        den = None if sink is None else jnp.exp2(sink - m)
        for (c0, t), s in zip(cols, parts):
            p = jnp.exp2(s - m)
            ps = jnp.sum(p, axis=-1, keepdims=True)
            den = ps if den is None else den + ps
            p_scr[rb, c0:c0 + t] = p.astype(BF16)
        r_scr[rb, :] = jnp.broadcast_to(den, (rows, LANES))
    acc = None
    for (c0, t), (_, v, _) in zip(cols, segs):
        pv = jnp.dot(p_scr[0:m_rows, c0:c0 + t], v, preferred_element_type=F32)
        acc = pv if acc is None else acc + pv
    return acc / r_scr[0:m_rows, :]


def _dup_halves(x, kv):
    lane = lax.broadcasted_iota(jnp.int32, x.shape, 1)
    r = pltpu.roll(x, 64, 1)
    lo = lane < 64
    return (jnp.where(lo, x, r) if kv % 2 == 0 else jnp.where(lo, r, x)).astype(BF16)


def _dup_segment(k, v, kv):
    c = LANES * (kv // 2)
    return _dup_halves(k[:, c:c + LANES], kv), _dup_halves(v[:, c:c + LANES], kv)


def _stage_rows(dst_ref, rows, fn):
    for kv in range(4):
        for r in range(0, rows, STAGE_RB):
            dst_ref[kv, r:r + STAGE_RB, :] = fn(slice(r, r + STAGE_RB), kv)


def _attend_heads64(q_ref, o_ref, sink_ref, segs_of, biases, tq, scr):
    lane = lax.broadcasted_iota(jnp.int32, (tq, LANES), 1)
    lo = lane < 64
    for kv in range(4):
        segs = [(k, v, b) for (k, v), b in zip(segs_of(kv), biases)]
        parts = []
        for pair in range(2):
            j = 2 * kv + pair
            qp = q_ref[:, LANES * j:LANES * (j + 1)]
            zero = jnp.zeros_like(qp)
            parts += [jnp.where(lo, qp, zero), jnp.where(lo, zero, qp)]
        sinks = [sink_ref[4 * kv + r] * LOG2E for r in range(4)]
        out = _attend(jnp.concatenate(parts, axis=0), segs, lambda r0: sinks[r0 // tq], tq, scr)
        for pair in range(2):
            j = 2 * kv + pair
            a = out[(2 * pair) * tq:(2 * pair + 1) * tq]
            b = out[(2 * pair + 1) * tq:(2 * pair + 2) * tq]
            o_ref[:, LANES * j:LANES * (j + 1)] = jnp.where(lo, a, b).astype(BF16)


def _attend_heads128(q_ref, o_ref, segs_of, tq, scr):
    for kv in range(4):
        segs = [(k, v, None) for k, v in segs_of(kv)]
        q = jnp.concatenate([q_ref[:, LANES * (2 * kv):LANES * (2 * kv + 1)],
                             q_ref[:, LANES * (2 * kv + 1):LANES * (2 * kv + 2)]], axis=0)
        out = _attend(q, segs, lambda r0: None, tq, scr)
        o_ref[:, LANES * (2 * kv):LANES * (2 * kv + 1)] = out[:tq].astype(BF16)
        o_ref[:, LANES * (2 * kv + 1):LANES * (2 * kv + 2)] = out[tq:].astype(BF16)


def _ctx_attn_a_body(sink_ref, q_ref, k_ref, v_ref, o_ref, *scr):
    _attend_heads64(q_ref, o_ref, sink_ref, lambda kv: [_dup_segment(k_ref[...], v_ref[...], kv)], [None],
                    CTX_S, scr)


def _cols128(ref, kv):
    return ref[:, LANES * kv:LANES * (kv + 1)].astype(BF16)


def _ctx_attn_c_body(q_ref, k_ref, v_ref, o_ref, *scr):
    _attend_heads128(q_ref, o_ref, lambda kv: [(_cols128(k_ref, kv), _cols128(v_ref, kv))], CTX_S, scr)


def _lat_attn_a_body(sink_ref, q_ref, k_ref, v_ref, ck_ref, cv_ref, o_ref, ck_scr, cv_scr, k_scr, v_scr,
                     bias_scr, *scr):
    qi = pl.program_id(1)

    @pl.when(qi == 0)
    def _():
        for src, dst, rows in ((ck_ref, ck_scr, PAST), (cv_ref, cv_scr, PAST),
                               (k_ref, k_scr, LAT_S), (v_ref, v_scr, LAT_S)):
            _stage_rows(dst, rows, lambda rb, kv, src=src: _dup_halves(
                src[rb, LANES * (kv // 2):LANES * (kv // 2 + 1)], kv))

    band = TQ + 2 * WINDOW
    ws = pl.multiple_of(jnp.clip(qi * TQ - WINDOW, 0, LAT_S - band), WINDOW)
    qpos = qi * TQ + lax.broadcasted_iota(jnp.int32, (TQ, band), 0)
    kpos = ws + lax.broadcasted_iota(jnp.int32, (TQ, band), 1)
    bias_scr[...] = jnp.where(jnp.abs(qpos - kpos) <= WINDOW, 0.0, NEG_BIG)
    segs_of = lambda kv: [(ck_scr[kv], cv_scr[kv]), (k_scr[kv, pl.ds(ws, band), :], v_scr[kv, pl.ds(ws, band), :])]
    _attend_heads64(q_ref, o_ref, sink_ref, segs_of, [None, bias_scr], TQ, scr)


def _lat_attn_c_body(q_ref, k_ref, v_ref, ck_ref, cv_ref, o_ref, ck_scr, cv_scr, k_scr, v_scr, *scr):
    @pl.when(pl.program_id(1) == 0)
    def _():
        for src, dst, rows in ((ck_ref, ck_scr, PAST), (cv_ref, cv_scr, PAST),
                               (k_ref, k_scr, LAT_S), (v_ref, v_scr, LAT_S)):
            _stage_rows(dst, rows, lambda rb, kv, src=src: src[rb, LANES * kv:LANES * (kv + 1)].astype(BF16))

    _attend_heads128(q_ref, o_ref, lambda kv: [(ck_scr[kv], cv_scr[kv]), (k_scr[kv], v_scr[kv])], TQ, scr)


_SMEM_SPEC = pl.BlockSpec(memory_space=pltpu.SMEM)


def _attention(q, k_ctx, k_lat, v_ctx, v_lat, cache_k, cache_v, sink, layer):
    nk = k_ctx.shape[1]
    ck = cache_k.reshape(LAT_B, PAST, nk)
    cv = cache_v.reshape(LAT_B, PAST, nk)
    ctx_specs = [
        pl.BlockSpec((CTX_S, D), lambda b: (b, 0)),
        pl.BlockSpec((CTX_S, nk), lambda b: (b, 0)),
        pl.BlockSpec((CTX_S, nk), lambda b: (b, 0)),
    ]
    qb = LAT_S // TQ
    lat_specs = [
        pl.BlockSpec((TQ, D), lambda b, i: (N_CTX // TQ + b * qb + i, 0)),
        pl.BlockSpec((LAT_S, nk), lambda b, i: (b, 0)),
        pl.BlockSpec((LAT_S, nk), lambda b, i: (b, 0)),
        pl.BlockSpec((None, PAST, nk), lambda b, i: (b, 0, 0)),
        pl.BlockSpec((None, PAST, nk), lambda b, i: (b, 0, 0)),
    ]
    heads_per_kv = D // nk
    band = TQ + 2 * WINDOW

    def softmax_scratch(m_rows, keys):
        return [pltpu.VMEM((m_rows, keys), F32), pltpu.VMEM((m_rows, keys), BF16), pltpu.VMEM((m_rows, LANES), F32)]

    lat_keys = PAST + (band if sink is not None else LAT_S)
    staged = [pltpu.VMEM((4, PAST, LANES), BF16), pltpu.VMEM((4, PAST, LANES), BF16),
              pltpu.VMEM((4, LAT_S, LANES), BF16), pltpu.VMEM((4, LAT_S, LANES), BF16)]
    if sink is not None:
        staged.append(pltpu.VMEM((TQ, band), F32))
    ctx_out = dict(out_specs=pl.BlockSpec((CTX_S, D), lambda b: (b, 0)),
                   out_shape=jax.ShapeDtypeStruct((N_CTX, D), BF16),
                   grid=(CTX_B,), compiler_params=_params("arbitrary"))
    lat_out = dict(out_specs=pl.BlockSpec((TQ, D), lambda b, i: (b * qb + i, 0)),
                   out_shape=jax.ShapeDtypeStruct((N_LAT, D), BF16),
                   scratch_shapes=staged + softmax_scratch(heads_per_kv * TQ, lat_keys),
                   grid=(LAT_B, qb), compiler_params=_params("arbitrary", "arbitrary"))
    if sink is not None:
        o_ctx = pl.pallas_call(_ctx_attn_a_body, in_specs=[_SMEM_SPEC] + ctx_specs,
                               name=f"attn_ctx_l{layer}", **ctx_out)(sink, q, k_ctx, v_ctx)
        o_lat = pl.pallas_call(_lat_attn_a_body, in_specs=[_SMEM_SPEC] + lat_specs,
                               name=f"attn_lat_l{layer}", **lat_out)(sink, q, k_lat, v_lat, ck, cv)
    else:
        o_ctx = pl.pallas_call(_ctx_attn_c_body, in_specs=ctx_specs,
                               name=f"attn_ctx_l{layer}", **ctx_out)(q, k_ctx, v_ctx)
        o_lat = pl.pallas_call(_lat_attn_c_body, in_specs=lat_specs,
                               name=f"attn_lat_l{layer}", **lat_out)(q, k_lat, v_lat, ck, cv)
    return o_ctx, o_lat


FFN_CHUNK = CTX_S
FFN_SLOT = FFN_CHUNK + 16
FFN_NCHUNK = TM_FFN // FFN_CHUNK


def _ffn_body(*refs, n_h, n_out, mixer):
    h_refs, refs = refs[:n_h], refs[n_h:]
    if mixer:
        (oc_ref, ol_ref, wo_ref), refs = refs[:3], refs[3:]
    (mod_ref, g_ref, wg_ref, wv_ref, bg_ref, bv_ref, kg_ref, kv_ref, cg_ref, cv_ref, wd_ref) = refs[:11]
    out_refs, refs = refs[11:11 + n_out], refs[11 + n_out:]
    xn_scr, acc_scr, ug_scr, uv_scr, wg_scr, wv_scr, wd_scr = refs
    i = pl.program_id(0)
    f = pl.program_id(1)
    is_lat = i >= N_CTX // TM_FFN
    out_cases = [(None, out_refs[0])] if n_out == 1 else [(jnp.logical_not(is_lat), out_refs[0]),
                                                           (is_lat, out_refs[1])]

    def guarded(cond, extra, fn):
        pl.when(extra if cond is None else jnp.logical_and(extra, cond))(fn)

    def load_h():
        if n_h == 1:
            return h_refs[0][...]
        return jnp.where(is_lat, h_refs[1][...], h_refs[0][...])

    def start(out_ref):
        mod = mod_ref[...]
        x = load_h()
        if mixer:
            heads = jnp.where(is_lat, ol_ref[...], oc_ref[...])
            x = x + mod[:, 2 * D:3 * D] * jnp.dot(heads, wo_ref[...], preferred_element_type=F32)
            out_ref[...] = x
        xn_scr[...] = _norm_mod(x, g_ref[...], mod[:, 3 * D:4 * D], mod[:, 4 * D:5 * D]).astype(BF16)
        acc_scr[...] = jnp.zeros_like(acc_scr)

    for cond, out_ref in out_cases:
        guarded(cond, f == 0, functools.partial(start, out_ref))

    wg_scr[...] = wg_ref[...].astype(BF16)
    wv_scr[...] = wv_ref[...].astype(BF16)
    wd_scr[...] = wd_ref[...].astype(BF16)
    base = [8 + FFN_SLOT * k for k in range(FFN_NCHUNK)]

    for k in range(FFN_NCHUNK):
        xk = xn_scr[FFN_CHUNK * k:FFN_CHUNK * (k + 1), :]
        ug_scr[base[k]:base[k] + FFN_CHUNK, :] = jnp.dot(xk, wg_scr[...], preferred_element_type=F32)
        uv_scr[base[k]:base[k] + FFN_CHUNK, :] = jnp.dot(xk, wv_scr[...], preferred_element_type=F32)

    for scr, b_ref in ((ug_scr, bg_ref), (uv_scr, bv_ref)):
        pad = -b_ref[...]
        tops = [pad] + [jnp.where(is_lat, scr[base[k - 1] + FFN_CHUNK - 1:base[k - 1] + FFN_CHUNK, :], pad)
                        for k in range(1, FFN_NCHUNK)]
        bots = [jnp.where(is_lat, scr[base[k + 1]:base[k + 1] + 1, :], pad)
                for k in range(FFN_NCHUNK - 1)] + [pad]
        for k in range(FFN_NCHUNK):
            scr[base[k] - 1:base[k], :] = tops[k]
            scr[base[k] + FFN_CHUNK:base[k] + FFN_CHUNK + 1, :] = bots[k]

    def conv(scr, b_ref, k_ref, c_ref, k):
        kk = k_ref[...]
        const = c_ref[...] + b_ref[...] * (kk[0:1] + kk[1:2] + kk[2:3])
        lo = base[k]
        return (const + kk[0:1] * scr[lo - 1:lo - 1 + FFN_CHUNK, :] + kk[1:2] * scr[lo:lo + FFN_CHUNK, :]
                + kk[2:3] * scr[lo + 1:lo + 1 + FFN_CHUNK, :])

    for k in range(FFN_NCHUNK):
        gate = conv(ug_scr, bg_ref, kg_ref, cg_ref, k)
        val = conv(uv_scr, bv_ref, kv_ref, cv_ref, k)
        a = (_silu(gate) * val).astype(BF16)
        rows = slice(FFN_CHUNK * k, FFN_CHUNK * (k + 1))
        acc_scr[rows, :] += jnp.dot(a, wd_scr[...], preferred_element_type=F32)

    def write(out_ref):
        x = out_ref[...] if mixer else load_h()
        out_ref[...] = x + mod_ref[:, 5 * D:6 * D] * acc_scr[...]

    for cond, out_ref in out_cases:
        guarded(cond, f == pl.num_programs(1) - 1, functools.partial(write, out_ref))


def _ffn(h, mods, layer, gain, w_up, b_up, conv_k, conv_b, w_down, split_out=False, mixer=None):
    nf = D_FF // TF
    b3 = b_up.reshape(N_LAYERS, 1, 2 * D_FF)
    c3 = conv_b.reshape(N_LAYERS, 1, 2 * D_FF)
    col = lambda off: (lambda i, f: (layer, 0, off + f))
    if isinstance(h, tuple):
        h_ops, h_specs = _token_pair(h, TM_FFN)
    else:
        h_ops, h_specs = (h,), [pl.BlockSpec((TM_FFN, D), lambda i, f: (i, 0))]
    if mixer is not None:
        o_ops, o_specs = _token_pair(mixer[:2], TM_FFN)
        h_ops = tuple(h_ops) + tuple(o_ops) + (mixer[2].astype(BF16),)
        h_specs = h_specs + o_specs + [pl.BlockSpec((D, D), lambda i, f: (0, 0))]
    if split_out:
        out_specs = _split_specs(TM_FFN, D)
        out_shape = [jax.ShapeDtypeStruct((N_CTX, D), F32), jax.ShapeDtypeStruct((N_LAT, D), F32)]
    else:
        out_specs = pl.BlockSpec((TM_FFN, D), lambda i, f: (i, 0))
        out_shape = jax.ShapeDtypeStruct((N_TOK, D), F32)
    slot_rows = 8 + FFN_SLOT * FFN_NCHUNK
    result = pl.pallas_call(
        functools.partial(_ffn_body, n_h=len(h_ops) - (3 if mixer is not None else 0),
                          n_out=2 if split_out else 1, mixer=mixer is not None),
        grid=(N_TOK // TM_FFN, nf),
        in_specs=h_specs + [
            _mod_spec(layer, TM_FFN),
            pl.BlockSpec((1, D), lambda i, f: (0, 0)),
            pl.BlockSpec((None, D, TF), col(0)),
            pl.BlockSpec((None, D, TF), col(nf)),
            pl.BlockSpec((None, 1, TF), col(0)),
            pl.BlockSpec((None, 1, TF), col(nf)),
            pl.BlockSpec((None, 3, TF), col(0)),
            pl.BlockSpec((None, 3, TF), col(nf)),
            pl.BlockSpec((None, 1, TF), col(0)),
            pl.BlockSpec((None, 1, TF), col(nf)),
            pl.BlockSpec((None, TF, D), lambda i, f: (layer, f, 0)),
        ],
        out_specs=out_specs,
        out_shape=out_shape,
        scratch_shapes=[pltpu.VMEM((TM_FFN, D), BF16), pltpu.VMEM((TM_FFN, D), F32),
                        pltpu.VMEM((slot_rows, TF), F32), pltpu.VMEM((slot_rows, TF), F32),
                        pltpu.VMEM((D, TF), BF16), pltpu.VMEM((D, TF), BF16), pltpu.VMEM((TF, D), BF16)],
        compiler_params=_params("arbitrary", "arbitrary"),
        name=f"ffn_l{layer}",
    )(*h_ops, mods.reshape(N_LAYERS, N_COND, 1, 6 * D), gain.reshape(1, D), w_up, w_up, b3, b3,
      conv_k, conv_k, c3, c3, w_down)
    return tuple(result) if split_out else result


S5_SEQ = 8
S5_CB = 4
CTX_SETS = CTX_B // S5_SEQ
CTX_CHUNKS = CTX_S // S5_L
LAT_CHUNKS = LAT_S // S5_L
S5_CTX_STEPS = CTX_SETS * CTX_CHUNKS // S5_CB
S5_LAT_STEPS = LAT_CHUNKS // S5_CB
S5_STEPS = S5_CTX_STEPS + S5_LAT_STEPS
S5_ROWS = (N_CTX + N_LAT) // S5_L
S5_CTX_ROWS = N_CTX // S5_L


def _s5_tile_specs():
    per = CTX_CHUNKS // S5_CB
    blk = (S5_SEQ, S5_CB, S5_L, D)

    def ctx_idx(n):
        m = jnp.minimum(n, S5_CTX_STEPS - 1)
        return (m // per, m % per, 0, 0)

    return blk, ctx_idx, (lambda n: (jnp.maximum(n - S5_CTX_STEPS, 0)))


def _s5_mod8(mods, layer):
    m = mods[layer]
    return jnp.stack([jnp.broadcast_to(m[0:1], (S5_SEQ, 6 * D)), m[LAT_ROW0:LAT_ROW0 + LAT_B]]).reshape(
        2, S5_SEQ, 1, 6 * D)


def _s5_in_body(hc_ref, hl_ref, mod_ref, g_ref, u_ref):
    is_lat = pl.program_id(0) >= S5_CTX_STEPS
    mod = mod_ref[...]
    gain = g_ref[...]
    for c in range(S5_CB):
        x = jnp.where(is_lat, hl_ref[:, c], hc_ref[:, c])
        u_ref[c] = _norm_mod(x, gain, mod[:, :, 0:D], mod[:, :, D:2 * D]).astype(BF16)


def _s5_in(h, mod8, gain):
    blk, ctx_idx, lat_idx = _s5_tile_specs()
    return pl.pallas_call(
        _s5_in_body,
        grid=(S5_STEPS,),
        in_specs=[pl.BlockSpec(blk, ctx_idx),
                  pl.BlockSpec(blk, lambda n: (N_CTX // N_LAT, lat_idx(n), 0, 0)),
                  pl.BlockSpec((None, S5_SEQ, 1, 6 * D), lambda n: (jnp.where(n >= S5_CTX_STEPS, 1, 0), 0, 0, 0)),
                  pl.BlockSpec((1, D), lambda n: (0, 0))],
        out_specs=pl.BlockSpec((None, S5_CB, S5_SEQ, S5_L, D), lambda n: (n, 0, 0, 0, 0)),
        out_shape=jax.ShapeDtypeStruct((S5_STEPS, S5_CB, S5_SEQ, S5_L, D), BF16),
        compiler_params=_params("arbitrary"),
        name="s5_in",
    )(h.reshape(N_TOK // CTX_S, CTX_CHUNKS, S5_L, D), h.reshape(N_TOK // LAT_S, LAT_CHUNKS, S5_L, D),
      mod8, gain.reshape(1, D))


def _s5_weights_body(lamc_re_ref, lamc_im_ref, lamr_re_ref, lamr_im_ref, ldt_ref,
                     bt_re_ref, bt_im_ref, ct_re_ref, ct_im_ref, w_ref, a_ref):
    blk = (lax.broadcasted_iota(jnp.int32, (1, 256), 1) // S5_L).astype(F32)
    lane256 = lax.broadcasted_iota(jnp.int32, (S5_C, 256), 1)
    hi = lax.Precision.HIGHEST
    krow = []
    st_rows = {}
    w_rows = {}
    for d in range(2):
        dt = jnp.exp(ldt_ref[d])
        lr = lamc_re_ref[d]
        li = lamc_im_ref[d]

        ang = (li * dt) * blk
        ph_r, ph_i = jnp.cos(ang), jnp.sin(ang)
        mag = jnp.exp((lr * dt) * blk)
        asc_r, asc_i = mag * ph_r, mag * ph_i
        top_r = ph_r[:, S5_L * (S5_L - 1):S5_L * (S5_L - 1) + 1]
        top_i = ph_i[:, S5_L * (S5_L - 1):S5_L * (S5_L - 1) + 1]
        mag = jnp.exp((lr * dt) * (15.0 - blk))
        dsc_r = mag * (top_r * ph_r + top_i * ph_i)
        dsc_i = mag * (top_i * ph_r - top_r * ph_i)
        ar = asc_r[:, S5_L:S5_L + 1]
        ai = asc_i[:, S5_L:S5_L + 1]
        den = lr * lr + li * li
        n_re = ar - 1.0
        f_re = (n_re * lr + ai * li) / den
        f_im = (ai * lr - n_re * li) / den
        btr = jnp.tile(bt_re_ref[d], (1, S5_L))
        bti = jnp.tile(bt_im_ref[d], (1, S5_L))
        bbr = f_re * btr - f_im * bti
        bbi = f_re * bti + f_im * btr
        e0r, e0i = (asc_r, asc_i) if d == 0 else (dsc_r, dsc_i)
        e1r = e0r * ar - e0i * ai
        e1i = e0r * ai + e0i * ar
        pr, pi = (dsc_r, dsc_i) if d == 0 else (asc_r, asc_i)
        st_rows[("re", d)] = pr * bbr - pi * bbi
        st_rows[("im", d)] = pr * bbi + pi * bbr
        ctr = jnp.tile(ct_re_ref[d].T, (1, S5_L))
        cti = jnp.tile(ct_im_ref[d].T, (1, S5_L))
        k_re = ctr * e0r - cti * e0i
        k_imneg = -(ctr * e0i + cti * e0r)
        w_rows[("re", d)] = ctr * e1r - cti * e1i
        w_rows[("im", d)] = -(ctr * e1i + cti * e1r)
        lrr = lamr_re_ref[d:d + 1, :]
        lir = lamr_im_ref[d:d + 1, :]
        magr = jnp.exp(lrr * dt)
        arr = magr * jnp.cos(lir * dt)
        air = magr * jnp.sin(lir * dt)
        denr = lrr * lrr + lir * lir
        nr = arr - 1.0
        fr = (nr * lrr + air * lir) / denr
        fi = (air * lrr - nr * lir) / denr
        b_re_t, b_im_t = bt_re_ref[d].T, bt_im_ref[d].T
        bbr_row = fr * b_re_t - fi * b_im_t
        bbi_row = fr * b_im_t + fi * b_re_t
        krow.append(jnp.dot(bbr_row, k_re, precision=hi, preferred_element_type=F32)
                    + jnp.dot(bbi_row, k_imneg, precision=hi, preferred_element_type=F32))
        mag16 = jnp.exp(lrr * dt * 16.0)
        a_ref[2 * d:2 * d + 1, :] = mag16 * jnp.cos(lir * dt * 16.0)
        a_ref[2 * d + 1:2 * d + 2, :] = mag16 * jnp.sin(lir * dt * 16.0)
    a_ref[4:8, :] = jnp.zeros((4, S5_P), F32)

    t_rows = []
    for s in range(S5_L):
        fwd = krow[0] if s == 0 else jnp.where(lane256 >= S5_C * s, pltpu.roll(krow[0], S5_C * s, 1), 0.0)
        bwd = krow[1] if s == S5_L - 1 else jnp.where(lane256 < S5_C * (s + 1),
                                                      pltpu.roll(krow[1], S5_C * (s + 1), 1), 0.0)
        t_rows.append(fwd + bwd)
    order = [("re", 0), ("re", 1), ("im", 0), ("im", 1)]
    w_ref[0:256, :] = jnp.concatenate(t_rows, axis=0).T.astype(BF16)
    w_ref[256:512, :] = jnp.concatenate([st_rows[o] for o in order], axis=0).astype(BF16)
    w_ref[512:768, :] = jnp.concatenate([w_rows[o] for o in order], axis=0).T.astype(BF16)


def _s5_weights(lam_re, lam_im, log_dt, b_re, b_im, c_re, c_im):
    col = lambda x: x.transpose(1, 0, 2).reshape(S5_G, 2, S5_P, 1)
    row = lambda x: x.transpose(1, 0, 2)
    ldt = log_dt.transpose(1, 0).reshape(S5_G, 2, 1, 1)
    per_group = lambda x: x.transpose(1, 0, 2, 3)
    g4 = lambda *tail: pl.BlockSpec((None,) + tail, lambda g: (g,) + (0,) * len(tail))
    return pl.pallas_call(
        _s5_weights_body,
        grid=(S5_G,),
        in_specs=[g4(2, S5_P, 1), g4(2, S5_P, 1), g4(2, S5_P), g4(2, S5_P), g4(2, 1, 1),
                  g4(2, S5_P, S5_C), g4(2, S5_P, S5_C), g4(2, S5_C, S5_P), g4(2, S5_C, S5_P)],
        out_specs=[g4(768, 256), g4(8, S5_P)],
        out_shape=[jax.ShapeDtypeStruct((S5_G, 768, 256), BF16),
                   jax.ShapeDtypeStruct((S5_G, 8, S5_P), F32)],
        compiler_params=_params("arbitrary"),
        name="s5_weights",
    )(col(lam_re), col(lam_im), row(lam_re), row(lam_im), ldt,
      per_group(b_re), per_group(b_im), per_group(c_re), per_group(c_im))


def _s5_scan(d_scr, r0, a_re, a_im, init_re, init_im, hin_scr, n_chunks):
    lane = lax.broadcasted_iota(jnp.int32, (S5_SEQ, LANES), 1)
    fwd = lane < S5_P
    hr, hi = init_re, init_im
    for c in range(n_chunks):
        cf = slice(r0 + c * S5_SEQ, r0 + (c + 1) * S5_SEQ)
        cb = slice(r0 + (n_chunks - 1 - c) * S5_SEQ, r0 + (n_chunks - c) * S5_SEQ)
        hin_scr[cf, 0:S5_P] = hr[:, 0:S5_P]
        hin_scr[cb, S5_P:LANES] = hr[:, S5_P:LANES]
        hin_scr[cf, LANES:LANES + S5_P] = hi[:, 0:S5_P]
        hin_scr[cb, LANES + S5_P:2 * LANES] = hi[:, S5_P:LANES]
        dr = jnp.where(fwd, d_scr[cf, 0:LANES], d_scr[cb, 0:LANES])
        di = jnp.where(fwd, d_scr[cf, LANES:2 * LANES], d_scr[cb, LANES:2 * LANES])
        hr, hi = hr * a_re - hi * a_im + dr, hr * a_im + hi * a_re + di
    return hr, hi


def _s5_core_body(ut_ref, w_ref, a_ref, s0_ref, yt_ref, fin_ref, ts_scr, d_scr, hin_scr):
    for gi in range(S5_GB):
        a_re = a_ref[gi, 0:1, :]
        a_im = a_ref[gi, 1:2, :]
        ch = slice(S5_C * gi, S5_C * (gi + 1))
        rhs = ut_ref[:, ch, :].reshape(S5_L * S5_C, S5_ROWS)
        ts_scr[gi] = jnp.dot(w_ref[gi, 0:512, :], rhs, preferred_element_type=F32)
        d_scr[gi] = ts_scr[gi, 256:512, :].T
        zero = jnp.zeros((S5_SEQ, LANES), F32)
        for hb in range(CTX_SETS):
            fr, fi = _s5_scan(d_scr.at[gi], hb * CTX_CHUNKS * S5_SEQ, a_re, a_im, zero, zero, hin_scr.at[gi],
                              CTX_CHUNKS)
            fin_ref[gi, S5_SEQ * hb:S5_SEQ * (hb + 1), 0:LANES] = fr
            fin_ref[gi, S5_SEQ * hb:S5_SEQ * (hb + 1), LANES:2 * LANES] = fi
        _s5_scan(d_scr.at[gi], S5_CTX_ROWS, a_re, a_im, s0_ref[gi, :, 0:LANES], s0_ref[gi, :, LANES:2 * LANES],
                 hin_scr.at[gi], LAT_CHUNKS)
        y = ts_scr[gi, 0:256, :] + lax.dot_general(w_ref[gi, 512:768, :], hin_scr[gi].astype(BF16),
                                                   (((1,), (1,)), ((), ())), preferred_element_type=F32)
        yt_ref[:, ch, :] = y.astype(BF16).reshape(S5_L, S5_C, S5_ROWS)


def _s5_core(ut, wall, avec, s0):
    g3 = lambda a, b: pl.BlockSpec((S5_GB, a, b), lambda g: (g, 0, 0))
    tok = pl.BlockSpec((S5_L, S5_GB * S5_C, S5_ROWS), lambda g: (0, g, 0))
    return pl.pallas_call(
        _s5_core_body,
        grid=(S5_G // S5_GB,),
        in_specs=[tok, g3(768, 256), g3(8, LANES), g3(LAT_B, 256)],
        out_specs=[tok, g3(CTX_B, 256)],
        out_shape=[jax.ShapeDtypeStruct((S5_L, D, S5_ROWS), BF16),
                   jax.ShapeDtypeStruct((S5_G, CTX_B, 256), F32)],
        scratch_shapes=[pltpu.VMEM((S5_GB, 512, S5_ROWS), F32), pltpu.VMEM((S5_GB, S5_ROWS, 256), F32),
                        pltpu.VMEM((S5_GB, S5_ROWS, 256), F32)],
        compiler_params=_params("arbitrary"),
        name="s5_core",
    )(ut, wall, avec, s0)


def _gelu_tanh(x):
    return 0.5 * x * (1.0 + jnp.tanh(math.sqrt(2.0 / math.pi) * (x + 0.044715 * (x * x * x))))


def _s5_out_body(hc_ref, hl_ref, y_ref, mod_ref, g_ref, dskip_ref, w_ref, oc_ref, ol_ref, w_scr, out_scr):
    n = pl.program_id(0)
    is_lat = n >= S5_CTX_STEPS

    @pl.when(n == 0)
    def _():
        w_scr[...] = w_ref[...].astype(BF16)

    mod = mod_ref[...]
    gain = g_ref[...]
    dskip = dskip_ref[...]
    rows = S5_SEQ * S5_L
    for half in range(0, S5_CB, 2):
        hs, gs = [], []
        for c in (half, half + 1):
            h = jnp.where(is_lat, hl_ref[:, c], hc_ref[:, c])
            u = _norm_mod(h, gain, mod[:, :, 0:D], mod[:, :, D:2 * D])
            y = u * dskip + y_ref[c].astype(F32)
            hs.append(h)
            gs.append(_gelu_tanh(y).astype(BF16).reshape(rows, D))
        hh = jnp.dot(jnp.concatenate(gs, axis=0), w_scr[...], preferred_element_type=F32)
        for j, c in enumerate((half, half + 1)):
            blk = hh[rows * j:rows * (j + 1)]
            mix = (blk[:, 0:D] * jax.nn.sigmoid(blk[:, D:2 * D])).reshape(S5_SEQ, S5_L, D)
            out_scr[c] = hs[j] + mod[:, :, 2 * D:3 * D] * mix

    def emit(out_ref):
        for c in range(S5_CB):
            out_ref[:, c] = out_scr[c]

    pl.when(is_lat)(functools.partial(emit, ol_ref))
    pl.when(jnp.logical_not(is_lat))(functools.partial(emit, oc_ref))


def _s5_out(h, y, mod8, gain, d_skip, w_glu):
    blk, ctx_idx, lat_idx = _s5_tile_specs()
    h_ctx, h_lat = pl.pallas_call(
        _s5_out_body,
        grid=(S5_STEPS,),
        in_specs=[pl.BlockSpec(blk, ctx_idx),
                  pl.BlockSpec(blk, lambda n: (N_CTX // N_LAT, lat_idx(n), 0, 0)),
                  pl.BlockSpec((None, S5_CB, S5_SEQ, S5_L, D), lambda n: (n, 0, 0, 0, 0)),
                  pl.BlockSpec((None, S5_SEQ, 1, 6 * D), lambda n: (jnp.where(n >= S5_CTX_STEPS, 1, 0), 0, 0, 0)),
                  pl.BlockSpec((1, D), lambda n: (0, 0)),
                  pl.BlockSpec((1, D), lambda n: (0, 0)),
                  pl.BlockSpec((D, 2 * D), lambda n: (0, 0))],
        out_specs=[pl.BlockSpec(blk, ctx_idx),
                   pl.BlockSpec(blk, lambda n: (0, lat_idx(n), 0, 0))],
        out_shape=[jax.ShapeDtypeStruct((CTX_B, CTX_CHUNKS, S5_L, D), F32),
                   jax.ShapeDtypeStruct((LAT_B, LAT_CHUNKS, S5_L, D), F32)],
        scratch_shapes=[pltpu.VMEM((D, 2 * D), BF16), pltpu.VMEM((S5_CB, S5_SEQ, S5_L, D), F32)],
        compiler_params=_params("arbitrary"),
        name="s5_out",
    )(h.reshape(N_TOK // CTX_S, CTX_CHUNKS, S5_L, D), h.reshape(N_TOK // LAT_S, LAT_CHUNKS, S5_L, D),
      y, mod8, gain.reshape(1, D), d_skip.reshape(1, D), w_glu)
    return h_ctx.reshape(N_CTX, D), h_lat.reshape(N_LAT, D)


def _s5_mixer(h, mods, layer, gain, state, lam_re, lam_im, log_dt, b_re, b_im, c_re, c_im, d_skip, w_glu):
    mod8 = _s5_mod8(mods, layer)
    u = _s5_in(h, mod8, gain)
    ut = u.reshape(S5_ROWS, S5_L, D).transpose(1, 2, 0)
    wall, avec = _s5_weights(lam_re, lam_im, log_dt, b_re, b_im, c_re, c_im)
    a2 = jnp.stack([jnp.concatenate([avec[:, 0], avec[:, 2]], axis=-1),
                    jnp.concatenate([avec[:, 1], avec[:, 3]], axis=-1)], axis=1)
    a2 = jnp.pad(a2, ((0, 0), (0, 6), (0, 0)))
    s0 = state.transpose(3, 0, 2, 1, 4).reshape(S5_G, LAT_B, 4 * S5_P)
    yt, fin = _s5_core(ut, wall, a2, s0)
    y = yt.transpose(2, 0, 1).reshape(S5_STEPS, S5_CB, S5_SEQ, S5_L, D)
    new_state = fin.reshape(S5_G, CTX_B, 2, 2, S5_P).transpose(1, 3, 2, 0, 4)
    return _s5_out(h, y, mod8, gain, d_skip, w_glu), new_state


def kernel(x_prompt, x_sample, cache_l0_k, cache_l0_v, state_l1, cache_l2_k, cache_l2_v, cache_l3_k, cache_l3_v, c, c_ctx, norm1, norm2, w_mod, b_mod, w_up, b_up, conv_k, conv_b, w_down, l0_w_qkv, l0_q_norm, l0_k_norm, l0_sink, l0_w_o, l1_lam_re, l1_lam_im, l1_log_dt, l1_b_re, l1_b_im, l1_c_re, l1_c_im, l1_d_skip, l1_w_glu, l2_w_qkv, l2_q_norm, l2_k_norm, l2_w_o, l3_w_qkv, l3_q_norm, l3_k_norm, l3_sink, l3_w_o):
    h = (x_prompt.reshape(N_CTX, D), x_sample.reshape(N_LAT, D))
    cond = jnp.concatenate([c_ctx[None, :], jnp.zeros((LAT_ROW0 - 1, D), F32), c], axis=0)
    mods = _modulation(cond, w_mod, b_mod)

    attn_layers = {
        0: (l0_w_qkv, l0_q_norm, l0_k_norm, l0_sink, l0_w_o, cache_l0_k, cache_l0_v, 16, 4, 64),
        2: (l2_w_qkv, l2_q_norm, l2_k_norm, None, l2_w_o, cache_l2_k, cache_l2_v, 8, 4, 128),
        3: (l3_w_qkv, l3_q_norm, l3_k_norm, l3_sink, l3_w_o, cache_l3_k, cache_l3_v, 16, 4, 64),
    }
    new_kv = {}
    new_state = None
    for layer in range(N_LAYERS):
        if layer in attn_layers:
            w_qkv, q_norm, k_norm, sink, w_o, ck, cv, n_heads, n_kv, dh = attn_layers[layer]
            q, k_ctx, k_lat, v_ctx, v_lat = _qkv_proj(h, mods, layer, norm1[layer], w_qkv, q_norm, k_norm,
                                                      n_heads, n_kv, dh)
            new_kv[layer] = (k_ctx.reshape(CTX_B, CTX_S, n_kv, dh), v_ctx.reshape(CTX_B, CTX_S, n_kv, dh))
            mixer = _attention(q, k_ctx, k_lat, v_ctx, v_lat, ck, cv, sink, layer) + (w_o,)
        else:
            mixer = None
            h, new_state = _s5_mixer(h, mods, layer, norm1[layer], state_l1, l1_lam_re, l1_lam_im, l1_log_dt,
                                     l1_b_re, l1_b_im, l1_c_re, l1_c_im, l1_d_skip, l1_w_glu)
        h = _ffn(h, mods, layer, norm2[layer], w_up, b_up, conv_k, conv_b, w_down,
                 split_out=layer == N_LAYERS - 1, mixer=mixer)

    y_prompt = h[0].reshape(CTX_B, CTX_S, D)
    y_sample = h[1].reshape(LAT_B, LAT_S, D)
    return (y_prompt, y_sample, new_kv[0][0], new_kv[0][1], new_state,
            new_kv[2][0], new_kv[2][1], new_kv[3][0], new_kv[3][1])
```
